```python
import jax, jax.numpy as jnp
from jax import lax
import numpy as np

D_MODEL = 2048
BATCH = 1
SEQ = 8192
DEPTH = 2
DEC_BATCH = 32
DEC_SEQ = 4
PAST_LEN = 8192
PAGE_SIZE = 128

N_META = 16
R_HEADS = 16
R_HD = 64
RW = R_HEADS * R_HD
F_HEADS = 16
F_HD = 64
FW = F_HEADS * F_HD
DECAY_LORA = 96
AAA_LORA = 96
MV_LORA = 64
SHIFT_W = 3 * RW + DECAY_LORA + AAA_LORA
IN_SIZES = (SHIFT_W, RW, FW, FW, FW, F_HEADS, FW, D_MODEL, D_MODEL)
IN_W = SHIFT_W + RW + 4 * FW + F_HEADS + 2 * D_MODEL
Q_BLOCK = 128
NORM_EPS = 1e-6
LNX_EPS = 1e-5 * R_HD
DECAY_OFFSET = 0.5
POOL_EXTRA_DIV = 4

kernel_name = 'rwkv7_fox_gated_hybrid_step'


def rms_norm(x, gain):
    xf = x.astype(jnp.float32)
    y = xf * lax.rsqrt(jnp.mean(xf * xf, axis=-1, keepdims=True) + NORM_EPS)
    return (y * gain.astype(jnp.float32)).astype(x.dtype)


def split_cols(z, sizes):
    cuts, acc = [], 0
    for s in sizes[:-1]:
        acc += s
        cuts.append(acc)
    return jnp.split(z, cuts, axis=-1)


def heads(t, n_heads):
    return t.reshape(t.shape[:-1] + (n_heads, t.shape[-1] // n_heads))


def mixer_inputs(x, gain, w_in):
    return split_cols(rms_norm(x, gain) @ w_in, IN_SIZES)


def merge_branches(x, o_r, o_f, m_r, m_f, w_pr, w_pf, w_o):
    mixed = jax.nn.sigmoid(m_r) * (o_r @ w_pr) + jax.nn.sigmoid(m_f) * (o_f @ w_pf)
    return x + mixed @ w_o


def wkv_scan(s0, r, decay, k, v, kk, a):
    def step(S, inp):
        r_t, w_t, k_t, v_t, kk_t, a_t = inp
        sa = jnp.einsum('bhvk,bhk->bhv', S, -kk_t)
        S = (S * w_t[:, :, None, :] + sa[..., None] * (kk_t * a_t)[:, :, None, :]
             + v_t[..., None] * k_t[:, :, None, :])
        return S, jnp.einsum('bhvk,bhk->bhv', S, r_t)
    xs = tuple(jnp.moveaxis(t, 1, 0) for t in (r, decay, k, v, kk, a))
    S, ys = lax.scan(step, s0, xs)
    return S, jnp.moveaxis(ys, 0, 1)


def rwkv_branch(zs, z_prev, gate, s0, v_first, rp, vres):
    mu, w0, w2, a0, a2, k_k, k_a, r_k, lnx_w, lnx_b = rp
    f32 = jnp.float32
    prev = jnp.concatenate([z_prev[:, None, :].astype(zs.dtype), zs[:, :-1]], axis=1)
    xs = zs + (prev - zs) * mu.astype(zs.dtype)
    r, k, v, wd, ad = split_cols(xs, (RW, RW, RW, DECAY_LORA, AAA_LORA))
    w_log = -jax.nn.softplus(-(w0 + jnp.tanh(wd) @ w2).astype(f32)) - DECAY_OFFSET
    decay = jnp.exp(-jnp.exp(w_log))
    a = jax.nn.sigmoid((a0 + ad @ a2).astype(f32))
    if vres is None:
        v_first = v
    else:
        v0, v1, v2 = vres
        v = v + (v_first - v) * jax.nn.sigmoid(v0 + (v @ v1) @ v2)
    r, k, v = r.astype(f32), k.astype(f32), v.astype(f32)
    kk = heads(k * k_k, R_HEADS)
    kk = kk / jnp.maximum(jnp.sqrt(jnp.sum(kk * kk, axis=-1, keepdims=True)), 1e-12)
    k = k * (1.0 + (a - 1.0) * k_a)
    rh, kh, vh = heads(r, R_HEADS), heads(k, R_HEADS), heads(v, R_HEADS)
    s_fin, o = wkv_scan(s0.astype(f32), rh, heads(decay, R_HEADS), kh, vh, kk, heads(a, R_HEADS))
    mean = jnp.mean(o, axis=-1, keepdims=True)
    var = jnp.mean(jnp.square(o - mean), axis=-1, keepdims=True)
    o = (o - mean) * lax.rsqrt(var + LNX_EPS) * heads(lnx_w, R_HEADS) + heads(lnx_b, R_HEADS)
    o = o + jnp.sum(rh * kh * r_k, axis=-1, keepdims=True) * vh
    out = o.reshape(o.shape[:2] + (RW,)).astype(zs.dtype) * jax.nn.silu(gate)
    return out, s_fin, zs[:, -1], v_first


def fox_prep(q, k, v, f, fb, qg, kg):
    f32 = jnp.float32
    qh = rms_norm(heads(q, F_HEADS), qg).astype(f32)
    kh = rms_norm(heads(k, F_HEADS), kg).astype(f32)
    vh = heads(v, F_HEADS).astype(f32)
    logf = jax.nn.log_sigmoid((f + fb).astype(f32))
    return qh, kh, vh, logf


def fox_attend(q, k, v, cq, ck, qpos, kpos):
    s = jnp.einsum('bqhd,bkhd->bhqk', q, k) * (F_HD ** -0.5)
    s = s + jnp.swapaxes(cq, 1, 2)[:, :, :, None] - jnp.swapaxes(ck, 1, 2)[:, :, None, :]
    s = jnp.where((kpos[None, :] <= qpos[:, None])[None, None], s, -jnp.inf)
    p = jax.nn.softmax(s, axis=-1)
    return jnp.einsum('bhqk,bkhd->bqhd', p, v)


def fox_prompt(q, k, v, logf):
    B, L = q.shape[0], q.shape[1]
    T = L - N_META
    nb = T // Q_BLOCK
    c = jnp.cumsum(logf, axis=1)
    pos = jnp.arange(L)
    o_meta = fox_attend(q[:, :N_META], k[:, :N_META], v[:, :N_META], c[:, :N_META], c[:, :N_META],
                        pos[:N_META], pos[:N_META])
    qb = jnp.moveaxis(q[:, N_META:].reshape(B, nb, Q_BLOCK, F_HEADS, F_HD), 1, 0)
    cb = jnp.moveaxis(c[:, N_META:].reshape(B, nb, Q_BLOCK, F_HEADS), 1, 0)
    pb = pos[N_META:].reshape(nb, Q_BLOCK)
    ob = lax.map(lambda t: fox_attend(t[0], k, v, t[1], c, t[2], pos), (qb, cb, pb))
    o_real = jnp.moveaxis(ob, 0, 1).reshape(B, T, F_HEADS, F_HD)
    return jnp.concatenate([o_meta, o_real], axis=1)


def fox_sample(q, k, v, logf, k_pool, v_pool, lf_pool, page_table):
    f32 = jnp.float32
    DB, TS = q.shape[0], q.shape[1]
    P = page_table.shape[1] * PAGE_SIZE
    k_past = k_pool[page_table].reshape(DB, P, F_HEADS, F_HD).astype(f32)
    v_past = v_pool[page_table].reshape(DB, P, F_HEADS, F_HD).astype(f32)
    c_past = jnp.cumsum(lf_pool[page_table].reshape(DB, P, F_HEADS).astype(f32), axis=1)
    c_new = c_past[:, -1:] + jnp.cumsum(logf, axis=1)
    k_all = jnp.concatenate([k_past, k], axis=1)
    v_all = jnp.concatenate([v_past, v], axis=1)
    c_all = jnp.concatenate([c_past, c_new], axis=1)
    return fox_attend(q, k_all, v_all, c_new, c_all, P + jnp.arange(TS), jnp.arange(P + TS))


def setup_inputs(seed: int = 0) -> dict:
    key = jax.random.key(seed)
    keys = iter(jax.random.split(key, 40))

    def nrm(shape, scale):
        return jax.random.normal(next(keys), shape, jnp.float32) * scale

    def near_one(shape):
        return 1.0 + nrm(shape, 0.1)

    n_pages = PAST_LEN // PAGE_SIZE
    n_used = DEC_BATCH * n_pages
    n_pool = n_used + n_used // POOL_EXTRA_DIV
    page_table = jax.random.permutation(next(keys), n_pool)[:n_used].reshape(DEC_BATCH, n_pages).astype(jnp.int32)
    return {
        'x_prompt': nrm((BATCH, SEQ, D_MODEL), 1.0),
        'x_sample': nrm((DEC_BATCH, DEC_SEQ, D_MODEL), 1.0),
        'cache_k': nrm((DEPTH, n_pool, PAGE_SIZE, F_HEADS, F_HD), 1.0),
        'cache_v': nrm((DEPTH, n_pool, PAGE_SIZE, F_HEADS, F_HD), 1.0),
        'cache_logf': jax.nn.log_sigmoid(nrm((DEPTH, n_pool, PAGE_SIZE, F_HEADS), 1.0) + 2.0),
        'state_wkv': nrm((DEPTH, DEC_BATCH, R_HEADS, R_HD, R_HD), 0.5),
        'state_shift': nrm((DEPTH, DEC_BATCH, SHIFT_W), 1.0),
        'page_table': page_table,
        'meta_tokens': nrm((N_META, D_MODEL), 1.0),
        'norm_gain': near_one((DEPTH, D_MODEL)),
        'w_in': nrm((DEPTH, D_MODEL, IN_W), D_MODEL ** -0.5),
        'r_mu': jax.random.uniform(next(keys), (DEPTH, SHIFT_W), jnp.float32),
        'r_w0': nrm((DEPTH, RW), 0.5),
        'r_w2': nrm((DEPTH, DECAY_LORA, RW), DECAY_LORA ** -0.5),
        'r_a0': nrm((DEPTH, RW), 0.1),
        'r_a2': nrm((DEPTH, AAA_LORA, RW), AAA_LORA ** -0.5),
        'r_v0': nrm((DEPTH - 1, RW), 0.1),
        'r_v1': nrm((DEPTH - 1, RW, MV_LORA), RW ** -0.5),
        'r_v2': nrm((DEPTH - 1, MV_LORA, RW), MV_LORA ** -0.5),
        'r_kk': near_one((DEPTH, RW)),
        'r_ka': near_one((DEPTH, RW)),
        'r_rk': nrm((DEPTH, R_HEADS, R_HD), 0.1),
        'r_lnx_w': near_one((DEPTH, RW)),
        'r_lnx_b': nrm((DEPTH, RW), 0.02),
        'f_bias': nrm((DEPTH, F_HEADS), 0.1),
        'f_qgain': near_one((DEPTH, F_HD)),
        'f_kgain': near_one((DEPTH, F_HD)),
        'w_proj_r': nrm((DEPTH, RW, D_MODEL), RW ** -0.5),
        'w_proj_f': nrm((DEPTH, FW, D_MODEL), FW ** -0.5),
        'w_out': nrm((DEPTH, D_MODEL, D_MODEL), D_MODEL ** -0.5),
    }


def reference(x_prompt, x_sample, cache_k, cache_v, cache_logf, state_wkv, state_shift, page_table,
              meta_tokens, norm_gain, w_in, r_mu, r_w0, r_w2, r_a0, r_a2, r_v0, r_v1, r_v2,
              r_kk, r_ka, r_rk, r_lnx_w, r_lnx_b, f_bias, f_qgain, f_kgain,
              w_proj_r, w_proj_f, w_out):
    f32 = jnp.float32
    B = x_prompt.shape[0]
    DB = x_sample.shape[0]
    dt = x_prompt.dtype
    xp = jnp.concatenate([jnp.broadcast_to(meta_tokens.astype(dt)[None], (B, N_META, D_MODEL)), x_prompt], axis=1)
    xs = x_sample
    vf_p = None
    vf_s = None
    kp, vp, lp, sp, hp = [], [], [], [], []
    ks, vs, ls, ss, hs = [], [], [], [], []
    for l in range(DEPTH):
        rp = (r_mu[l], r_w0[l], r_w2[l], r_a0[l], r_a2[l], r_kk[l], r_ka[l], r_rk[l], r_lnx_w[l], r_lnx_b[l])
        vres = None if l == 0 else (r_v0[l - 1], r_v1[l - 1], r_v2[l - 1])
        zs, g_r, q, k, v, f, g_f, m_r, m_f = mixer_inputs(xp, norm_gain[l], w_in[l])
        o_r, s_fin, sh_fin, vf_p = rwkv_branch(zs, jnp.zeros((B, SHIFT_W), zs.dtype), g_r,
                                               jnp.zeros((B, R_HEADS, R_HD, R_HD), f32), vf_p, rp, vres)
        qh, kh, vh, lf = fox_prep(q, k, v, f, f_bias[l], f_qgain[l], f_kgain[l])
        o_f = fox_prompt(qh, kh, vh, lf).reshape(B, -1, FW).astype(xp.dtype) * jax.nn.silu(g_f)
        xp = merge_branches(xp, o_r, o_f, m_r, m_f, w_proj_r[l], w_proj_f[l], w_out[l])
        kp.append(kh.astype(dt))
        vp.append(vh.astype(dt))
        lp.append(lf)
        sp.append(s_fin)
        hp.append(sh_fin)
        zs, g_r, q, k, v, f, g_f, m_r, m_f = mixer_inputs(xs, norm_gain[l], w_in[l])
        o_r, s_fin, sh_fin, vf_s = rwkv_branch(zs, state_shift[l], g_r, state_wkv[l], vf_s, rp, vres)
        qh, kh, vh, lf = fox_prep(q, k, v, f, f_bias[l], f_qgain[l], f_kgain[l])
        o_f = fox_sample(qh, kh, vh, lf, cache_k[l], cache_v[l], cache_logf[l], page_table)
        o_f = o_f.reshape(DB, -1, FW).astype(xs.dtype) * jax.nn.silu(g_f)
        xs = merge_branches(xs, o_r, o_f, m_r, m_f, w_proj_r[l], w_proj_f[l], w_out[l])
        ks.append(kh.astype(dt))
        vs.append(vh.astype(dt))
        ls.append(lf)
        ss.append(s_fin)
        hs.append(sh_fin)
    y_prompt = xp[:, N_META:]
    y_sample = xs
    return (y_prompt, y_sample,
            jnp.stack(kp), jnp.stack(vp), jnp.stack(lp), jnp.stack(sp), jnp.stack(hp),
            jnp.stack(ks), jnp.stack(vs), jnp.stack(ls), jnp.stack(ss), jnp.stack(hs))
```

```python
import functools

import jax
import jax.numpy as jnp
from jax import lax
from jax.experimental import pallas as pl
from jax.experimental.pallas import tpu as pltpu

F32 = jnp.float32
BF16 = jnp.bfloat16
HI = lax.Precision.HIGHEST

LANES = 128
HEAD_DIM = 64
N_META = 16
NORM_EPS = 1e-6
LNX_EPS = 1e-5 * HEAD_DIM
DECAY_OFFSET = 0.5
NEG_BIG = -1e30
VMEM_LIMIT = 56 * 1024 * 1024


def _cp(sem, vmem=VMEM_LIMIT):
    return pltpu.CompilerParams(dimension_semantics=sem, vmem_limit_bytes=vmem)


def _round_up(x, m):
    return (x + m - 1) // m * m


def _pick_tile(n, cap, mult=128):
    if n <= cap:
        return n
    best = mult
    t = mult
    while t <= cap:
        if n % t == 0:
            best = t
        t += mult
    return best


def _dot(a, b, precision=None):
    return jnp.dot(a, b, preferred_element_type=F32, precision=precision)


def _dot_nt(a, b, precision=None):
    return lax.dot_general(a, b, (((1,), (1,)), ((), ())), preferred_element_type=F32, precision=precision)


def _iota(shape, dim):
    return lax.broadcasted_iota(jnp.int32, shape, dim)


def _head_block_ones():
    return (_iota((LANES, LANES), 0) // HEAD_DIM == _iota((LANES, LANES), 1) // HEAD_DIM).astype(F32)


def _head_sums(x):
    bd = _head_block_ones()
    parts = [_dot(x[:, i * LANES:(i + 1) * LANES], bd, HI) for i in range(x.shape[1] // LANES)]
    return parts[0] if len(parts) == 1 else jnp.concatenate(parts, axis=-1)


def _softplus(x):
    return jnp.maximum(x, 0.0) + jnp.log(1.0 + jnp.exp(-jnp.abs(x)))


def _silu(x):
    return x * jax.nn.sigmoid(x)


def _inproj_kernel(x_ref, g_ref, w_ref, o_ref, xn_ref):
    @pl.when(pl.program_id(1) == 0)
    def _():
        x = x_ref[...]
        ms = jnp.mean(x * x, axis=-1, keepdims=True)
        xn_ref[...] = (x * lax.rsqrt(ms + NORM_EPS) * g_ref[...]).astype(BF16)

    o_ref[...] = _dot(xn_ref[...], w_ref[...])


def _inproj(x, gain, w):
    m, d = x.shape
    n = w.shape[1]
    tm = _pick_tile(m, 768)
    tn = _pick_tile(n, 1152)
    return pl.pallas_call(
        _inproj_kernel,
        out_shape=jax.ShapeDtypeStruct((m, n), F32),
        grid=(m // tm, n // tn),
        in_specs=[pl.BlockSpec((tm, d), lambda i, j: (i, 0)),
                  pl.BlockSpec((1, d), lambda i, j: (0, 0)),
                  pl.BlockSpec((d, tn), lambda i, j: (0, j))],
        out_specs=pl.BlockSpec((tm, tn), lambda i, j: (i, j)),
        scratch_shapes=[pltpu.VMEM((tm, d), BF16)],
        compiler_params=_cp(("parallel", "arbitrary")),
        name="inproj",
    )(x, gain, w)


def _rwkv_prep_math(z3, zw, p3, pw, vf, prm, has_vres):
    (mu3, muw, w0, w2, a0, a2, kkg, kag, v0, v1, v2) = prm
    rw = w0.shape[1]
    x3 = z3 + (p3 - z3) * mu3
    xw = zw + (pw - zw) * muw
    r = x3[:, :rw]
    k = x3[:, rw:2 * rw]
    v = x3[:, 2 * rw:]
    wd = xw[:, :LANES]
    ad = xw[:, LANES:]
    w_raw = w0 + _dot(jnp.tanh(wd).astype(BF16), w2)
    w_log = -_softplus(-w_raw) - DECAY_OFFSET
    lw = -jnp.exp(w_log)
    a = jax.nn.sigmoid(a0 + _dot(ad.astype(BF16), a2))
    if has_vres:
        lora = _dot(_dot(v.astype(BF16), v1).astype(BF16), v2)
        v = v + (vf - v) * jax.nn.sigmoid(v0 + lora)
    kk = k * kkg
    ss = _head_sums(kk * kk)
    kk = kk / jnp.maximum(jnp.sqrt(ss), 1e-12)
    k2 = k * (1.0 + (a - 1.0) * kag)
    return r, lw, k2, v, kk, kk * a


def _rwkv_prep_kernel(*refs, has_vres, chained):
    z3_ref, zw_ref, p3_ref, pw_ref = refs[:4]
    pos = 4
    vf = None
    if has_vres:
        vf = refs[pos][...]
        pos += 1
    prm = [r[...] for r in refs[pos:pos + 11]]
    outs = refs[pos + 11:]
    z3 = z3_ref[...]
    zw = zw_ref[...]
    if chained:
        first = pl.program_id(0) == 0
        row0 = _iota((z3.shape[0], 1), 0) == 0
        last3 = jnp.where(first, 0.0, p3_ref[7:8, :])
        lastw = jnp.where(first, 0.0, pw_ref[7:8, :])
        p3 = jnp.where(row0, last3, pltpu.roll(z3, 1, 0))
        pw = jnp.where(row0, lastw, pltpu.roll(zw, 1, 0))
    else:
        p3 = p3_ref[...]
        pw = pw_ref[...]
    res = _rwkv_prep_math(z3, zw, p3, pw, vf, prm, has_vres)
    for o_ref, val in zip(outs, res):
        o_ref[...] = val


def _rwkv_prep(z, prev3, prevw, vfirst, prm, *, rw, wa_col, chained):
    m = z.shape[0]
    tm = _pick_tile(m, 256)
    has_vres = vfirst is not None
    wa_blk = wa_col // (2 * LANES)
    if chained:
        sub = tm // 8
        p3_spec = pl.BlockSpec((8, 3 * rw), lambda i: (jnp.maximum(i * sub - 1, 0), 0))
        pw_spec = pl.BlockSpec((8, 2 * LANES), lambda i: (jnp.maximum(i * sub - 1, 0), wa_blk))
        prev_args = (z, z)
    else:
        p3_spec = pl.BlockSpec((tm, 3 * rw), lambda i: (i, 0))
        pw_spec = pl.BlockSpec((tm, 2 * LANES), lambda i: (i, 0))
        prev_args = (prev3, prevw)
    in_specs = [pl.BlockSpec((tm, 3 * rw), lambda i: (i, 0)),
                pl.BlockSpec((tm, 2 * LANES), lambda i: (i, wa_blk)),
                p3_spec, pw_spec]
    args = [z, z, *prev_args]
    if has_vres:
        in_specs.append(pl.BlockSpec((tm, rw), lambda i: (i, 0)))
        args.append(vfirst)
    for p in prm:
        in_specs.append(pl.BlockSpec(p.shape, lambda i: (0, 0)))
        args.append(p)
    out = pl.pallas_call(
        functools.partial(_rwkv_prep_kernel, has_vres=has_vres, chained=chained),
        out_shape=[jax.ShapeDtypeStruct((m, rw), F32)] * 6,
        grid=(m // tm,),
        in_specs=in_specs,
        out_specs=[pl.BlockSpec((tm, rw), lambda i: (i, 0))] * 6,
        compiler_params=_cp(("parallel",)),
        name="rwkv_prep",
    )(*args)
    return out


def _wkv_kernel(r_ref, lw_ref, k_ref, v_ref, kk_ref, b_ref, g_ref, s0_ref, rk_ref, lnw_ref, lnb_ref,
                o_ref, sfin_ref, s_sc, *, chunk, valid_len):
    c = pl.program_id(2)
    n_c = pl.num_programs(2)
    C = chunk
    C2 = 2 * C

    @pl.when(c == 0)
    def _():
        s_sc[...] = s0_ref[...]

    r = r_ref[...]
    lw = lw_ref[...]
    k = k_ref[...]
    v = v_ref[...]
    kk = kk_ref[...]
    b = b_ref[...]
    if valid_len is not None:
        ok = (c * C + _iota((C, 1), 0)) < valid_len
        lw = jnp.where(ok, lw, 0.0)
        k = jnp.where(ok, k, 0.0)
        kk = jnp.where(ok, kk, 0.0)
        b = jnp.where(ok, b, 0.0)

    ii = _iota((C, C), 0)
    jj = _iota((C, C), 1)
    g = _dot((ii >= jj).astype(F32), lw, HI)
    g_last = g[C - 1:C, :]
    e_g = jnp.exp(g)
    e_ng = jnp.exp(-g)
    e_tail = jnp.exp(g_last - g)
    a_t = -kk * jnp.exp(g - lw)
    r_t = r * e_g
    b_t = b * e_ng
    k_t = k * e_ng
    b_h = b * e_tail
    k_h = k * e_tail

    lane_lo = _iota((1, LANES), 1) < HEAD_DIM

    def stack(x):
        return jnp.concatenate([jnp.where(lane_lo, x, 0.0), jnp.where(lane_lo, 0.0, x)], axis=0)

    left = jnp.concatenate([stack(a_t), stack(r_t)], axis=0)
    right = jnp.concatenate([stack(b_t), stack(k_t)], axis=0)
    v2 = stack(v)
    sc = _dot_nt(left, right, HI)
    s_prev = s_sc[...]
    hs = _dot_nt(left, s_prev, HI)

    i2 = _iota((C2, C2), 0)
    j2 = _iota((C2, C2), 1)
    strict = i2 > j2
    incl = i2 >= j2
    m_ab = jnp.where(strict, sc[:C2, :C2], 0.0)
    m_ak = jnp.where(strict, sc[:C2, C2:], 0.0)
    p_rb = jnp.where(incl, sc[C2:, :C2], 0.0)
    p_rk = jnp.where(incl, sc[C2:, C2:], 0.0)

    rhs = hs[:C2] + _dot(m_ak, v2, HI)

    base = min(8, C)
    same = (i2 // base) == (j2 // base)
    mb = jnp.where(same, m_ab, 0.0)
    x = (i2 == j2).astype(F32) + mb
    pw = mb
    steps = 1
    while steps * 2 < base:
        pw = _dot(pw, pw, HI)
        x = x + _dot(x, pw, HI)
        steps *= 2
    n = base
    while n < C:
        off = jnp.where(((i2 // (2 * n)) == (j2 // (2 * n))) & ((i2 // n) != (j2 // n)), m_ab, 0.0)
        x = x + _dot(_dot(x, off, HI), x, HI)
        n *= 2

    u = _dot(x, rhs, HI)
    o2 = hs[C2:] + _dot(p_rb, u, HI) + _dot(p_rk, v2, HI)
    o = o2[:C] + o2[C:]

    s_new = s_prev * jnp.exp(g_last) + _dot(u.T, stack(b_h), HI) + _dot(v2.T, stack(k_h), HI)
    s_sc[...] = s_new

    @pl.when(c == n_c - 1)
    def _():
        sfin_ref[...] = s_new

    bd = _head_block_ones()
    mean = _dot(o, bd, HI) * (1.0 / HEAD_DIM)
    d = o - mean
    var = _dot(d * d, bd, HI) * (1.0 / HEAD_DIM)
    y = d * lax.rsqrt(var + LNX_EPS) * lnw_ref[...] + lnb_ref[...]
    bonus = _dot(r * k * rk_ref[...], bd, HI)
    y = y + bonus * v
    o_ref[...] = (y * _silu(g_ref[...])).astype(o_ref.dtype)


def _wkv(r, lw, k, v, kk, b, gate, gate_blk, s0, rk, lnw, lnb, *, chunk, valid_len):
    bsz, t, rw = r.shape
    npair = rw // LANES
    nchunk = t // chunk
    seq = pl.BlockSpec((None, chunk, LANES), lambda bi, p, c: (bi, c, p))
    par = pl.BlockSpec((1, LANES), lambda bi, p, c: (0, p))
    st = pl.BlockSpec((None, None, LANES, LANES), lambda bi, p, c: (bi, p, 0, 0))
    return pl.pallas_call(
        functools.partial(_wkv_kernel, chunk=chunk, valid_len=valid_len),
        out_shape=[jax.ShapeDtypeStruct((bsz, t, rw), BF16),
                   jax.ShapeDtypeStruct((bsz, npair, LANES, LANES), F32)],
        grid=(bsz, npair, nchunk),
        in_specs=[seq] * 6 + [pl.BlockSpec((None, chunk, LANES), lambda bi, p, c: (bi, c, gate_blk + p)),
                              st, par, par, par],
        out_specs=[seq, st],
        scratch_shapes=[pltpu.VMEM((LANES, LANES), F32)],
        compiler_params=_cp(("parallel", "parallel", "arbitrary")),
        name="wkv",
    )(r, lw, k, v, kk, b, gate, s0, rk, lnw, lnb)


def _pack_state(s):
    bsz, h, n, _ = s.shape
    s = s.reshape(bsz, h // 2, 2, n, n)
    z = jnp.zeros_like(s[:, :, 0])
    top = jnp.concatenate([s[:, :, 0], z], axis=-1)
    bot = jnp.concatenate([z, s[:, :, 1]], axis=-1)
    return jnp.concatenate([top, bot], axis=-2)


def _unpack_state(sp):
    bsz, p = sp.shape[:2]
    n = HEAD_DIM
    s = jnp.stack([sp[:, :, :n, :n], sp[:, :, n:, n:]], axis=2)
    return s.reshape(bsz, 2 * p, n, n)


def _split3(c):
    hi = c.astype(BF16)
    r1 = c - hi.astype(F32)
    mid = r1.astype(BF16)
    lo = (r1 - mid.astype(F32)).astype(BF16)
    return hi, mid, lo


def _fox_prep_kernel(*refs, augment):
    q_ref, k_ref, f_ref, fb_ref, qg_ref, kg_ref = refs[:6]
    q = q_ref[...]
    k = k_ref[...]
    qn = q * lax.rsqrt(_head_sums(q * q) * (1.0 / HEAD_DIM) + NORM_EPS) * qg_ref[...]
    kn = k * lax.rsqrt(_head_sums(k * k) * (1.0 / HEAD_DIM) + NORM_EPS) * kg_ref[...]
    logf = -_softplus(-(f_ref[...] + fb_ref[...]))
    if not augment:
        qn_ref, kn_ref, lf_ref = refs[6:]
        qn_ref[...] = qn
        kn_ref[...] = kn
        lf_ref[...] = logf
        return
    pq_ref, pk_ref, sq_ref, sk_ref, cq_ref, ck_ref, kn_ref, lf_ref, qa_ref, ka_ref, carry = refs[6:]

    @pl.when(pl.program_id(0) == 0)
    def _():
        carry[...] = jnp.zeros_like(carry)

    tm = q.shape[0]
    tri = (_iota((tm, tm), 0) >= _iota((tm, tm), 1)).astype(F32)
    c = carry[...] + _dot(tri, logf, HI)
    carry[...] = c[tm - 1:tm, :]
    c3 = jnp.concatenate(_split3(c), axis=-1)
    kn_ref[...] = kn
    lf_ref[...] = logf
    qa_ref[...] = (_dot(qn.astype(BF16), pq_ref[...]) + _dot(c3, sq_ref[...]) + cq_ref[...]).astype(BF16)
    ka_ref[...] = (_dot(kn.astype(BF16), pk_ref[...]) + _dot(c3, sk_ref[...]) + ck_ref[...]).astype(BF16)


def _fox_aug_consts(fw):
    nh = fw // HEAD_DIM
    rows = jnp.arange(fw)
    cols = (rows // HEAD_DIM) * LANES + rows % HEAD_DIM
    place = jnp.zeros((fw, nh * LANES), F32).at[rows, cols].set(1.0)
    pq = (place * HEAD_DIM ** -0.5).astype(BF16)
    pk = place.astype(BF16)
    h = jnp.arange(nh)
    sq = jnp.zeros((3 * LANES, nh * LANES), F32)
    sk = jnp.zeros((3 * LANES, nh * LANES), F32)
    cq = jnp.zeros((1, nh * LANES), F32)
    ck = jnp.zeros((1, nh * LANES), F32)
    for part in range(3):
        sq = sq.at[part * LANES + h, h * LANES + HEAD_DIM + part].set(1.0)
        sk = sk.at[part * LANES + h, h * LANES + HEAD_DIM + 3 + part].set(-1.0)
        cq = cq.at[0, h * LANES + HEAD_DIM + 3 + part].set(1.0)
        ck = ck.at[0, h * LANES + HEAD_DIM + part].set(1.0)
    return pq, pk, sq.astype(BF16), sk.astype(BF16), cq, ck


def _fox_prep(z, fb, qg, kg, *, fw, q_col, k_col, f_col, aug):
    m = z.shape[0]
    tm = _pick_tile(m, 256)
    qb, kb, fblk = q_col // fw, k_col // fw, f_col // LANES
    in_specs = [pl.BlockSpec((tm, fw), lambda i: (i, qb)),
                pl.BlockSpec((tm, fw), lambda i: (i, kb)),
                pl.BlockSpec((tm, LANES), lambda i: (i, fblk)),
                pl.BlockSpec((1, LANES), lambda i: (0, 0)),
                pl.BlockSpec((1, fw), lambda i: (0, 0)),
                pl.BlockSpec((1, fw), lambda i: (0, 0))]
    args = [z, z, z, fb, qg, kg]
    row = lambda w: pl.BlockSpec((tm, w), lambda i: (i, 0))
    if aug is None:
        return pl.pallas_call(
            functools.partial(_fox_prep_kernel, augment=False),
            out_shape=[jax.ShapeDtypeStruct((m, fw), F32), jax.ShapeDtypeStruct((m, fw), F32),
                       jax.ShapeDtypeStruct((m, LANES), F32)],
            grid=(m // tm,), in_specs=in_specs, out_specs=[row(fw), row(fw), row(LANES)],
            compiler_params=_cp(("parallel",)), name="fox_prep_s",
        )(*args)
    for a in aug:
        in_specs.append(pl.BlockSpec(a.shape, lambda i: (0, 0)))
        args.append(a)
    nh = fw // HEAD_DIM
    return pl.pallas_call(
        functools.partial(_fox_prep_kernel, augment=True),
        out_shape=[jax.ShapeDtypeStruct((m, fw), F32), jax.ShapeDtypeStruct((m, LANES), F32),
                   jax.ShapeDtypeStruct((m, nh * LANES), BF16), jax.ShapeDtypeStruct((m, nh * LANES), BF16)],
        grid=(m // tm,), in_specs=in_specs,
        out_specs=[row(fw), row(LANES), row(nh * LANES), row(nh * LANES)],
        scratch_shapes=[pltpu.VMEM((1, LANES), F32)],
        compiler_params=_cp(("arbitrary",)), name="fox_prep_p",
    )(*args)


def _flash_kernel(q_ref, k_ref, v_ref, g_ref, o_ref, acc_sc, *, tile):
    qi = pl.program_id(1)
    t = tile
    causal = _iota((t, t), 0) >= _iota((t, t), 1)
    outs = []
    for hh in range(2):
        q = q_ref[:, hh * LANES:(hh + 1) * LANES]
        acc_sc[...] = jnp.zeros_like(acc_sc)

        def step(kt, carry, masked):
            m, l = carry
            off = pl.multiple_of(kt * t, t)
            kb = k_ref[pl.ds(off, t), hh * LANES:(hh + 1) * LANES]
            vb = v_ref[pl.ds(off, t), :]
            s = _dot_nt(q, kb)
            if masked:
                s = jnp.where(causal, s, NEG_BIG)
            m_new = jnp.maximum(m, jnp.max(s, axis=-1, keepdims=True))
            alpha = jnp.exp(m - m_new)
            p = jnp.exp(s - m_new)
            l = alpha * l + jnp.sum(p, axis=-1, keepdims=True)
            acc_sc[...] = acc_sc[...] * alpha + _dot(p.astype(BF16), vb)
            return m_new, l

        init = (jnp.full((t, 1), NEG_BIG, F32), jnp.zeros((t, 1), F32))
        carry = lax.fori_loop(0, qi, lambda kt, cr: step(kt, cr, False), init)
        m, l = step(qi, carry, True)
        outs.append(acc_sc[...] / l)
    lane_lo = _iota((1, LANES), 1) < HEAD_DIM
    o = jnp.where(lane_lo, outs[0], outs[1])
    o_ref[...] = (o * _silu(g_ref[...])).astype(o_ref.dtype)


def _flash(qa, ka, vb, z, g_col):
    lp = qa.shape[0]
    npair = qa.shape[1] // (2 * LANES)
    t = _pick_tile(lp, 256)
    gblk = g_col // LANES
    return pl.pallas_call(
        functools.partial(_flash_kernel, tile=t),
        out_shape=jax.ShapeDtypeStruct((lp, npair * LANES), BF16),
        grid=(npair, lp // t),
        in_specs=[pl.BlockSpec((t, 2 * LANES), lambda p, i: (i, p)),
                  pl.BlockSpec((lp, 2 * LANES), lambda p, i: (0, p)),
                  pl.BlockSpec((lp, LANES), lambda p, i: (0, p)),
                  pl.BlockSpec((t, LANES), lambda p, i: (i, gblk + p))],
        out_specs=pl.BlockSpec((t, LANES), lambda p, i: (i, p)),
        scratch_shapes=[pltpu.VMEM((t, LANES), F32)],
        compiler_params=_cp(("parallel", "arbitrary")),
        name="fox_flash",
    )(qa, ka, vb, z)


def _cast_kernel(x_ref, o_ref):
    o_ref[...] = x_ref[...].astype(o_ref.dtype)


def _cast_cols(z, col, width, dtype):
    m = z.shape[0]
    tm = _pick_tile(m, 512)
    blk = col // width
    return pl.pallas_call(
        _cast_kernel,
        out_shape=jax.ShapeDtypeStruct((m, width), dtype),
        grid=(m // tm,),
        in_specs=[pl.BlockSpec((tm, width), lambda i: (i, blk))],
        out_specs=pl.BlockSpec((tm, width), lambda i: (i, 0)),
        compiler_params=_cp(("parallel",)), name="cast_cols",
    )(z)


def _decode_kernel(pt_ref, qbd_ref, k_ref, v_ref, lft_ref, kn_ref, vn_ref, lfn_ref, g_ref, o_ref,
                   m_sc, l_sc, acc_sc, r_sc, *, ts, nheads):
    del pt_ref
    j = pl.program_id(1)
    nj = pl.num_programs(1)
    nrow = nheads * ts
    page = k_ref.shape[0]

    @pl.when(j == 0)
    def _():
        m_sc[...] = jnp.full_like(m_sc, NEG_BIG)
        l_sc[...] = jnp.zeros_like(l_sc)
        acc_sc[...] = jnp.zeros_like(acc_sc)
        r_sc[...] = jnp.zeros_like(r_sc)

    qbd = qbd_ref[...]
    expand = (_iota((nrow, nheads), 0) // ts == _iota((nrow, nheads), 1)).astype(F32)

    def update(kpage, vpage, bias):
        s = _dot_nt(qbd, kpage.astype(BF16)) + bias
        m_old = m_sc[...]
        m_new = jnp.maximum(m_old, jnp.max(s, axis=-1, keepdims=True))
        alpha = jnp.exp(m_old - m_new)
        p = jnp.exp(s - m_new)
        l_sc[...] = alpha * l_sc[...] + jnp.sum(p, axis=-1, keepdims=True)
        acc_sc[...] = acc_sc[...] * alpha + _dot(p.astype(BF16), vpage.astype(BF16))
        m_sc[...] = m_new

    lft = lft_ref[...]
    jr = _iota((page, 2 * page), 0)
    jc = _iota((page, 2 * page), 1)
    tri2 = ((jr > jc) | (jc == page)).astype(F32)
    a2 = _dot(lft, tri2, HI)
    r_old = r_sc[...]
    bias = _dot(expand, a2[:, :page] + r_old, HI)
    r_sc[...] = r_old + a2[:, page:page + 1]
    update(k_ref[...], v_ref[...], bias)

    @pl.when(j == nj - 1)
    def _():
        kr = _iota((page, page), 0)
        kc = _iota((page, page), 1)
        cum = _dot(lfn_ref[...], (kr <= kc).astype(F32), HI)
        tq = _iota((nrow, page), 0) % ts
        tk = _iota((nrow, page), 1)
        bias_n = jnp.where((tk <= tq) & (tk < ts), -_dot(expand, cum, HI), NEG_BIG)
        update(kn_ref[...], vn_ref[...], bias_n)
        out = acc_sc[...] / l_sc[...]
        fw = out.shape[1]
        own = (_iota((nrow, fw), 0) // ts) == (_iota((nrow, fw), 1) // HEAD_DIM)
        sel = (_iota((8, nrow), 1) % ts == _iota((8, nrow), 0)).astype(F32)
        o = _dot(sel, jnp.where(own, out, 0.0), HI)
        o_ref[...] = (o * _silu(g_ref[...])).astype(o_ref.dtype)


def _decode(page_table, qbd, kpool, vpool, lftpool, layer, knew, vnew, lfn_t, gate, *, ts, nheads):
    bsz, nrow, fw = qbd.shape
    npg = page_table.shape[1]
    page = kpool.shape[2]
    pool = lambda w0, w1: pl.BlockSpec((None, None, w0, w1), lambda b, j, pt: (layer, pt[b, npg - 1 - j], 0, 0))
    per_b = lambda w0, w1: pl.BlockSpec((None, w0, w1), lambda b, j, pt: (b, 0, 0))
    grid_spec = pltpu.PrefetchScalarGridSpec(
        num_scalar_prefetch=1, grid=(bsz, npg),
        in_specs=[per_b(nrow, fw), pool(page, fw), pool(page, fw), pool(nheads, page),
                  per_b(page, fw), per_b(page, fw), per_b(nheads, page), per_b(8, fw)],
        out_specs=per_b(8, fw),
        scratch_shapes=[pltpu.VMEM((nrow, 1), F32), pltpu.VMEM((nrow, 1), F32),
                        pltpu.VMEM((nrow, fw), F32), pltpu.VMEM((nheads, 1), F32)])
    return pl.pallas_call(
        functools.partial(_decode_kernel, ts=ts, nheads=nheads),
        out_shape=jax.ShapeDtypeStruct((bsz, 8, fw), BF16),
        grid_spec=grid_spec,
        compiler_params=_cp(("parallel", "arbitrary")),
        name="fox_decode",
    )(page_table, qbd, kpool, vpool, lftpool, knew, vnew, lfn_t, gate)


def _merge_kernel(x_ref, or_ref, of_ref, mr_ref, mf_ref, wpr_ref, wpf_ref, wo_ref, o_ref):
    a = _dot(or_ref[...], wpr_ref[...])
    b = _dot(of_ref[...], wpf_ref[...])
    mixed = jax.nn.sigmoid(mr_ref[...]) * a + jax.nn.sigmoid(mf_ref[...]) * b
    o_ref[...] = x_ref[...] + _dot(mixed.astype(BF16), wo_ref[...])


def _merge(x, o_r, o_f, z, mr_col, mf_col, wpr, wpf, wo):
    m, d = x.shape
    rw = o_r.shape[1]
    fw = o_f.shape[1]
    tm = _pick_tile(m, 256)
    whole = lambda a: pl.BlockSpec(a.shape, lambda i: (0, 0), pipeline_mode=pl.Buffered(1))
    return pl.pallas_call(
        _merge_kernel,
        out_shape=jax.ShapeDtypeStruct((m, d), F32),
        grid=(m // tm,),
        in_specs=[pl.BlockSpec((tm, d), lambda i: (i, 0)),
                  pl.BlockSpec((tm, rw), lambda i: (i, 0)),
                  pl.BlockSpec((tm, fw), lambda i: (i, 0)),
                  pl.BlockSpec((tm, d), lambda i: (i, mr_col // d)),
                  pl.BlockSpec((tm, d), lambda i: (i, mf_col // d)),
                  whole(wpr), whole(wpf), whole(wo)],
        out_specs=pl.BlockSpec((tm, d), lambda i: (i, 0)),
        compiler_params=_cp(("parallel",)),
        name="merge",
    )(x, o_r, o_f, z, z, wpr, wpf, wo)


def _pad_cols(a, width):
    return jnp.pad(a, [(0, 0)] * (a.ndim - 1) + [(0, width - a.shape[-1])])


def kernel(x_prompt, x_sample, cache_k, cache_v, cache_logf, state_wkv, state_shift, page_table, meta_tokens, norm_gain, w_in, r_mu, r_w0, r_w2, r_a0, r_a2, r_v0, r_v1, r_v2, r_kk, r_ka, r_rk, r_lnx_w, r_lnx_b, f_bias, f_qgain, f_kgain, w_proj_r, w_proj_f, w_out):
    bsz, seq, d = x_prompt.shape
    db, ts, _ = x_sample.shape
    depth = w_in.shape[0]
    rw = r_w0.shape[1]
    rh = r_rk.shape[1]
    fh = f_bias.shape[1]
    fw = fh * HEAD_DIM
    lora_w = r_w2.shape[1]
    lora_a = r_a2.shape[1]
    shift_w = 3 * rw + lora_w + lora_a
    page = cache_k.shape[2]
    npool = cache_k.shape[1]
    assert bsz == 1 and rw == rh * HEAD_DIM and fw == rw and d == 2 * rw
    assert lora_w <= LANES and lora_a <= LANES and fh <= LANES and ts <= 8 and page == LANES

    s_gr = shift_w
    s_q = s_gr + rw
    s_k = s_q + fw
    s_v = s_k + fw
    s_f = s_v + fw
    s_gf = s_f + fh
    s_mr = s_gf + fw
    s_mf = s_mr + d
    c_gr = 3 * rw
    c_q = c_gr + rw
    c_k = c_q + fw
    c_v = c_k + fw
    c_gf = c_v + fw
    c_mr = c_gf + fw
    c_mf = c_mr + d
    c_wa = c_mf + d
    c_f = c_wa + 2 * LANES
    n_pack = c_f + LANES

    def pack_w(w):
        segs = [w[:, :3 * rw], w[:, s_gr:s_gr + rw], w[:, s_q:s_q + fw], w[:, s_k:s_k + fw], w[:, s_v:s_v + fw],
                w[:, s_gf:s_gf + fw], w[:, s_mr:s_mr + d], w[:, s_mf:s_mf + d],
                _pad_cols(w[:, 3 * rw:3 * rw + lora_w], LANES),
                _pad_cols(w[:, 3 * rw + lora_w:shift_w], LANES),
                _pad_cols(w[:, s_f:s_f + fh], LANES)]
        return jnp.concatenate(segs, axis=1).astype(BF16)

    def pack_shift(a):
        return a[..., :3 * rw], jnp.concatenate([_pad_cols(a[..., 3 * rw:3 * rw + lora_w], LANES),
                                                 _pad_cols(a[..., 3 * rw + lora_w:], LANES)], axis=-1)

    def unpack_shift(zrow):
        return jnp.concatenate([zrow[..., :3 * rw], zrow[..., c_wa:c_wa + lora_w],
                                zrow[..., c_wa + LANES:c_wa + LANES + lora_a]], axis=-1)

    length = seq + N_META
    lp = _round_up(length, 256)
    chunk = 64
    xp = jnp.concatenate([meta_tokens.astype(F32), x_prompt[0], jnp.zeros((lp - length, d), F32)], axis=0)
    xs = x_sample.reshape(db * ts, d)
    aug = _fox_aug_consts(fw)
    kpool = cache_k.reshape(depth, npool, page, fw)
    vpool = cache_v.reshape(depth, npool, page, fw)
    lftpool = jnp.swapaxes(cache_logf, 2, 3)
    s0_p = jnp.zeros((1, rh // 2, LANES, LANES), F32)

    vf_p = None
    vf_s = None
    outs = {n: [] for n in ("kp", "vp", "lp", "sp", "hp", "ks", "vs", "ls", "ss", "hs")}
    for l in range(depth):
        w_pack = pack_w(w_in[l])
        gain = norm_gain[l][None, :]
        mu3, muw = pack_shift(r_mu[l][None, :])
        if l == 0:
            v0 = jnp.zeros((1, rw), F32)
            v1 = jnp.zeros((rw, LANES), BF16)
            v2 = jnp.zeros((LANES, rw), BF16)
        else:
            v0 = r_v0[l - 1][None, :]
            v1 = _pad_cols(r_v1[l - 1], LANES).astype(BF16)
            v2 = jnp.pad(r_v2[l - 1], ((0, LANES - r_v2.shape[1]), (0, 0))).astype(BF16)
        prm = (mu3, muw, r_w0[l][None, :],
               jnp.pad(r_w2[l], ((0, LANES - lora_w), (0, 0))).astype(BF16),
               r_a0[l][None, :],
               jnp.pad(r_a2[l], ((0, LANES - lora_a), (0, 0))).astype(BF16),
               r_kk[l][None, :], r_ka[l][None, :], v0, v1, v2)
        rk = r_rk[l].reshape(1, rw)
        lnw = r_lnx_w[l][None, :]
        lnb = r_lnx_b[l][None, :]
        fb = _pad_cols(f_bias[l][None, :], LANES)
        qg = jnp.tile(f_qgain[l], fh)[None, :]
        kg = jnp.tile(f_kgain[l], fh)[None, :]
        wpr = w_proj_r[l].astype(BF16)
        wpf = w_proj_f[l].astype(BF16)
        wo = w_out[l].astype(BF16)

        z = _inproj(xp, gain, w_pack)
        r, lw, k2, v, kk, b = _rwkv_prep(z, None, None, vf_p, prm, rw=rw, wa_col=c_wa, chained=True)
        if l == 0:
            vf_p = v
        o_r, s_fin = _wkv(r[None], lw[None], k2[None], v[None], kk[None], b[None], z[None], c_gr // LANES,
                          s0_p, rk, lnw, lnb, chunk=chunk, valid_len=length)
        kn, lf, qa, ka = _fox_prep(z, fb, qg, kg, fw=fw, q_col=c_q, k_col=c_k, f_col=c_f, aug=aug)
        vb = _cast_cols(z, c_v, fw, BF16)
        o_f = _flash(qa, ka, vb, z, c_gf)
        xp = _merge(xp, o_r[0], o_f, z, c_mr, c_mf, wpr, wpf, wo)
        outs["kp"].append(kn[:length].reshape(1, length, fh, HEAD_DIM))
        outs["vp"].append(z[:length, c_v:c_v + fw].reshape(1, length, fh, HEAD_DIM))
        outs["lp"].append(lf[:length, :fh][None])
        outs["sp"].append(_unpack_state(s_fin))
        outs["hp"].append(unpack_shift(z[length - 1:length]))

        zs = _inproj(xs, gain, w_pack)
        zs3 = zs.reshape(db, ts, n_pack)
        sh3, shw = pack_shift(state_shift[l])
        prev3 = jnp.concatenate([sh3[:, None], zs3[:, :-1, :3 * rw]], axis=1).reshape(db * ts, 3 * rw)
        prevw = jnp.concatenate([shw[:, None], zs3[:, :-1, c_wa:c_wa + 2 * LANES]], axis=1).reshape(db * ts, 2 * LANES)
        res = _rwkv_prep(zs, prev3, prevw, vf_s, prm, rw=rw, wa_col=c_wa, chained=False)
        if l == 0:
            vf_s = res[3]
        pad_t = lambda a: jnp.pad(a.reshape(db, ts, -1), ((0, 0), (0, 8 - ts), (0, 0)))
        r, lw, k2, v, kk, b = (pad_t(a) for a in res)
        gate_r = pad_t(zs[:, c_gr:c_gr + rw])
        o_r, s_fin = _wkv(r, lw, k2, v, kk, b, gate_r, 0, _pack_state(state_wkv[l]), rk, lnw, lnb,
                          chunk=8, valid_len=None)
        o_r = o_r[:, :ts].reshape(db * ts, rw)
        qn, kn, lf = _fox_prep(zs, fb, qg, kg, fw=fw, q_col=c_q, k_col=c_k, f_col=c_f, aug=None)
        vn = zs[:, c_v:c_v + fw]
        q4 = qn.reshape(db, ts, fh, HEAD_DIM) * HEAD_DIM ** -0.5
        qbd = jnp.einsum("bthd,hg->bhtgd", q4, jnp.eye(fh, dtype=F32)).reshape(db, fh * ts, fw).astype(BF16)
        pad_p = lambda a: jnp.pad(a.reshape(db, ts, -1), ((0, 0), (0, page - ts), (0, 0)))
        lfn_t = jnp.swapaxes(pad_p(lf[:, :fh]), 1, 2)
        gate_f = pad_t(zs[:, c_gf:c_gf + fw])
        o_f = _decode(page_table, qbd, kpool, vpool, lftpool, l, pad_p(kn), pad_p(vn), lfn_t, gate_f,
                      ts=ts, nheads=fh)
        o_f = o_f[:, :ts].reshape(db * ts, fw)
        xs = _merge(xs, o_r, o_f, zs, c_mr, c_mf, wpr, wpf, wo)
        outs["ks"].append(kn.reshape(db, ts, fh, HEAD_DIM))
        outs["vs"].append(vn.reshape(db, ts, fh, HEAD_DIM))
        outs["ls"].append(lf[:, :fh].reshape(db, ts, fh))
        outs["ss"].append(_unpack_state(s_fin))
        outs["hs"].append(unpack_shift(zs3[:, -1]))

    y_prompt = xp[N_META:length][None]
    y_sample = xs.reshape(db, ts, d)
    st = lambda n: jnp.stack(outs[n])
    return (y_prompt, y_sample, st("kp"), st("vp"), st("lp"), st("sp"), st("hp"),
            st("ks"), st("vs"), st("ls"), st("ss"), st("hs"))
```

```python
import functools
import math

import jax
import jax.numpy as jnp
from jax import lax
from jax.experimental import pallas as pl
from jax.experimental.pallas import tpu as pltpu

F32 = jnp.float32
BF16 = jnp.bfloat16
HI = lax.Precision.HIGHEST

LANES = 128
HEAD_DIM = 64
N_META = 16
NORM_EPS = 1e-6
LNX_EPS = 1e-5 * HEAD_DIM
DECAY_OFFSET = 0.5
NEG_BIG = -1e30
LOG2E = math.log2(math.e)
VMEM_LIMIT = 56 * 1024 * 1024


def _cp(sem, vmem=VMEM_LIMIT):
    return pltpu.CompilerParams(dimension_semantics=sem, vmem_limit_bytes=vmem)


def _round_up(x, m):
    return (x + m - 1) // m * m


def _pick_tile(n, cap, mult=128):
    if n <= cap:
        return n
    best = mult
    t = mult
    while t <= cap:
        if n % t == 0:
            best = t
        t += mult
    return best


def _dot(a, b, precision=None):
    return jnp.dot(a, b, preferred_element_type=F32, precision=precision)


def _dot_nt(a, b, precision=None):
    return lax.dot_general(a, b, (((1,), (1,)), ((), ())), preferred_element_type=F32, precision=precision)


def _bdot(a, b):
    return _dot(a.astype(BF16), b.astype(BF16))


def _iota(shape, dim):
    return lax.broadcasted_iota(jnp.int32, shape, dim)


def _split2(x):
    hi = x.astype(BF16)
    return hi, (x - hi.astype(F32)).astype(BF16)


def _split3(x):
    hi = x.astype(BF16)
    r1 = x - hi.astype(F32)
    mid = r1.astype(BF16)
    return hi, mid, (r1 - mid.astype(F32)).astype(BF16)


def _head_block_ones(dtype=F32):
    return (_iota((LANES, LANES), 0) // HEAD_DIM == _iota((LANES, LANES), 1) // HEAD_DIM).astype(dtype)


def _head_sums(x):
    bd = _head_block_ones()
    parts = [_dot(x[:, i * LANES:(i + 1) * LANES], bd, HI) for i in range(x.shape[1] // LANES)]
    return parts[0] if len(parts) == 1 else jnp.concatenate(parts, axis=-1)


def _head_sums2(x, bd):
    hi, lo = _split2(x)
    return _dot(hi, bd) + _dot(lo, bd)


def _softplus(x):
    return jnp.maximum(x, 0.0) + jnp.log(1.0 + jnp.exp(-jnp.abs(x)))


def _silu(x):
    return x * jax.nn.sigmoid(x)


def _inproj_kernel(x_ref, g_ref, w_ref, o_ref, xn_ref):
    @pl.when(pl.program_id(1) == 0)
    def _():
        x = x_ref[...]
        ms = jnp.mean(x * x, axis=-1, keepdims=True)
        xn_ref[...] = (x * lax.rsqrt(ms + NORM_EPS) * g_ref[...]).astype(BF16)

    o_ref[...] = _dot(xn_ref[...], w_ref[...])


def _inproj(x, gain, w):
    m, d = x.shape
    n = w.shape[1]
    tm = _pick_tile(m, 768)
    tn = _pick_tile(n, 1152)
    return pl.pallas_call(
        _inproj_kernel,
        out_shape=jax.ShapeDtypeStruct((m, n), F32),
        grid=(m // tm, n // tn),
        in_specs=[pl.BlockSpec((tm, d), lambda i, j: (i, 0)),
                  pl.BlockSpec((1, d), lambda i, j: (0, 0)),
                  pl.BlockSpec((d, tn), lambda i, j: (0, j))],
        out_specs=pl.BlockSpec((tm, tn), lambda i, j: (i, j)),
        scratch_shapes=[pltpu.VMEM((tm, d), BF16)],
        compiler_params=_cp(("parallel", "arbitrary")),
        name="inproj",
    )(x, gain, w)


def _rwkv_prep_math(z3, zw, p3, pw, vf, prm, has_vres):
    (mu3, muw, w0, w2, a0, a2, kkg, kag, v0, v1, v2) = prm
    rw = w0.shape[1]
    x3 = z3 + (p3 - z3) * mu3
    xw = zw + (pw - zw) * muw
    r = x3[:, :rw]
    k = x3[:, rw:2 * rw]
    v = x3[:, 2 * rw:]
    wd = xw[:, :LANES]
    ad = xw[:, LANES:]
    w_raw = w0 + _dot(jnp.tanh(wd).astype(BF16), w2)
    w_log = -_softplus(-w_raw) - DECAY_OFFSET
    lw = -jnp.exp(w_log)
    a = jax.nn.sigmoid(a0 + _dot(ad.astype(BF16), a2))
    if has_vres:
        lora = _dot(_dot(v.astype(BF16), v1).astype(BF16), v2)
        v = v + (vf - v) * jax.nn.sigmoid(v0 + lora)
    kk = k * kkg
    ss = _head_sums(kk * kk)
    kk = kk / jnp.maximum(jnp.sqrt(ss), 1e-12)
    k2 = k * (1.0 + (a - 1.0) * kag)
    return r, lw, k2, v, kk, kk * a


def _rwkv_prep_kernel(*refs, has_vres, chained):
    z3_ref, zw_ref, p3_ref, pw_ref = refs[:4]
    pos = 4
    vf = None
    if has_vres:
        vf = refs[pos][...]
        pos += 1
    prm = [r[...] for r in refs[pos:pos + 11]]
    outs = refs[pos + 11:]
    z3 = z3_ref[...]
    zw = zw_ref[...]
    if chained:
        first = pl.program_id(0) == 0
        row0 = _iota((z3.shape[0], 1), 0) == 0
        last3 = jnp.where(first, 0.0, p3_ref[7:8, :])
        lastw = jnp.where(first, 0.0, pw_ref[7:8, :])
        p3 = jnp.where(row0, last3, pltpu.roll(z3, 1, 0))
        pw = jnp.where(row0, lastw, pltpu.roll(zw, 1, 0))
    else:
        p3 = p3_ref[...]
        pw = pw_ref[...]
    res = _rwkv_prep_math(z3, zw, p3, pw, vf, prm, has_vres)
    for o_ref, val in zip(outs, res):
        o_ref[...] = val


def _rwkv_prep(z, prev3, prevw, vfirst, prm, *, rw, wa_col, chained):
    m = z.shape[0]
    tm = _pick_tile(m, 256)
    has_vres = vfirst is not None
    wa_blk = wa_col // (2 * LANES)
    if chained:
        sub = tm // 8
        p3_spec = pl.BlockSpec((8, 3 * rw), lambda i: (jnp.maximum(i * sub - 1, 0), 0))
        pw_spec = pl.BlockSpec((8, 2 * LANES), lambda i: (jnp.maximum(i * sub - 1, 0), wa_blk))
        prev_args = (z, z)
    else:
        p3_spec = pl.BlockSpec((tm, 3 * rw), lambda i: (i, 0))
        pw_spec = pl.BlockSpec((tm, 2 * LANES), lambda i: (i, 0))
        prev_args = (prev3, prevw)
    in_specs = [pl.BlockSpec((tm, 3 * rw), lambda i: (i, 0)),
                pl.BlockSpec((tm, 2 * LANES), lambda i: (i, wa_blk)),
                p3_spec, pw_spec]
    args = [z, z, *prev_args]
    if has_vres:
        in_specs.append(pl.BlockSpec((tm, rw), lambda i: (i, 0)))
        args.append(vfirst)
    for p in prm:
        in_specs.append(pl.BlockSpec(p.shape, lambda i: (0, 0)))
        args.append(p)
    out = pl.pallas_call(
        functools.partial(_rwkv_prep_kernel, has_vres=has_vres, chained=chained),
        out_shape=[jax.ShapeDtypeStruct((m, rw), F32)] * 6,
        grid=(m // tm,),
        in_specs=in_specs,
        out_specs=[pl.BlockSpec((tm, rw), lambda i: (i, 0))] * 6,
        compiler_params=_cp(("parallel",)),
        name="rwkv_prep",
    )(*args)
    return out


def _wkv_pair(r, lw, k, v, kk, b, gate, s_prev, rk, lnw, lnb, consts, C):
    tri_b, lane_lo, strict, incl, eye, level_masks, bd = consts
    C2 = 2 * C
    g3 = _dot(tri_b, jnp.concatenate(_split3(lw), axis=-1))
    yield
    g = g3[:, :LANES] + g3[:, LANES:2 * LANES] + g3[:, 2 * LANES:]
    g_last = g[C - 1:C, :]
    e_g = jnp.exp(g)
    e_ng = jnp.exp(-g)
    e_tail = jnp.exp(g_last - g)

    def stack(x):
        return jnp.concatenate([jnp.where(lane_lo, x, 0.0), jnp.where(lane_lo, 0.0, x)], axis=0).astype(BF16)

    left = jnp.concatenate([stack(-kk * jnp.exp(g - lw)), stack(r * e_g)], axis=0)
    right = jnp.concatenate([stack(b * e_ng), stack(k * e_ng), s_prev.astype(BF16)], axis=0)
    v2 = stack(v)
    sc = _dot_nt(left, right)
    yield
    hs_a = sc[:C2, 2 * C2:]
    hs_r = sc[C2:, 2 * C2:]
    m_ab = jnp.where(strict, sc[:C2, :C2], 0.0)
    m_ak = jnp.where(strict, sc[:C2, C2:2 * C2], 0.0)
    p_rb = jnp.where(incl, sc[C2:, :C2], 0.0)
    p_rk = jnp.where(incl, sc[C2:, C2:2 * C2], 0.0)

    rhs = hs_a + _bdot(m_ak, v2)
    bonus = _head_sums2(r * k * rk, bd) * v
    yield

    same, offs = level_masks
    mb = jnp.where(same, m_ab, 0.0)
    x = eye + mb
    pw = mb
    steps = 1
    while steps * 2 < min(8, C):
        pw = _bdot(pw, pw)
        yield
        x = x + _bdot(x, pw)
        yield
        steps *= 2
    for off_mask in offs:
        xb = x.astype(BF16)
        t = _dot(xb, jnp.where(off_mask, m_ab, 0.0).astype(BF16)).astype(BF16)
        yield
        x = x + _dot(t, xb)
        yield

    u = _bdot(x, rhs)
    yield
    uv = jnp.concatenate([u.astype(BF16), v2], axis=0)
    o2 = hs_r + _dot(jnp.concatenate([p_rb, p_rk], axis=1).astype(BF16), uv)
    bk = jnp.concatenate([stack(b * e_tail), stack(k * e_tail)], axis=0)
    uv_t = jnp.concatenate([u.T, v2.astype(F32).T], axis=1).astype(BF16)
    s_new = s_prev * jnp.exp(g_last) + _dot(uv_t, bk)
    yield
    o = o2[:C] + o2[C:]

    mean = _head_sums2(o, bd) * (1.0 / HEAD_DIM)
    yield
    d = o - mean
    var = _head_sums2(d * d, bd) * (1.0 / HEAD_DIM)
    yield
    y = d * lax.rsqrt(var + LNX_EPS) * lnw + lnb + bonus
    return (y * _silu(gate)), s_new


def _run_interleaved(gens):
    results = [None] * len(gens)
    live = list(range(len(gens)))
    while live:
        for i in list(live):
            try:
                next(gens[i])
            except StopIteration as stop:
                results[i] = stop.value
                live.remove(i)
    return results


def _wkv_kernel(r_ref, lw_ref, k_ref, v_ref, kk_ref, b_ref, g_ref, s0_ref, rk_ref, lnw_ref, lnb_ref,
                o_ref, sfin_ref, s_sc, *, chunk, valid_len, npair):
    c = pl.program_id(2)
    n_c = pl.num_programs(2)
    C = chunk
    C2 = 2 * C

    @pl.when(c == 0)
    def _():
        s_sc[...] = s0_ref[...]

    i2 = _iota((C2, C2), 0)
    j2 = _iota((C2, C2), 1)
    base = min(8, C)
    offs = []
    n = base
    while n < C:
        offs.append(((i2 // (2 * n)) == (j2 // (2 * n))) & ((i2 // n) != (j2 // n)))
        n *= 2
    consts = ((_iota((C, C), 0) >= _iota((C, C), 1)).astype(BF16),
              _iota((1, LANES), 1) < HEAD_DIM,
              i2 > j2, i2 >= j2, (i2 == j2).astype(F32),
              ((i2 // base) == (j2 // base), offs),
              _head_block_ones(BF16))
    ok = None
    if valid_len is not None:
        ok = (c * C + _iota((C, 1), 0)) < valid_len

    gens = []
    for p in range(npair):
        sl = slice(p * LANES, (p + 1) * LANES)
        lw, k, kk, b = lw_ref[:, sl], k_ref[:, sl], kk_ref[:, sl], b_ref[:, sl]
        if ok is not None:
            lw = jnp.where(ok, lw, 0.0)
            k = jnp.where(ok, k, 0.0)
            kk = jnp.where(ok, kk, 0.0)
            b = jnp.where(ok, b, 0.0)
        gens.append(_wkv_pair(r_ref[:, sl], lw, k, v_ref[:, sl], kk, b, g_ref[:, sl], s_sc[p],
                              rk_ref[:, sl], lnw_ref[:, sl], lnb_ref[:, sl], consts, C))
    for p, (y, s_new) in enumerate(_run_interleaved(gens)):
        o_ref[:, p * LANES:(p + 1) * LANES] = y.astype(o_ref.dtype)
        s_sc[p] = s_new

    @pl.when(c == n_c - 1)
    def _():
        sfin_ref[...] = s_sc[...]


def _wkv(r, lw, k, v, kk, b, gate, gate_col, s0, rk, lnw, lnb, *, chunk, valid_len, pairs_per_step):
    bsz, t, rw = r.shape
    pw = pairs_per_step
    wid = pw * LANES
    ngrp = rw // wid
    nchunk = t // chunk
    gblk = gate_col // wid
    seq = pl.BlockSpec((None, chunk, wid), lambda bi, g, c: (bi, c, g))
    par = pl.BlockSpec((1, wid), lambda bi, g, c: (0, g))
    st = pl.BlockSpec((None, pw, LANES, LANES), lambda bi, g, c: (bi, g, 0, 0))
    return pl.pallas_call(
        functools.partial(_wkv_kernel, chunk=chunk, valid_len=valid_len, npair=pw),
        out_shape=[jax.ShapeDtypeStruct((bsz, t, rw), BF16),
                   jax.ShapeDtypeStruct((bsz, rw // LANES, LANES, LANES), F32)],
        grid=(bsz, ngrp, nchunk),
        in_specs=[seq] * 6 + [pl.BlockSpec((None, chunk, wid), lambda bi, g, c: (bi, c, gblk + g)),
                              st, par, par, par],
        out_specs=[seq, st],
        scratch_shapes=[pltpu.VMEM((pw, LANES, LANES), F32)],
        compiler_params=_cp(("parallel", "parallel", "arbitrary")),
        name="wkv",
    )(r, lw, k, v, kk, b, gate, s0, rk, lnw, lnb)


def _pack_state(s):
    bsz, h, n, _ = s.shape
    s = s.reshape(bsz, h // 2, 2, n, n)
    z = jnp.zeros_like(s[:, :, 0])
    top = jnp.concatenate([s[:, :, 0], z], axis=-1)
    bot = jnp.concatenate([z, s[:, :, 1]], axis=-1)
    return jnp.concatenate([top, bot], axis=-2)


def _unpack_state(sp):
    bsz, p = sp.shape[:2]
    n = HEAD_DIM
    s = jnp.stack([sp[:, :, :n, :n], sp[:, :, n:, n:]], axis=2)
    return s.reshape(bsz, 2 * p, n, n)


def _fox_prep_kernel(*refs, augment):
    q_ref, k_ref, f_ref, fb_ref, qg_ref, kg_ref = refs[:6]
    q = q_ref[...]
    k = k_ref[...]
    qn = q * lax.rsqrt(_head_sums(q * q) * (1.0 / HEAD_DIM) + NORM_EPS) * qg_ref[...]
    kn = k * lax.rsqrt(_head_sums(k * k) * (1.0 / HEAD_DIM) + NORM_EPS) * kg_ref[...]
    logf = -_softplus(-(f_ref[...] + fb_ref[...]))
    if not augment:
        qn_ref, kn_ref, lf_ref = refs[6:]
        qn_ref[...] = qn
        kn_ref[...] = kn
        lf_ref[...] = logf
        return
    v_ref, pl_ref, sq_ref, sk_ref, cq_ref, ck_ref, cv_ref, kn_ref, lf_ref, qa_ref, ka_ref, va_ref, carry = refs[6:]

    @pl.when(pl.program_id(0) == 0)
    def _():
        carry[...] = jnp.zeros_like(carry)

    tm = q.shape[0]
    tri = (_iota((tm, tm), 0) >= _iota((tm, tm), 1)).astype(F32)
    c = carry[...] + _dot(tri, logf, HI)
    carry[...] = c[tm - 1:tm, :]
    c3 = jnp.concatenate(_split3(c * LOG2E), axis=-1)
    place = pl_ref[...]
    kn_ref[...] = kn
    lf_ref[...] = logf
    qs = (qn * (LOG2E * HEAD_DIM ** -0.5)).astype(BF16)
    qa_ref[...] = (_dot(qs, place) + _dot(c3, sq_ref[...]) + cq_ref[...]).astype(BF16)
    ka_ref[...] = (_dot(kn.astype(BF16), place) + _dot(c3, sk_ref[...]) + ck_ref[...]).astype(BF16)
    va_ref[...] = (_dot(v_ref[...].astype(BF16), place) + cv_ref[...]).astype(BF16)


def _fox_aug_consts(fw):
    nh = fw // HEAD_DIM
    rows = jnp.arange(fw)
    cols = (rows // HEAD_DIM) * LANES + rows % HEAD_DIM
    place = jnp.zeros((fw, nh * LANES), F32).at[rows, cols].set(1.0).astype(BF16)
    h = jnp.arange(nh)
    sq = jnp.zeros((3 * LANES, nh * LANES), F32)
    sk = jnp.zeros((3 * LANES, nh * LANES), F32)
    cq = jnp.zeros((1, nh * LANES), F32)
    ck = jnp.zeros((1, nh * LANES), F32)
    cv = jnp.zeros((1, nh * LANES), F32).at[0, h * LANES + HEAD_DIM].set(1.0)
    for part in range(3):
        sq = sq.at[part * LANES + h, h * LANES + HEAD_DIM + part].set(1.0)
        sk = sk.at[part * LANES + h, h * LANES + HEAD_DIM + 3 + part].set(-1.0)
        cq = cq.at[0, h * LANES + HEAD_DIM + 3 + part].set(1.0)
        ck = ck.at[0, h * LANES + HEAD_DIM + part].set(1.0)
    return place, sq.astype(BF16), sk.astype(BF16), cq, ck, cv


def _fox_prep(z, fb, qg, kg, *, fw, q_col, k_col, v_col, f_col, aug):
    m = z.shape[0]
    tm = _pick_tile(m, 256)
    qb, kb, vblk, fblk = q_col // fw, k_col // fw, v_col // fw, f_col // LANES
    in_specs = [pl.BlockSpec((tm, fw), lambda i: (i, qb)),
                pl.BlockSpec((tm, fw), lambda i: (i, kb)),
                pl.BlockSpec((tm, LANES), lambda i: (i, fblk)),
                pl.BlockSpec((1, LANES), lambda i: (0, 0)),
                pl.BlockSpec((1, fw), lambda i: (0, 0)),
                pl.BlockSpec((1, fw), lambda i: (0, 0))]
    args = [z, z, z, fb, qg, kg]
    row = lambda w: pl.BlockSpec((tm, w), lambda i: (i, 0))
    if aug is None:
        return pl.pallas_call(
            functools.partial(_fox_prep_kernel, augment=False),
            out_shape=[jax.ShapeDtypeStruct((m, fw), F32), jax.ShapeDtypeStruct((m, fw), F32),
                       jax.ShapeDtypeStruct((m, LANES), F32)],
            grid=(m // tm,), in_specs=in_specs, out_specs=[row(fw), row(fw), row(LANES)],
            compiler_params=_cp(("parallel",)), name="fox_prep_s",
        )(*args)
    in_specs.append(pl.BlockSpec((tm, fw), lambda i: (i, vblk)))
    args.append(z)
    for a in aug:
        in_specs.append(pl.BlockSpec(a.shape, lambda i: (0, 0)))
        args.append(a)
    nh = fw // HEAD_DIM
    wide = jax.ShapeDtypeStruct((m, nh * LANES), BF16)
    return pl.pallas_call(
        functools.partial(_fox_prep_kernel, augment=True),
        out_shape=[jax.ShapeDtypeStruct((m, fw), F32), jax.ShapeDtypeStruct((m, LANES), F32), wide, wide, wide],
        grid=(m // tm,), in_specs=in_specs,
        out_specs=[row(fw), row(LANES), row(nh * LANES), row(nh * LANES), row(nh * LANES)],
        scratch_shapes=[pltpu.VMEM((1, LANES), F32)],
        compiler_params=_cp(("arbitrary",)), name="fox_prep_p",
    )(*args)


def _flash_kernel(q_ref, k_ref, v_ref, g_ref, o_ref, m_sc, acc_sc, *, tq, tk):
    qi = pl.program_id(1)
    nsub = tq // tk
    m_sc[...] = jnp.full_like(m_sc, NEG_BIG)
    acc_sc[...] = jnp.zeros_like(acc_sc)

    def tile(kt, mask):
        off = pl.multiple_of(kt * tk, tk)
        new = []
        for hh in range(2):
            sl = slice(hh * LANES, (hh + 1) * LANES)
            s = _dot_nt(q_ref[:, sl], k_ref[pl.ds(off, tk), sl])
            if mask is not None:
                s = jnp.where(mask, s, NEG_BIG)
            m_old = m_sc[hh]
            m_new = jnp.maximum(m_old, jnp.broadcast_to(jnp.max(s, axis=-1, keepdims=True), m_old.shape))
            p = jnp.exp2(s - jnp.concatenate([m_new] * (tk // LANES), axis=-1)).astype(BF16)
            acc = acc_sc[hh] * jnp.exp2(m_old - m_new) + _dot(p, v_ref[pl.ds(off, tk), sl])
            new.append((m_new, acc))
        for hh, (m_new, acc) in enumerate(new):
            m_sc[hh] = m_new
            acc_sc[hh] = acc

    def body(kt, carry):
        tile(kt, None)
        return carry

    lax.fori_loop(0, qi * nsub, body, 0)
    rows = _iota((tq, tk), 0)
    cols = _iota((tq, tk), 1)
    for j in range(nsub):
        tile(qi * nsub + j, rows >= cols + j * tk)

    outs = []
    for hh in range(2):
        acc = acc_sc[hh]
        outs.append(acc / acc[:, HEAD_DIM:HEAD_DIM + 1])
    o = jnp.where(_iota((1, LANES), 1) < HEAD_DIM, outs[0], pltpu.roll(outs[1], HEAD_DIM, 1))
    o_ref[...] = (o * _silu(g_ref[...])).astype(o_ref.dtype)


def _flash(qa, ka, va, z, g_col):
    lp = qa.shape[0]
    npair = qa.shape[1] // (2 * LANES)
    tk = 256
    tq = _pick_tile(lp, 768, tk)
    gblk = g_col // LANES
    return pl.pallas_call(
        functools.partial(_flash_kernel, tq=tq, tk=tk),
        out_shape=jax.ShapeDtypeStruct((lp, npair * LANES), BF16),
        grid=(npair, lp // tq),
        in_specs=[pl.BlockSpec((tq, 2 * LANES), lambda p, i: (i, p)),
                  pl.BlockSpec((lp, 2 * LANES), lambda p, i: (0, p)),
                  pl.BlockSpec((lp, 2 * LANES), lambda p, i: (0, p)),
                  pl.BlockSpec((tq, LANES), lambda p, i: (i, gblk + p))],
        out_specs=pl.BlockSpec((tq, LANES), lambda p, i: (i, p)),
        scratch_shapes=[pltpu.VMEM((2, tq, LANES), F32), pltpu.VMEM((2, tq, LANES), F32)],
        compiler_params=_cp(("parallel", "arbitrary")),
        name="fox_flash",
    )(qa, ka, va, z)


def _lfpool_kernel(x_ref, o_ref):
    n = x_ref.shape[1]
    jr = _iota((n, 2 * n), 0)
    jc = _iota((n, 2 * n), 1)
    tri2 = ((jr > jc) | (jc == n)).astype(F32)
    o_ref[...] = _dot(x_ref[...], tri2, HI)


def _lfpool(lft):
    rws, n = lft.shape
    tm = _pick_tile(rws, 2048)
    return pl.pallas_call(
        _lfpool_kernel,
        out_shape=jax.ShapeDtypeStruct((rws, 2 * n), F32),
        grid=(rws // tm,),
        in_specs=[pl.BlockSpec((tm, n), lambda i: (i, 0))],
        out_specs=pl.BlockSpec((tm, 2 * n), lambda i: (i, 0)),
        compiler_params=_cp(("parallel",)), name="fox_lfpool",
    )(lft)


def _decode_kernel(*refs, ts, nheads, pps):
    q_ref = refs[1]
    k_refs = refs[2:2 + pps]
    v_refs = refs[2 + pps:2 + 2 * pps]
    s_refs = refs[2 + 2 * pps:2 + 3 * pps]
    t_refs = refs[2 + 3 * pps:2 + 4 * pps]
    kn_ref, vn_ref, lfn_ref, g_ref, o_ref, m_sc, l_sc, acc_sc, r_sc, mask_sc = refs[2 + 4 * pps:]
    j = pl.program_id(1)
    nj = pl.num_programs(1)
    nrow = nheads * ts
    ncol = k_refs[0].shape[0] * nheads

    @pl.when(j == 0)
    def _():
        m_sc[...] = jnp.full_like(m_sc, NEG_BIG)
        l_sc[...] = jnp.zeros_like(l_sc)
        acc_sc[...] = jnp.zeros_like(acc_sc)
        r_sc[...] = jnp.zeros_like(r_sc)
        own = (_iota((nrow, ncol), 0) // ts) == (_iota((nrow, ncol), 1) % nheads)
        mask_sc[...] = jnp.where(own, 0.0, NEG_BIG)

    q = q_ref[...]

    def update(blocks):
        ss = [_dot_nt(q, kf.astype(BF16)) + bias for kf, _, bias in blocks]
        m_old = m_sc[...]
        m_new = m_old
        for s in ss:
            m_new = jnp.maximum(m_new, jnp.max(s, axis=-1, keepdims=True))
        alpha = jnp.exp(m_old - m_new)
        l_new = alpha * l_sc[...]
        acc = acc_sc[...] * alpha
        for s, (_, vf, _) in zip(ss, blocks):
            p = jnp.exp(s - m_new)
            l_new = l_new + jnp.sum(p, axis=-1, keepdims=True)
            acc = acc + _dot(p.astype(BF16), vf.astype(BF16))
        l_sc[...] = l_new
        acc_sc[...] = acc
        m_sc[...] = m_new

    run = r_sc[...]
    mask = mask_sc[...]
    blocks = []
    for k_ref, v_ref, s_ref, t_ref in zip(k_refs, v_refs, s_refs, t_refs):
        blocks.append((k_ref[...].reshape(ncol, HEAD_DIM), v_ref[...].reshape(ncol, HEAD_DIM),
                       mask + (s_ref[...] + run)))
        run = run + t_ref[...]
    r_sc[...] = run
    update(blocks)

    @pl.when(j == nj - 1)
    def _():
        nn = kn_ref.shape[0]
        cr = _iota((nn, nn), 0)
        cc = _iota((nn, nn), 1)
        same_head_le = ((cr % nheads) == (cc % nheads)) & (cr <= cc)
        cum = _dot(lfn_ref[...], same_head_le.astype(F32), HI)
        rr = _iota((nrow, nn), 0)
        cn = _iota((nrow, nn), 1)
        ok = ((rr // ts) == (cn % nheads)) & ((cn // nheads) <= (rr % ts)) & ((cn // nheads) < ts)
        update([(kn_ref[...], vn_ref[...], jnp.where(ok, -cum, NEG_BIG))])
        o_ref[...] = (acc_sc[...] / l_sc[...] * _silu(g_ref[...])).astype(o_ref.dtype)


def _decode(page_table, q_rows, cache_k, cache_v, sfx, tot, layer, knew, vnew, lfn, gate, *, ts, nheads, pps):
    bsz, nrow, hd = q_rows.shape
    npg = page_table.shape[1]
    page = cache_k.shape[2]
    ncol = page * nheads
    nn = knew.shape[1]

    def pool(shape, which):
        nd = len(shape)
        return pl.BlockSpec((None, None) + shape,
                            lambda b, j, pt: (layer, pt[b, npg - 1 - pps * j - which]) + (0,) * nd)

    per_b = lambda w0, w1: pl.BlockSpec((None, w0, w1), lambda b, j, pt: (b, 0, 0))
    kv = (page, nheads, hd)
    each = range(pps)
    grid_spec = pltpu.PrefetchScalarGridSpec(
        num_scalar_prefetch=1, grid=(bsz, npg // pps),
        in_specs=([per_b(nrow, hd)] + [pool(kv, w) for w in each] + [pool(kv, w) for w in each]
                  + [pool((1, ncol), w) for w in each] + [pool((1, ncol), w) for w in each]
                  + [per_b(nn, hd), per_b(nn, hd), per_b(1, nn), per_b(nrow, hd)]),
        out_specs=per_b(nrow, hd),
        scratch_shapes=[pltpu.VMEM((nrow, 1), F32), pltpu.VMEM((nrow, 1), F32), pltpu.VMEM((nrow, hd), F32),
                        pltpu.VMEM((1, ncol), F32), pltpu.VMEM((nrow, ncol), F32)])
    return pl.pallas_call(
        functools.partial(_decode_kernel, ts=ts, nheads=nheads, pps=pps),
        out_shape=jax.ShapeDtypeStruct((bsz, nrow, hd), BF16),
        grid_spec=grid_spec,
        compiler_params=_cp(("parallel", "arbitrary")),
        name="fox_decode",
    )(page_table, q_rows, *([cache_k] * pps), *([cache_v] * pps), *([sfx] * pps), *([tot] * pps),
      knew, vnew, lfn, gate)


def _merge_kernel(x_ref, or_ref, of_ref, mr_ref, mf_ref, wpr_ref, wpf_ref, wo_ref, o_ref):
    a = _dot(or_ref[...], wpr_ref[...])
    b = _dot(of_ref[...], wpf_ref[...])
    mixed = jax.nn.sigmoid(mr_ref[...]) * a + jax.nn.sigmoid(mf_ref[...]) * b
    o_ref[...] = x_ref[...] + _dot(mixed.astype(BF16), wo_ref[...])


def _merge(x, o_r, o_f, z, mr_col, mf_col, wpr, wpf, wo):
    m, d = x.shape
    rw = o_r.shape[1]
    fw = o_f.shape[1]
    tm = _pick_tile(m, 256)
    whole = lambda a: pl.BlockSpec(a.shape, lambda i: (0, 0), pipeline_mode=pl.Buffered(1))
    return pl.pallas_call(
        _merge_kernel,
        out_shape=jax.ShapeDtypeStruct((m, d), F32),
        grid=(m // tm,),
        in_specs=[pl.BlockSpec((tm, d), lambda i: (i, 0)),
                  pl.BlockSpec((tm, rw), lambda i: (i, 0)),
                  pl.BlockSpec((tm, fw), lambda i: (i, 0)),
                  pl.BlockSpec((tm, d), lambda i: (i, mr_col // d)),
                  pl.BlockSpec((tm, d), lambda i: (i, mf_col // d)),
                  whole(wpr), whole(wpf), whole(wo)],
        out_specs=pl.BlockSpec((tm, d), lambda i: (i, 0)),
        compiler_params=_cp(("parallel",)),
        name="merge",
    )(x, o_r, o_f, z, z, wpr, wpf, wo)


def _pad_cols(a, width):
    return jnp.pad(a, [(0, 0)] * (a.ndim - 1) + [(0, width - a.shape[-1])])


def kernel(x_prompt, x_sample, cache_k, cache_v, cache_logf, state_wkv, state_shift, page_table, meta_tokens, norm_gain, w_in, r_mu, r_w0, r_w2, r_a0, r_a2, r_v0, r_v1, r_v2, r_kk, r_ka, r_rk, r_lnx_w, r_lnx_b, f_bias, f_qgain, f_kgain, w_proj_r, w_proj_f, w_out):
    bsz, seq, d = x_prompt.shape
    db, ts, _ = x_sample.shape
    depth = w_in.shape[0]
    rw = r_w0.shape[1]
    rh = r_rk.shape[1]
    fh = f_bias.shape[1]
    fw = fh * HEAD_DIM
    lora_w = r_w2.shape[1]
    lora_a = r_a2.shape[1]
    shift_w = 3 * rw + lora_w + lora_a
    page = cache_k.shape[2]
    npool = cache_k.shape[1]
    npg = page_table.shape[1]
    assert bsz == 1 and rw == rh * HEAD_DIM and fw == rw and d == 2 * rw
    assert lora_w <= LANES and lora_a <= LANES and fh <= LANES and ts * fh <= LANES
    pps = 4 if npg % 4 == 0 else (2 if npg % 2 == 0 else 1)

    s_gr = shift_w
    s_q = s_gr + rw
    s_k = s_q + fw
    s_v = s_k + fw
    s_f = s_v + fw
    s_gf = s_f + fh
    s_mr = s_gf + fw
    s_mf = s_mr + d
    c_gr = 3 * rw
    c_q = c_gr + rw
    c_k = c_q + fw
    c_v = c_k + fw
    c_gf = c_v + fw
    c_mr = c_gf + fw
    c_mf = c_mr + d
    c_wa = c_mf + d
    c_f = c_wa + 2 * LANES
    n_pack = c_f + LANES

    def pack_w(w):
        segs = [w[:, :3 * rw], w[:, s_gr:s_gr + rw], w[:, s_q:s_q + fw], w[:, s_k:s_k + fw], w[:, s_v:s_v + fw],
                w[:, s_gf:s_gf + fw], w[:, s_mr:s_mr + d], w[:, s_mf:s_mf + d],
                _pad_cols(w[:, 3 * rw:3 * rw + lora_w], LANES),
                _pad_cols(w[:, 3 * rw + lora_w:shift_w], LANES),
                _pad_cols(w[:, s_f:s_f + fh], LANES)]
        return jnp.concatenate(segs, axis=1).astype(BF16)

    def pack_shift(a):
        return a[..., :3 * rw], jnp.concatenate([_pad_cols(a[..., 3 * rw:3 * rw + lora_w], LANES),
                                                 _pad_cols(a[..., 3 * rw + lora_w:], LANES)], axis=-1)

    def unpack_shift(zrow):
        return jnp.concatenate([zrow[..., :3 * rw], zrow[..., c_wa:c_wa + lora_w],
                                zrow[..., c_wa + LANES:c_wa + LANES + lora_a]], axis=-1)

    length = seq + N_META
    lp = _round_up(length, 768) if length > 768 else _round_up(length, 256)
    chunk = 64
    xp = jnp.concatenate([meta_tokens.astype(F32), x_prompt[0], jnp.zeros((lp - length, d), F32)], axis=0)
    xs = x_sample.reshape(db * ts, d)
    aug = _fox_aug_consts(fw)
    s0_p = jnp.zeros((1, rh // 2, LANES, LANES), F32)

    lft = jnp.swapaxes(cache_logf, 2, 3).reshape(depth * npool * fh, page)
    lfo = _lfpool(lft).reshape(depth, npool, fh, 2 * page)
    sfx = jnp.swapaxes(lfo[..., :page], 2, 3).reshape(depth, npool, 1, page * fh)
    tot = jnp.broadcast_to(lfo[:, :, None, :, page], (depth, npool, page, fh)).reshape(depth, npool, 1, page * fh)

    vf_p = None
    vf_s = None
    outs = {n: [] for n in ("kp", "vp", "lp", "sp", "hp", "ks", "vs", "ls", "ss", "hs")}
    for l in range(depth):
        w_pack = pack_w(w_in[l])
        gain = norm_gain[l][None, :]
        mu3, muw = pack_shift(r_mu[l][None, :])
        if l == 0:
            v0 = jnp.zeros((1, rw), F32)
            v1 = jnp.zeros((rw, LANES), BF16)
            v2 = jnp.zeros((LANES, rw), BF16)
        else:
            v0 = r_v0[l - 1][None, :]
            v1 = _pad_cols(r_v1[l - 1], LANES).astype(BF16)
            v2 = jnp.pad(r_v2[l - 1], ((0, LANES - r_v2.shape[1]), (0, 0))).astype(BF16)
        prm = (mu3, muw, r_w0[l][None, :],
               jnp.pad(r_w2[l], ((0, LANES - lora_w), (0, 0))).astype(BF16),
               r_a0[l][None, :],
               jnp.pad(r_a2[l], ((0, LANES - lora_a), (0, 0))).astype(BF16),
               r_kk[l][None, :], r_ka[l][None, :], v0, v1, v2)
        rk = r_rk[l].reshape(1, rw)
        lnw = r_lnx_w[l][None, :]
        lnb = r_lnx_b[l][None, :]
        fb = _pad_cols(f_bias[l][None, :], LANES)
        qg = jnp.tile(f_qgain[l], fh)[None, :]
        kg = jnp.tile(f_kgain[l], fh)[None, :]
        wpr = w_proj_r[l].astype(BF16)
        wpf = w_proj_f[l].astype(BF16)
        wo = w_out[l].astype(BF16)

        z = _inproj(xp, gain, w_pack)
        r, lw, k2, v, kk, b = _rwkv_prep(z, None, None, vf_p, prm, rw=rw, wa_col=c_wa, chained=True)
        if l == 0:
            vf_p = v
        o_r, s_fin = _wkv(r[None], lw[None], k2[None], v[None], kk[None], b[None], z[None], c_gr,
                          s0_p, rk, lnw, lnb, chunk=chunk, valid_len=length, pairs_per_step=rw // LANES)
        kn, lf, qa, ka, va = _fox_prep(z, fb, qg, kg, fw=fw, q_col=c_q, k_col=c_k, v_col=c_v, f_col=c_f, aug=aug)
        o_f = _flash(qa, ka, va, z, c_gf)
        xp = _merge(xp, o_r[0], o_f, z, c_mr, c_mf, wpr, wpf, wo)
        outs["kp"].append(kn[:length].reshape(1, length, fh, HEAD_DIM))
        outs["vp"].append(z[:length, c_v:c_v + fw].reshape(1, length, fh, HEAD_DIM))
        outs["lp"].append(lf[:length, :fh][None])
        outs["sp"].append(_unpack_state(s_fin))
        outs["hp"].append(unpack_shift(z[length - 1:length]))

        zs = _inproj(xs, gain, w_pack)
        zs3 = zs.reshape(db, ts, n_pack)
        sh3, shw = pack_shift(state_shift[l])
        prev3 = jnp.concatenate([sh3[:, None], zs3[:, :-1, :3 * rw]], axis=1).reshape(db * ts, 3 * rw)
        prevw = jnp.concatenate([shw[:, None], zs3[:, :-1, c_wa:c_wa + 2 * LANES]], axis=1).reshape(db * ts, 2 * LANES)
        res = _rwkv_prep(zs, prev3, prevw, vf_s, prm, rw=rw, wa_col=c_wa, chained=False)
        if l == 0:
            vf_s = res[3]
        pad_t = lambda a: jnp.pad(a.reshape(db, ts, -1), ((0, 0), (0, 8 - ts), (0, 0)))
        r, lw, k2, v, kk, b = (pad_t(a) for a in res)
        gate_r = pad_t(zs[:, c_gr:c_gr + rw])
        o_r, s_fin = _wkv(r, lw, k2, v, kk, b, gate_r, 0, _pack_state(state_wkv[l]), rk, lnw, lnb,
                          chunk=8, valid_len=None, pairs_per_step=rw // LANES)
        o_r = o_r[:, :ts].reshape(db * ts, rw)
        qn, kn, lf = _fox_prep(zs, fb, qg, kg, fw=fw, q_col=c_q, k_col=c_k, v_col=c_v, f_col=c_f, aug=None)
        vn = zs[:, c_v:c_v + fw]
        to_rows = lambda a: jnp.swapaxes(a.reshape(db, ts, fh, HEAD_DIM), 1, 2).reshape(db, fh * ts, HEAD_DIM)
        q_rows = to_rows(qn * HEAD_DIM ** -0.5).astype(BF16)
        gate_f = to_rows(zs[:, c_gf:c_gf + fw])
        pad_n = lambda a: jnp.pad(a.reshape(db, ts * fh, -1), ((0, 0), (0, LANES - ts * fh), (0, 0)))
        lfn = jnp.swapaxes(pad_n(lf[:, :fh]), 1, 2)
        o_f = _decode(page_table, q_rows, cache_k, cache_v, sfx, tot, l, pad_n(kn), pad_n(vn), lfn, gate_f,
                      ts=ts, nheads=fh, pps=pps)
        o_f = jnp.swapaxes(o_f.reshape(db, fh, ts, HEAD_DIM), 1, 2).reshape(db * ts, fw)
        xs = _merge(xs, o_r, o_f, zs, c_mr, c_mf, wpr, wpf, wo)
        outs["ks"].append(kn.reshape(db, ts, fh, HEAD_DIM))
        outs["vs"].append(vn.reshape(db, ts, fh, HEAD_DIM))
        outs["ls"].append(lf[:, :fh].reshape(db, ts, fh))
        outs["ss"].append(_unpack_state(s_fin))
        outs["hs"].append(unpack_shift(zs3[:, -1]))

    y_prompt = xp[N_META:length][None]
    y_sample = xs.reshape(db, ts, d)
    st = lambda n: jnp.stack(outs[n])
    return (y_prompt, y_sample, st("kp"), st("vp"), st("lp"), st("sp"), st("hp"),
            st("ks"), st("vs"), st("ls"), st("ss"), st("hs"))
```

```python
import functools
import math

import jax
import jax.numpy as jnp
from jax import lax
from jax.experimental import pallas as pl
from jax.experimental.pallas import tpu as pltpu

F32 = jnp.float32
BF16 = jnp.bfloat16
HI = lax.Precision.HIGHEST

LANES = 128
HEAD_DIM = 64
N_META = 16
NORM_EPS = 1e-6
LNX_EPS = 1e-5 * HEAD_DIM
DECAY_OFFSET = 0.5
NEG_BIG = -1e30
LOG2E = math.log2(math.e)
VMEM_LIMIT = 56 * 1024 * 1024


def _cp(sem, vmem=VMEM_LIMIT):
    return pltpu.CompilerParams(dimension_semantics=sem, vmem_limit_bytes=vmem)


def _round_up(x, m):
    return (x + m - 1) // m * m


def _pick_tile(n, cap, mult=128):
    if n <= cap:
        return n
    best = mult
    t = mult
    while t <= cap:
        if n % t == 0:
            best = t
        t += mult
    return best


def _dot(a, b, precision=None):
    return jnp.dot(a, b, preferred_element_type=F32, precision=precision)


def _dot_nt(a, b, precision=None):
    return lax.dot_general(a, b, (((1,), (1,)), ((), ())), preferred_element_type=F32, precision=precision)


def _bdot(a, b):
    return _dot(a.astype(BF16), b.astype(BF16))


def _iota(shape, dim):
    return lax.broadcasted_iota(jnp.int32, shape, dim)


def _split2(x):
    hi = x.astype(BF16)
    return hi, (x - hi.astype(F32)).astype(BF16)


def _split3(x):
    hi = x.astype(BF16)
    r1 = x - hi.astype(F32)
    mid = r1.astype(BF16)
    return hi, mid, (r1 - mid.astype(F32)).astype(BF16)


def _head_block_ones(dtype=F32):
    return (_iota((LANES, LANES), 0) // HEAD_DIM == _iota((LANES, LANES), 1) // HEAD_DIM).astype(dtype)


def _head_sums(x):
    bd = _head_block_ones()
    parts = [_dot(x[:, i * LANES:(i + 1) * LANES], bd, HI) for i in range(x.shape[1] // LANES)]
    return parts[0] if len(parts) == 1 else jnp.concatenate(parts, axis=-1)


def _head_sums2(x, bd):
    hi, lo = _split2(x)
    return _dot(hi, bd) + _dot(lo, bd)


def _softplus(x):
    return jnp.maximum(x, 0.0) + jnp.log(1.0 + jnp.exp(-jnp.abs(x)))


def _silu(x):
    return x * jax.nn.sigmoid(x)


def _inproj_kernel(x_ref, g_ref, w_ref, o_ref, xn_ref):
    @pl.when(pl.program_id(1) == 0)
    def _():
        x = x_ref[...]
        ms = jnp.mean(x * x, axis=-1, keepdims=True)
        xn_ref[...] = (x * lax.rsqrt(ms + NORM_EPS) * g_ref[...]).astype(BF16)

    o_ref[...] = _dot(xn_ref[...], w_ref[...])


def _inproj(x, gain, w):
    m, d = x.shape
    n = w.shape[1]
    tm = _pick_tile(m, 768)
    tn = _pick_tile(n, 1152)
    return pl.pallas_call(
        _inproj_kernel,
        out_shape=jax.ShapeDtypeStruct((m, n), F32),
        grid=(m // tm, n // tn),
        in_specs=[pl.BlockSpec((tm, d), lambda i, j: (i, 0)),
                  pl.BlockSpec((1, d), lambda i, j: (0, 0)),
                  pl.BlockSpec((d, tn), lambda i, j: (0, j))],
        out_specs=pl.BlockSpec((tm, tn), lambda i, j: (i, j)),
        scratch_shapes=[pltpu.VMEM((tm, d), BF16)],
        compiler_params=_cp(("parallel", "arbitrary")),
        name="inproj",
    )(x, gain, w)


def _rwkv_prep_math(z3, zw, p3, pw, vf, prm, has_vres):
    (mu3, muw, w0, w2, a0, a2, kkg, kag, v0, v1, v2) = prm
    rw = w0.shape[1]
    x3 = z3 + (p3 - z3) * mu3
    xw = zw + (pw - zw) * muw
    r = x3[:, :rw]
    k = x3[:, rw:2 * rw]
    v = x3[:, 2 * rw:]
    wd = xw[:, :LANES]
    ad = xw[:, LANES:]
    w_raw = w0 + _dot(jnp.tanh(wd).astype(BF16), w2)
    w_log = -_softplus(-w_raw) - DECAY_OFFSET
    lw = -jnp.exp(w_log)
    a = jax.nn.sigmoid(a0 + _dot(ad.astype(BF16), a2))
    if has_vres:
        lora = _dot(_dot(v.astype(BF16), v1).astype(BF16), v2)
        v = v + (vf - v) * jax.nn.sigmoid(v0 + lora)
    kk = k * kkg
    ss = _head_sums(kk * kk)
    kk = kk / jnp.maximum(jnp.sqrt(ss), 1e-12)
    k2 = k * (1.0 + (a - 1.0) * kag)
    return r, lw, k2, v, kk, kk * a


def _rwkv_prep_kernel(*refs, has_vres, chained):
    z3_ref, zw_ref, p3_ref, pw_ref = refs[:4]
    pos = 4
    vf = None
    if has_vres:
        vf = refs[pos][...]
        pos += 1
    prm = [r[...] for r in refs[pos:pos + 11]]
    outs = refs[pos + 11:]
    z3 = z3_ref[...]
    zw = zw_ref[...]
    if chained:
        first = pl.program_id(0) == 0
        row0 = _iota((z3.shape[0], 1), 0) == 0
        last3 = jnp.where(first, 0.0, p3_ref[7:8, :])
        lastw = jnp.where(first, 0.0, pw_ref[7:8, :])
        p3 = jnp.where(row0, last3, pltpu.roll(z3, 1, 0))
        pw = jnp.where(row0, lastw, pltpu.roll(zw, 1, 0))
    else:
        p3 = p3_ref[...]
        pw = pw_ref[...]
    res = _rwkv_prep_math(z3, zw, p3, pw, vf, prm, has_vres)
    for o_ref, val in zip(outs, res):
        o_ref[...] = val


def _rwkv_prep(z, prev3, prevw, vfirst, prm, *, rw, wa_col, chained):
    m = z.shape[0]
    tm = _pick_tile(m, 256)
    has_vres = vfirst is not None
    wa_blk = wa_col // (2 * LANES)
    if chained:
        sub = tm // 8
        p3_spec = pl.BlockSpec((8, 3 * rw), lambda i: (jnp.maximum(i * sub - 1, 0), 0))
        pw_spec = pl.BlockSpec((8, 2 * LANES), lambda i: (jnp.maximum(i * sub - 1, 0), wa_blk))
        prev_args = (z, z)
    else:
        p3_spec = pl.BlockSpec((tm, 3 * rw), lambda i: (i, 0))
        pw_spec = pl.BlockSpec((tm, 2 * LANES), lambda i: (i, 0))
        prev_args = (prev3, prevw)
    in_specs = [pl.BlockSpec((tm, 3 * rw), lambda i: (i, 0)),
                pl.BlockSpec((tm, 2 * LANES), lambda i: (i, wa_blk)),
                p3_spec, pw_spec]
    args = [z, z, *prev_args]
    if has_vres:
        in_specs.append(pl.BlockSpec((tm, rw), lambda i: (i, 0)))
        args.append(vfirst)
    for p in prm:
        in_specs.append(pl.BlockSpec(p.shape, lambda i: (0, 0)))
        args.append(p)
    out = pl.pallas_call(
        functools.partial(_rwkv_prep_kernel, has_vres=has_vres, chained=chained),
        out_shape=[jax.ShapeDtypeStruct((m, rw), F32)] * 6,
        grid=(m // tm,),
        in_specs=in_specs,
        out_specs=[pl.BlockSpec((tm, rw), lambda i: (i, 0))] * 6,
        compiler_params=_cp(("parallel",)),
        name="rwkv_prep",
    )(*args)
    return out


def _wkv_pair(r, lw, k, v, kk, b, gate, s_prev, rk, lnw, lnb, consts, C):
    tri_b, lane_lo, strict, incl, eye, level_masks, bd = consts
    C2 = 2 * C
    g3 = _dot(tri_b, jnp.concatenate(_split3(lw), axis=-1))
    yield
    g = g3[:, :LANES] + g3[:, LANES:2 * LANES] + g3[:, 2 * LANES:]
    g_last = g[C - 1:C, :]
    e_g = jnp.exp(g)
    e_ng = jnp.exp(-g)
    e_tail = jnp.exp(g_last - g)

    def stack(x):
        return jnp.concatenate([jnp.where(lane_lo, x, 0.0), jnp.where(lane_lo, 0.0, x)], axis=0).astype(BF16)

    left = jnp.concatenate([stack(-kk * jnp.exp(g - lw)), stack(r * e_g)], axis=0)
    right = jnp.concatenate([stack(b * e_ng), stack(k * e_ng), s_prev.astype(BF16)], axis=0)
    v2 = stack(v)
    sc = _dot_nt(left, right)
    yield
    hs_a = sc[:C2, 2 * C2:]
    hs_r = sc[C2:, 2 * C2:]
    m_ab = jnp.where(strict, sc[:C2, :C2], 0.0)
    m_ak = jnp.where(strict, sc[:C2, C2:2 * C2], 0.0)
    p_rb = jnp.where(incl, sc[C2:, :C2], 0.0)
    p_rk = jnp.where(incl, sc[C2:, C2:2 * C2], 0.0)

    rhs = hs_a + _bdot(m_ak, v2)
    bonus = _head_sums2(r * k * rk, bd) * v
    yield

    same, offs = level_masks
    mb = jnp.where(same, m_ab, 0.0)
    x = eye + mb
    pw = mb
    steps = 1
    while steps * 2 < min(8, C):
        pw = _bdot(pw, pw)
        yield
        x = x + _bdot(x, pw)
        yield
        steps *= 2
    for off_mask in offs:
        xb = x.astype(BF16)
        t = _dot(xb, jnp.where(off_mask, m_ab, 0.0).astype(BF16)).astype(BF16)
        yield
        x = x + _dot(t, xb)
        yield

    u = _bdot(x, rhs)
    yield
    uv = jnp.concatenate([u.astype(BF16), v2], axis=0)
    o2 = hs_r + _dot(jnp.concatenate([p_rb, p_rk], axis=1).astype(BF16), uv)
    bk = jnp.concatenate([stack(b * e_tail), stack(k * e_tail)], axis=0)
    uv_t = jnp.concatenate([u.T, v2.astype(F32).T], axis=1).astype(BF16)
    s_new = s_prev * jnp.exp(g_last) + _dot(uv_t, bk)
    yield
    o = o2[:C] + o2[C:]

    mean = _head_sums2(o, bd) * (1.0 / HEAD_DIM)
    yield
    d = o - mean
    var = _head_sums2(d * d, bd) * (1.0 / HEAD_DIM)
    yield
    y = d * lax.rsqrt(var + LNX_EPS) * lnw + lnb + bonus
    return (y * _silu(gate)), s_new


def _run_interleaved(gens):
    results = [None] * len(gens)
    live = list(range(len(gens)))
    while live:
        for i in list(live):
            try:
                next(gens[i])
            except StopIteration as stop:
                results[i] = stop.value
                live.remove(i)
    return results


def _wkv_kernel(r_ref, lw_ref, k_ref, v_ref, kk_ref, b_ref, g_ref, s0_ref, rk_ref, lnw_ref, lnb_ref,
                o_ref, sfin_ref, s_sc, *, chunk, valid_len, npair):
    c = pl.program_id(2)
    n_c = pl.num_programs(2)
    C = chunk
    C2 = 2 * C

    @pl.when(c == 0)
    def _():
        s_sc[...] = s0_ref[...]

    i2 = _iota((C2, C2), 0)
    j2 = _iota((C2, C2), 1)
    base = min(8, C)
    offs = []
    n = base
    while n < C:
        offs.append(((i2 // (2 * n)) == (j2 // (2 * n))) & ((i2 // n) != (j2 // n)))
        n *= 2
    consts = ((_iota((C, C), 0) >= _iota((C, C), 1)).astype(BF16),
              _iota((1, LANES), 1) < HEAD_DIM,
              i2 > j2, i2 >= j2, (i2 == j2).astype(F32),
              ((i2 // base) == (j2 // base), offs),
              _head_block_ones(BF16))
    ok = None
    if valid_len is not None:
        ok = (c * C + _iota((C, 1), 0)) < valid_len

    gens = []
    for p in range(npair):
        sl = slice(p * LANES, (p + 1) * LANES)
        lw, k, kk, b = lw_ref[:, sl], k_ref[:, sl], kk_ref[:, sl], b_ref[:, sl]
        if ok is not None:
            lw = jnp.where(ok, lw, 0.0)
            k = jnp.where(ok, k, 0.0)
            kk = jnp.where(ok, kk, 0.0)
            b = jnp.where(ok, b, 0.0)
        gens.append(_wkv_pair(r_ref[:, sl], lw, k, v_ref[:, sl], kk, b, g_ref[:, sl], s_sc[p],
                              rk_ref[:, sl], lnw_ref[:, sl], lnb_ref[:, sl], consts, C))
    for p, (y, s_new) in enumerate(_run_interleaved(gens)):
        o_ref[:, p * LANES:(p + 1) * LANES] = y.astype(o_ref.dtype)
        s_sc[p] = s_new

    @pl.when(c == n_c - 1)
    def _():
        sfin_ref[...] = s_sc[...]


def _wkv(r, lw, k, v, kk, b, gate, gate_col, s0, rk, lnw, lnb, *, chunk, valid_len, pairs_per_step):
    bsz, t, rw = r.shape
    pw = pairs_per_step
    wid = pw * LANES
    ngrp = rw // wid
    nchunk = t // chunk
    gblk = gate_col // wid
    seq = pl.BlockSpec((None, chunk, wid), lambda bi, g, c: (bi, c, g))
    par = pl.BlockSpec((1, wid), lambda bi, g, c: (0, g))
    st = pl.BlockSpec((None, pw, LANES, LANES), lambda bi, g, c: (bi, g, 0, 0))
    return pl.pallas_call(
        functools.partial(_wkv_kernel, chunk=chunk, valid_len=valid_len, npair=pw),
        out_shape=[jax.ShapeDtypeStruct((bsz, t, rw), BF16),
                   jax.ShapeDtypeStruct((bsz, rw // LANES, LANES, LANES), F32)],
        grid=(bsz, ngrp, nchunk),
        in_specs=[seq] * 6 + [pl.BlockSpec((None, chunk, wid), lambda bi, g, c: (bi, c, gblk + g)),
                              st, par, par, par],
        out_specs=[seq, st],
        scratch_shapes=[pltpu.VMEM((pw, LANES, LANES), F32)],
        compiler_params=_cp(("parallel", "parallel", "arbitrary")),
        name="wkv",
    )(r, lw, k, v, kk, b, gate, s0, rk, lnw, lnb)


def _pack_state(s):
    bsz, h, n, _ = s.shape
    s = s.reshape(bsz, h // 2, 2, n, n)
    z = jnp.zeros_like(s[:, :, 0])
    top = jnp.concatenate([s[:, :, 0], z], axis=-1)
    bot = jnp.concatenate([z, s[:, :, 1]], axis=-1)
    return jnp.concatenate([top, bot], axis=-2)


def _unpack_state(sp):
    bsz, p = sp.shape[:2]
    n = HEAD_DIM
    s = jnp.stack([sp[:, :, :n, :n], sp[:, :, n:, n:]], axis=2)
    return s.reshape(bsz, 2 * p, n, n)


def _fox_prep_kernel(*refs, augment):
    q_ref, k_ref, f_ref, fb_ref, qg_ref, kg_ref = refs[:6]
    q = q_ref[...]
    k = k_ref[...]
    qn = q * lax.rsqrt(_head_sums(q * q) * (1.0 / HEAD_DIM) + NORM_EPS) * qg_ref[...]
    kn = k * lax.rsqrt(_head_sums(k * k) * (1.0 / HEAD_DIM) + NORM_EPS) * kg_ref[...]
    logf = -_softplus(-(f_ref[...] + fb_ref[...]))
    if not augment:
        qn_ref, kn_ref, lf_ref = refs[6:]
        qn_ref[...] = qn
        kn_ref[...] = kn
        lf_ref[...] = logf
        return
    v_ref, pl_ref, sq_ref, sk_ref, cq_ref, ck_ref, cv_ref, kn_ref, lf_ref, qa_ref, ka_ref, va_ref, carry = refs[6:]

    @pl.when(pl.program_id(0) == 0)
    def _():
        carry[...] = jnp.zeros_like(carry)

    tm = q.shape[0]
    tri = (_iota((tm, tm), 0) >= _iota((tm, tm), 1)).astype(F32)
    c = carry[...] + _dot(tri, logf, HI)
    carry[...] = c[tm - 1:tm, :]
    c3 = jnp.concatenate(_split3(c * LOG2E), axis=-1)
    place = pl_ref[...]

    def spread(x):
        xb = x.astype(BF16)
        return jnp.concatenate([_dot(xb[:, i * LANES:(i + 1) * LANES], place) for i in range(x.shape[1] // LANES)],
                               axis=-1)

    kn_ref[...] = kn
    lf_ref[...] = logf
    qa_ref[...] = (spread(qn * (LOG2E * HEAD_DIM ** -0.5)) + _dot(c3, sq_ref[...]) + cq_ref[...]).astype(BF16)
    ka_ref[...] = (spread(kn) + _dot(c3, sk_ref[...]) + ck_ref[...]).astype(BF16)
    va_ref[...] = (spread(v_ref[...]) + cv_ref[...]).astype(BF16)


def _fox_aug_consts(fw):
    nh = fw // HEAD_DIM
    rows = jnp.arange(LANES)
    cols = (rows // HEAD_DIM) * LANES + rows % HEAD_DIM
    place = jnp.zeros((LANES, 2 * LANES), F32).at[rows, cols].set(1.0).astype(BF16)
    h = jnp.arange(nh)
    sq = jnp.zeros((3 * LANES, nh * LANES), F32)
    sk = jnp.zeros((3 * LANES, nh * LANES), F32)
    cq = jnp.zeros((1, nh * LANES), F32)
    ck = jnp.zeros((1, nh * LANES), F32)
    cv = jnp.zeros((1, nh * LANES), F32).at[0, h * LANES + HEAD_DIM].set(1.0)
    for part in range(3):
        sq = sq.at[part * LANES + h, h * LANES + HEAD_DIM + part].set(1.0)
        sk = sk.at[part * LANES + h, h * LANES + HEAD_DIM + 3 + part].set(-1.0)
        cq = cq.at[0, h * LANES + HEAD_DIM + 3 + part].set(1.0)
        ck = ck.at[0, h * LANES + HEAD_DIM + part].set(1.0)
    return place, sq.astype(BF16), sk.astype(BF16), cq, ck, cv


def _fox_prep(z, fb, qg, kg, *, fw, q_col, k_col, v_col, f_col, aug):
    m = z.shape[0]
    tm = _pick_tile(m, 256)
    qb, kb, vblk, fblk = q_col // fw, k_col // fw, v_col // fw, f_col // LANES
    in_specs = [pl.BlockSpec((tm, fw), lambda i: (i, qb)),
                pl.BlockSpec((tm, fw), lambda i: (i, kb)),
                pl.BlockSpec((tm, LANES), lambda i: (i, fblk)),
                pl.BlockSpec((1, LANES), lambda i: (0, 0)),
                pl.BlockSpec((1, fw), lambda i: (0, 0)),
                pl.BlockSpec((1, fw), lambda i: (0, 0))]
    args = [z, z, z, fb, qg, kg]
    row = lambda w: pl.BlockSpec((tm, w), lambda i: (i, 0))
    if aug is None:
        return pl.pallas_call(
            functools.partial(_fox_prep_kernel, augment=False),
            out_shape=[jax.ShapeDtypeStruct((m, fw), F32), jax.ShapeDtypeStruct((m, fw), F32),
                       jax.ShapeDtypeStruct((m, LANES), F32)],
            grid=(m // tm,), in_specs=in_specs, out_specs=[row(fw), row(fw), row(LANES)],
            compiler_params=_cp(("parallel",)), name="fox_prep_s",
        )(*args)
    in_specs.append(pl.BlockSpec((tm, fw), lambda i: (i, vblk)))
    args.append(z)
    for a in aug:
        in_specs.append(pl.BlockSpec(a.shape, lambda i: (0, 0)))
        args.append(a)
    nh = fw // HEAD_DIM
    wide = jax.ShapeDtypeStruct((m, nh * LANES), BF16)
    return pl.pallas_call(
        functools.partial(_fox_prep_kernel, augment=True),
        out_shape=[jax.ShapeDtypeStruct((m, fw), F32), jax.ShapeDtypeStruct((m, LANES), F32), wide, wide, wide],
        grid=(m // tm,), in_specs=in_specs,
        out_specs=[row(fw), row(LANES), row(nh * LANES), row(nh * LANES), row(nh * LANES)],
        scratch_shapes=[pltpu.VMEM((1, LANES), F32)],
        compiler_params=_cp(("arbitrary",)), name="fox_prep_p",
    )(*args)


def _flash_kernel(q_ref, k_ref, v_ref, g_ref, o_ref, m_sc, acc_sc, *, tq, tk):
    qi = pl.program_id(1)
    nsub = tq // tk
    m_sc[...] = jnp.full_like(m_sc, NEG_BIG)
    acc_sc[...] = jnp.zeros_like(acc_sc)

    def tile(kt, mask):
        off = pl.multiple_of(kt * tk, tk)
        new = []
        for hh in range(2):
            sl = slice(hh * LANES, (hh + 1) * LANES)
            s = _dot_nt(q_ref[:, sl], k_ref[pl.ds(off, tk), sl])
            if mask is not None:
                s = jnp.where(mask, s, NEG_BIG)
            m_old = m_sc[hh]
            m_new = jnp.maximum(m_old, jnp.broadcast_to(jnp.max(s, axis=-1, keepdims=True), m_old.shape))
            p = jnp.exp2(s - jnp.concatenate([m_new] * (tk // LANES), axis=-1)).astype(BF16)
            acc = acc_sc[hh] * jnp.exp2(m_old - m_new) + _dot(p, v_ref[pl.ds(off, tk), sl])
            new.append((m_new, acc))
        for hh, (m_new, acc) in enumerate(new):
            m_sc[hh] = m_new
            acc_sc[hh] = acc

    def body(kt, carry):
        tile(kt, None)
        return carry

    lax.fori_loop(0, qi * nsub, body, 0)
    rows = _iota((tq, tk), 0)
    cols = _iota((tq, tk), 1)
    for j in range(nsub):
        tile(qi * nsub + j, rows >= cols + j * tk)

    outs = []
    for hh in range(2):
        acc = acc_sc[hh]
        outs.append(acc / acc[:, HEAD_DIM:HEAD_DIM + 1])
    o = jnp.where(_iota((1, LANES), 1) < HEAD_DIM, outs[0], pltpu.roll(outs[1], HEAD_DIM, 1))
    o_ref[...] = (o * _silu(g_ref[...])).astype(o_ref.dtype)


def _flash(qa, ka, va, z, g_col):
    lp = qa.shape[0]
    npair = qa.shape[1] // (2 * LANES)
    tk = 256
    tq = _pick_tile(lp, 768, tk)
    gblk = g_col // LANES
    return pl.pallas_call(
        functools.partial(_flash_kernel, tq=tq, tk=tk),
        out_shape=jax.ShapeDtypeStruct((lp, npair * LANES), BF16),
        grid=(npair, lp // tq),
        in_specs=[pl.BlockSpec((tq, 2 * LANES), lambda p, i: (i, p)),
                  pl.BlockSpec((lp, 2 * LANES), lambda p, i: (0, p)),
                  pl.BlockSpec((lp, 2 * LANES), lambda p, i: (0, p)),
                  pl.BlockSpec((tq, LANES), lambda p, i: (i, gblk + p))],
        out_specs=pl.BlockSpec((tq, LANES), lambda p, i: (i, p)),
        scratch_shapes=[pltpu.VMEM((2, tq, LANES), F32), pltpu.VMEM((2, tq, LANES), F32)],
        compiler_params=_cp(("parallel", "arbitrary")),
        name="fox_flash",
    )(qa, ka, va, z)


def _lfpool_kernel(x_ref, o_ref):
    n = x_ref.shape[1]
    jr = _iota((n, 2 * n), 0)
    jc = _iota((n, 2 * n), 1)
    tri2 = ((jr > jc) | (jc == n)).astype(F32)
    o_ref[...] = _dot(x_ref[...], tri2, HI)


def _lfpool(lft):
    rws, n = lft.shape
    tm = _pick_tile(rws, 2048)
    return pl.pallas_call(
        _lfpool_kernel,
        out_shape=jax.ShapeDtypeStruct((rws, 2 * n), F32),
        grid=(rws // tm,),
        in_specs=[pl.BlockSpec((tm, n), lambda i: (i, 0))],
        out_specs=pl.BlockSpec((tm, 2 * n), lambda i: (i, 0)),
        compiler_params=_cp(("parallel",)), name="fox_lfpool",
    )(lft)


def _decode_kernel(*refs, ts, nheads, pps):
    q_ref = refs[1]
    k_refs = refs[2:2 + pps]
    v_refs = refs[2 + pps:2 + 2 * pps]
    f_refs = refs[2 + 2 * pps:2 + 3 * pps]
    kn_ref, vn_ref, lfn_ref, g_ref, o_ref, m_sc, l_sc, acc_sc, r_sc = refs[2 + 3 * pps:]
    j = pl.program_id(1)
    nj = pl.num_programs(1)
    nrow = nheads * ts
    page = k_refs[0].shape[-1]
    fw = nheads * HEAD_DIM

    @pl.when(j == 0)
    def _():
        m_sc[...] = jnp.full_like(m_sc, NEG_BIG)
        l_sc[...] = jnp.zeros_like(l_sc)
        acc_sc[...] = jnp.zeros_like(acc_sc)
        r_sc[...] = jnp.zeros_like(r_sc)

    q = q_ref[...]

    def per_row(x):
        return jnp.concatenate([x] * ts, axis=0)

    def update(blocks):
        ss = [_dot(q, kt.astype(BF16)) + bias for kt, _, bias in blocks]
        m_old = m_sc[...]
        m_new = m_old
        for s in ss:
            m_new = jnp.maximum(m_new, jnp.max(s, axis=-1, keepdims=True))
        alpha = jnp.exp(m_old - m_new)
        l_new = alpha * l_sc[...]
        acc = acc_sc[...] * alpha
        for s, (_, vt, _) in zip(ss, blocks):
            p = jnp.exp(s - m_new)
            l_new = l_new + jnp.sum(p, axis=-1, keepdims=True)
            acc = acc + _dot_nt(p.astype(BF16), vt.astype(BF16))
        l_sc[...] = l_new
        acc_sc[...] = acc
        m_sc[...] = m_new

    run = r_sc[...]
    blocks = []
    for k_ref, v_ref, f_ref in zip(k_refs, v_refs, f_refs):
        f = f_ref[...]
        blocks.append((k_ref[...].reshape(fw, page), v_ref[...].reshape(fw, page),
                       per_row(f[:, :page] + run)))
        run = run + f[:, page:page + 1]
    r_sc[...] = run
    update(blocks)

    @pl.when(j == nj - 1)
    def _():
        nn = kn_ref.shape[-1]
        upper = (_iota((nn, nn), 0) <= _iota((nn, nn), 1)).astype(F32)
        cum = per_row(_dot(lfn_ref[...], upper, HI))
        tq = _iota((nrow, nn), 0) // nheads
        tk = _iota((nrow, nn), 1)
        update([(kn_ref[...], vn_ref[...], jnp.where((tk <= tq) & (tk < ts), -cum, NEG_BIG))])
        out = acc_sc[...] / l_sc[...]
        own = (_iota((nrow, fw), 0) % nheads) == (_iota((nrow, fw), 1) // HEAD_DIM)
        sel = (_iota((8, nrow), 1) // nheads == _iota((8, nrow), 0)).astype(F32)
        o = _dot(sel, jnp.where(own, out, 0.0), HI)
        o_ref[...] = (o * _silu(g_ref[...])).astype(o_ref.dtype)


def _decode(page_table, qbd, kt_pool, vt_pool, lfo, layer, knew_t, vnew_t, lfn_t, gate, *, ts, nheads, pps):
    bsz, nrow, fw = qbd.shape
    npg = page_table.shape[1]
    page = kt_pool.shape[-1]
    nn = knew_t.shape[-1]

    def pool(shape, which):
        nd = len(shape)
        return pl.BlockSpec((None, None) + shape,
                            lambda b, j, pt: (layer, pt[b, npg - 1 - pps * j - which]) + (0,) * nd)

    per_b = lambda w0, w1: pl.BlockSpec((None, w0, w1), lambda b, j, pt: (b, 0, 0))
    kv = (nheads, HEAD_DIM, page)
    each = range(pps)
    grid_spec = pltpu.PrefetchScalarGridSpec(
        num_scalar_prefetch=1, grid=(bsz, npg // pps),
        in_specs=([per_b(nrow, fw)] + [pool(kv, w) for w in each] + [pool(kv, w) for w in each]
                  + [pool((nheads, 2 * page), w) for w in each]
                  + [per_b(fw, nn), per_b(fw, nn), per_b(nheads, nn), per_b(8, fw)]),
        out_specs=per_b(8, fw),
        scratch_shapes=[pltpu.VMEM((nrow, 1), F32), pltpu.VMEM((nrow, 1), F32), pltpu.VMEM((nrow, fw), F32),
                        pltpu.VMEM((nheads, 1), F32)])
    return pl.pallas_call(
        functools.partial(_decode_kernel, ts=ts, nheads=nheads, pps=pps),
        out_shape=jax.ShapeDtypeStruct((bsz, 8, fw), BF16),
        grid_spec=grid_spec,
        compiler_params=_cp(("parallel", "arbitrary")),
        name="fox_decode",
    )(page_table, qbd, *([kt_pool] * pps), *([vt_pool] * pps), *([lfo] * pps), knew_t, vnew_t, lfn_t, gate)


def _merge_kernel(x_ref, or_ref, of_ref, mr_ref, mf_ref, wpr_ref, wpf_ref, wo_ref, o_ref):
    a = _dot(or_ref[...], wpr_ref[...])
    b = _dot(of_ref[...], wpf_ref[...])
    mixed = jax.nn.sigmoid(mr_ref[...]) * a + jax.nn.sigmoid(mf_ref[...]) * b
    o_ref[...] = x_ref[...] + _dot(mixed.astype(BF16), wo_ref[...])


def _merge(x, o_r, o_f, z, mr_col, mf_col, wpr, wpf, wo):
    m, d = x.shape
    rw = o_r.shape[1]
    fw = o_f.shape[1]
    tm = _pick_tile(m, 256)
    whole = lambda a: pl.BlockSpec(a.shape, lambda i: (0, 0), pipeline_mode=pl.Buffered(1))
    return pl.pallas_call(
        _merge_kernel,
        out_shape=jax.ShapeDtypeStruct((m, d), F32),
        grid=(m // tm,),
        in_specs=[pl.BlockSpec((tm, d), lambda i: (i, 0)),
                  pl.BlockSpec((tm, rw), lambda i: (i, 0)),
                  pl.BlockSpec((tm, fw), lambda i: (i, 0)),
                  pl.BlockSpec((tm, d), lambda i: (i, mr_col // d)),
                  pl.BlockSpec((tm, d), lambda i: (i, mf_col // d)),
                  whole(wpr), whole(wpf), whole(wo)],
        out_specs=pl.BlockSpec((tm, d), lambda i: (i, 0)),
        compiler_params=_cp(("parallel",)),
        name="merge",
    )(x, o_r, o_f, z, z, wpr, wpf, wo)


def _pad_cols(a, width):
    return jnp.pad(a, [(0, 0)] * (a.ndim - 1) + [(0, width - a.shape[-1])])


def kernel(x_prompt, x_sample, cache_k, cache_v, cache_logf, state_wkv, state_shift, page_table, meta_tokens, norm_gain, w_in, r_mu, r_w0, r_w2, r_a0, r_a2, r_v0, r_v1, r_v2, r_kk, r_ka, r_rk, r_lnx_w, r_lnx_b, f_bias, f_qgain, f_kgain, w_proj_r, w_proj_f, w_out):
    bsz, seq, d = x_prompt.shape
    db, ts, _ = x_sample.shape
    depth = w_in.shape[0]
    rw = r_w0.shape[1]
    rh = r_rk.shape[1]
    fh = f_bias.shape[1]
    fw = fh * HEAD_DIM
    lora_w = r_w2.shape[1]
    lora_a = r_a2.shape[1]
    shift_w = 3 * rw + lora_w + lora_a
    page = cache_k.shape[2]
    npool = cache_k.shape[1]
    npg = page_table.shape[1]
    assert bsz == 1 and rw == rh * HEAD_DIM and fw == rw and d == 2 * rw
    assert lora_w <= LANES and lora_a <= LANES and fh <= LANES and ts * fh <= LANES
    pps = 4 if npg % 4 == 0 else (2 if npg % 2 == 0 else 1)

    s_gr = shift_w
    s_q = s_gr + rw
    s_k = s_q + fw
    s_v = s_k + fw
    s_f = s_v + fw
    s_gf = s_f + fh
    s_mr = s_gf + fw
    s_mf = s_mr + d
    c_gr = 3 * rw
    c_q = c_gr + rw
    c_k = c_q + fw
    c_v = c_k + fw
    c_gf = c_v + fw
    c_mr = c_gf + fw
    c_mf = c_mr + d
    c_wa = c_mf + d
    c_f = c_wa + 2 * LANES
    n_pack = c_f + LANES

    def pack_w(w):
        segs = [w[:, :3 * rw], w[:, s_gr:s_gr + rw], w[:, s_q:s_q + fw], w[:, s_k:s_k + fw], w[:, s_v:s_v + fw],
                w[:, s_gf:s_gf + fw], w[:, s_mr:s_mr + d], w[:, s_mf:s_mf + d],
                _pad_cols(w[:, 3 * rw:3 * rw + lora_w], LANES),
                _pad_cols(w[:, 3 * rw + lora_w:shift_w], LANES),
                _pad_cols(w[:, s_f:s_f + fh], LANES)]
        return jnp.concatenate(segs, axis=1).astype(BF16)

    def pack_shift(a):
        return a[..., :3 * rw], jnp.concatenate([_pad_cols(a[..., 3 * rw:3 * rw + lora_w], LANES),
                                                 _pad_cols(a[..., 3 * rw + lora_w:], LANES)], axis=-1)

    def unpack_shift(zrow):
        return jnp.concatenate([zrow[..., :3 * rw], zrow[..., c_wa:c_wa + lora_w],
                                zrow[..., c_wa + LANES:c_wa + LANES + lora_a]], axis=-1)

    length = seq + N_META
    lp = _round_up(length, 768) if length > 768 else _round_up(length, 256)
    chunk = 64
    xp = jnp.concatenate([meta_tokens.astype(F32), x_prompt[0], jnp.zeros((lp - length, d), F32)], axis=0)
    xs = x_sample.reshape(db * ts, d)
    aug = _fox_aug_consts(fw)
    s0_p = jnp.zeros((1, rh // 2, LANES, LANES), F32)

    kt_pool = jnp.transpose(cache_k, (0, 1, 3, 4, 2))
    vt_pool = jnp.transpose(cache_v, (0, 1, 3, 4, 2))
    lft = jnp.swapaxes(cache_logf, 2, 3).reshape(depth * npool * fh, page)
    lfo = _lfpool(lft).reshape(depth, npool, fh, 2 * page)

    vf_p = None
    vf_s = None
    outs = {n: [] for n in ("kp", "vp", "lp", "sp", "hp", "ks", "vs", "ls", "ss", "hs")}
    for l in range(depth):
        w_pack = pack_w(w_in[l])
        gain = norm_gain[l][None, :]
        mu3, muw = pack_shift(r_mu[l][None, :])
        if l == 0:
            v0 = jnp.zeros((1, rw), F32)
            v1 = jnp.zeros((rw, LANES), BF16)
            v2 = jnp.zeros((LANES, rw), BF16)
        else:
            v0 = r_v0[l - 1][None, :]
            v1 = _pad_cols(r_v1[l - 1], LANES).astype(BF16)
            v2 = jnp.pad(r_v2[l - 1], ((0, LANES - r_v2.shape[1]), (0, 0))).astype(BF16)
        prm = (mu3, muw, r_w0[l][None, :],
               jnp.pad(r_w2[l], ((0, LANES - lora_w), (0, 0))).astype(BF16),
               r_a0[l][None, :],
               jnp.pad(r_a2[l], ((0, LANES - lora_a), (0, 0))).astype(BF16),
               r_kk[l][None, :], r_ka[l][None, :], v0, v1, v2)
        rk = r_rk[l].reshape(1, rw)
        lnw = r_lnx_w[l][None, :]
        lnb = r_lnx_b[l][None, :]
        fb = _pad_cols(f_bias[l][None, :], LANES)
        qg = jnp.tile(f_qgain[l], fh)[None, :]
        kg = jnp.tile(f_kgain[l], fh)[None, :]
        wpr = w_proj_r[l].astype(BF16)
        wpf = w_proj_f[l].astype(BF16)
        wo = w_out[l].astype(BF16)

        z = _inproj(xp, gain, w_pack)
        r, lw, k2, v, kk, b = _rwkv_prep(z, None, None, vf_p, prm, rw=rw, wa_col=c_wa, chained=True)
        if l == 0:
            vf_p = v
        o_r, s_fin = _wkv(r[None], lw[None], k2[None], v[None], kk[None], b[None], z[None], c_gr,
                          s0_p, rk, lnw, lnb, chunk=chunk, valid_len=length, pairs_per_step=rw // LANES)
        kn, lf, qa, ka, va = _fox_prep(z, fb, qg, kg, fw=fw, q_col=c_q, k_col=c_k, v_col=c_v, f_col=c_f, aug=aug)
        o_f = _flash(qa, ka, va, z, c_gf)
        xp = _merge(xp, o_r[0], o_f, z, c_mr, c_mf, wpr, wpf, wo)
        outs["kp"].append(kn[:length].reshape(1, length, fh, HEAD_DIM))
        outs["vp"].append(z[:length, c_v:c_v + fw].reshape(1, length, fh, HEAD_DIM))
        outs["lp"].append(lf[:length, :fh][None])
        outs["sp"].append(_unpack_state(s_fin))
        outs["hp"].append(unpack_shift(z[length - 1:length]))

        zs = _inproj(xs, gain, w_pack)
        zs3 = zs.reshape(db, ts, n_pack)
        sh3, shw = pack_shift(state_shift[l])
        prev3 = jnp.concatenate([sh3[:, None], zs3[:, :-1, :3 * rw]], axis=1).reshape(db * ts, 3 * rw)
        prevw = jnp.concatenate([shw[:, None], zs3[:, :-1, c_wa:c_wa + 2 * LANES]], axis=1).reshape(db * ts, 2 * LANES)
        res = _rwkv_prep(zs, prev3, prevw, vf_s, prm, rw=rw, wa_col=c_wa, chained=False)
        if l == 0:
            vf_s = res[3]
        pad_t = lambda a: jnp.pad(a.reshape(db, ts, -1), ((0, 0), (0, 8 - ts), (0, 0)))
        r, lw, k2, v, kk, b = (pad_t(a) for a in res)
        gate_r = pad_t(zs[:, c_gr:c_gr + rw])
        o_r, s_fin = _wkv(r, lw, k2, v, kk, b, gate_r, 0, _pack_state(state_wkv[l]), rk, lnw, lnb,
                          chunk=8, valid_len=None, pairs_per_step=rw // LANES)
        o_r = o_r[:, :ts].reshape(db * ts, rw)
        qn, kn, lf = _fox_prep(zs, fb, qg, kg, fw=fw, q_col=c_q, k_col=c_k, v_col=c_v, f_col=c_f, aug=None)
        vn = zs[:, c_v:c_v + fw]
        q4 = qn.reshape(db, ts, fh, HEAD_DIM) * HEAD_DIM ** -0.5
        qbd = jnp.einsum("bthd,hg->bthgd", q4, jnp.eye(fh, dtype=F32)).reshape(db, ts * fh, fw).astype(BF16)
        pad_k = lambda a: _pad_cols(jnp.swapaxes(a.reshape(db, ts, -1), 1, 2), page)
        gate_f = pad_t(zs[:, c_gf:c_gf + fw])
        o_f = _decode(page_table, qbd, kt_pool, vt_pool, lfo, l, pad_k(kn), pad_k(vn), pad_k(lf[:, :fh]), gate_f,
                      ts=ts, nheads=fh, pps=pps)
        o_f = o_f[:, :ts].reshape(db * ts, fw)
        xs = _merge(xs, o_r, o_f, zs, c_mr, c_mf, wpr, wpf, wo)
        outs["ks"].append(kn.reshape(db, ts, fh, HEAD_DIM))
        outs["vs"].append(vn.reshape(db, ts, fh, HEAD_DIM))
        outs["ls"].append(lf[:, :fh].reshape(db, ts, fh))
        outs["ss"].append(_unpack_state(s_fin))
        outs["hs"].append(unpack_shift(zs3[:, -1]))

    y_prompt = xp[N_META:length][None]
    y_sample = xs.reshape(db, ts, d)
    st = lambda n: jnp.stack(outs[n])
    return (y_prompt, y_sample, st("kp"), st("vp"), st("lp"), st("sp"), st("hp"),
            st("ks"), st("vs"), st("ls"), st("ss"), st("hs"))
```

```python
import functools
import math

import jax
import jax.numpy as jnp
from jax import lax
from jax.experimental import pallas as pl
from jax.experimental.pallas import tpu as pltpu

F32 = jnp.float32
BF16 = jnp.bfloat16
HI = lax.Precision.HIGHEST

LANES = 128
HEAD_DIM = 64
N_META = 16
NORM_EPS = 1e-6
LNX_EPS = 1e-5 * HEAD_DIM
DECAY_OFFSET = 0.5
NEG_BIG = -1e30
LOG2E = math.log2(math.e)
VMEM_LIMIT = 56 * 1024 * 1024
INPROJ_TN = 1280


def _cp(sem, vmem=VMEM_LIMIT):
    return pltpu.CompilerParams(dimension_semantics=sem, vmem_limit_bytes=vmem)


def _round_up(x, m):
    return (x + m - 1) // m * m


def _pick_tile(n, cap, mult=128):
    if n <= cap:
        return n
    best = mult
    t = mult
    while t <= cap:
        if n % t == 0:
            best = t
        t += mult
    return best


def _dot(a, b, precision=None):
    return jnp.dot(a, b, preferred_element_type=F32, precision=precision)


def _dot_nt(a, b, precision=None):
    return lax.dot_general(a, b, (((1,), (1,)), ((), ())), preferred_element_type=F32, precision=precision)


def _bdot(a, b):
    return _dot(a.astype(BF16), b.astype(BF16))


def _iota(shape, dim):
    return lax.broadcasted_iota(jnp.int32, shape, dim)


def _split2(x):
    hi = x.astype(BF16)
    return hi, (x - hi.astype(F32)).astype(BF16)


def _split3(x):
    hi = x.astype(BF16)
    r1 = x - hi.astype(F32)
    mid = r1.astype(BF16)
    return hi, mid, (r1 - mid.astype(F32)).astype(BF16)


def _head_block_ones(dtype=F32):
    return (_iota((LANES, LANES), 0) // HEAD_DIM == _iota((LANES, LANES), 1) // HEAD_DIM).astype(dtype)


def _head_sums(x):
    bd = _head_block_ones()
    parts = [_dot(x[:, i * LANES:(i + 1) * LANES], bd, HI) for i in range(x.shape[1] // LANES)]
    return parts[0] if len(parts) == 1 else jnp.concatenate(parts, axis=-1)


def _head_sums2(x, bd):
    hi, lo = _split2(x)
    return _dot(hi, bd) + _dot(lo, bd)


def _softplus(x):
    return jnp.maximum(x, 0.0) + jnp.log(1.0 + jnp.exp(-jnp.abs(x)))


def _silu(x):
    return x * jax.nn.sigmoid(x)


def _inproj_kernel(x_ref, g_ref, w_ref, o_ref, xn_ref):
    @pl.when(pl.program_id(1) == 0)
    def _():
        x = x_ref[...]
        ms = jnp.mean(x * x, axis=-1, keepdims=True)
        xn_ref[...] = (x * lax.rsqrt(ms + NORM_EPS) * g_ref[...]).astype(BF16)

    o_ref[...] = _dot(xn_ref[...], w_ref[...])


def _inproj(x, gain, w):
    m, d = x.shape
    n = w.shape[1]
    tm = _pick_tile(m, 768)
    tn = _pick_tile(n, INPROJ_TN)
    return pl.pallas_call(
        _inproj_kernel,
        out_shape=jax.ShapeDtypeStruct((m, n), F32),
        grid=(m // tm, n // tn),
        in_specs=[pl.BlockSpec((tm, d), lambda i, j: (i, 0)),
                  pl.BlockSpec((1, d), lambda i, j: (0, 0)),
                  pl.BlockSpec((d, tn), lambda i, j: (0, j))],
        out_specs=pl.BlockSpec((tm, tn), lambda i, j: (i, j)),
        scratch_shapes=[pltpu.VMEM((tm, d), BF16)],
        compiler_params=_cp(("parallel", "arbitrary")),
        name="inproj",
    )(x, gain, w)


def _rwkv_prep_math(z3, zw, p3, pw, vf, prm, has_vres):
    (mu3, muw, w0, w2, a0, a2, kkg, kag, v0, v1, v2) = prm
    rw = w0.shape[1]
    x3 = z3 + (p3 - z3) * mu3
    xw = zw + (pw - zw) * muw
    r = x3[:, :rw]
    k = x3[:, rw:2 * rw]
    v = x3[:, 2 * rw:]
    wd = xw[:, :LANES]
    ad = xw[:, LANES:]
    w_raw = w0 + _dot(jnp.tanh(wd).astype(BF16), w2)
    w_log = -_softplus(-w_raw) - DECAY_OFFSET
    lw = -jnp.exp(w_log)
    a = jax.nn.sigmoid(a0 + _dot(ad.astype(BF16), a2))
    if has_vres:
        lora = _dot(_dot(v.astype(BF16), v1).astype(BF16), v2)
        v = v + (vf - v) * jax.nn.sigmoid(v0 + lora)
    kk = k * kkg
    ss = _head_sums(kk * kk)
    kk = kk / jnp.maximum(jnp.sqrt(ss), 1e-12)
    k2 = k * (1.0 + (a - 1.0) * kag)
    return r, lw, k2, v, kk, kk * a


def _rwkv_prep_kernel(*refs, has_vres, chained):
    z3_ref, zw_ref, p3_ref, pw_ref = refs[:4]
    pos = 4
    vf = None
    if has_vres:
        vf = refs[pos][...]
        pos += 1
    prm = [r[...] for r in refs[pos:pos + 11]]
    outs = refs[pos + 11:]
    z3 = z3_ref[...]
    zw = zw_ref[...]
    if chained:
        first = pl.program_id(0) == 0
        row0 = _iota((z3.shape[0], 1), 0) == 0
        last3 = jnp.where(first, 0.0, p3_ref[7:8, :])
        lastw = jnp.where(first, 0.0, pw_ref[7:8, :])
        p3 = jnp.where(row0, last3, pltpu.roll(z3, 1, 0))
        pw = jnp.where(row0, lastw, pltpu.roll(zw, 1, 0))
    else:
        p3 = p3_ref[...]
        pw = pw_ref[...]
    res = _rwkv_prep_math(z3, zw, p3, pw, vf, prm, has_vres)
    for o_ref, val in zip(outs, res):
        o_ref[...] = val


def _rwkv_prep(z, prev3, prevw, vfirst, prm, *, rw, wa_col, chained):
    m = z.shape[0]
    tm = _pick_tile(m, 256)
    has_vres = vfirst is not None
    wa_blk = wa_col // (2 * LANES)
    if chained:
        sub = tm // 8
        p3_spec = pl.BlockSpec((8, 3 * rw), lambda i: (jnp.maximum(i * sub - 1, 0), 0))
        pw_spec = pl.BlockSpec((8, 2 * LANES), lambda i: (jnp.maximum(i * sub - 1, 0), wa_blk))
        prev_args = (z, z)
    else:
        p3_spec = pl.BlockSpec((tm, 3 * rw), lambda i: (i, 0))
        pw_spec = pl.BlockSpec((tm, 2 * LANES), lambda i: (i, 0))
        prev_args = (prev3, prevw)
    in_specs = [pl.BlockSpec((tm, 3 * rw), lambda i: (i, 0)),
                pl.BlockSpec((tm, 2 * LANES), lambda i: (i, wa_blk)),
                p3_spec, pw_spec]
    args = [z, z, *prev_args]
    if has_vres:
        in_specs.append(pl.BlockSpec((tm, rw), lambda i: (i, 0)))
        args.append(vfirst)
    for p in prm:
        in_specs.append(pl.BlockSpec(p.shape, lambda i: (0, 0)))
        args.append(p)
    out = pl.pallas_call(
        functools.partial(_rwkv_prep_kernel, has_vres=has_vres, chained=chained),
        out_shape=[jax.ShapeDtypeStruct((m, rw), F32)] * 6,
        grid=(m // tm,),
        in_specs=in_specs,
        out_specs=[pl.BlockSpec((tm, rw), lambda i: (i, 0))] * 6,
        compiler_params=_cp(("parallel",)),
        name="rwkv_prep",
    )(*args)
    return out


def _wkv_pair(r, lw, k, v, kk, b, gate, s_prev, rk, lnw, lnb, consts, C):
    tri_b, lane_lo, strict, incl, eye, level_masks, bd = consts
    C2 = 2 * C
    g3 = _dot(tri_b, jnp.concatenate(_split3(lw), axis=-1))
    yield
    g = g3[:, :LANES] + g3[:, LANES:2 * LANES] + g3[:, 2 * LANES:]
    g_last = g[C - 1:C, :]
    e_g = jnp.exp(g)
    e_ng = jnp.exp(-g)
    e_tail = jnp.exp(g_last - g)

    def stack(x):
        return jnp.concatenate([jnp.where(lane_lo, x, 0.0), jnp.where(lane_lo, 0.0, x)], axis=0).astype(BF16)

    left = jnp.concatenate([stack(-kk * jnp.exp(g - lw)), stack(r * e_g)], axis=0)
    right = jnp.concatenate([stack(b * e_ng), stack(k * e_ng), s_prev.astype(BF16)], axis=0)
    v2 = stack(v)
    sc = _dot_nt(left, right)
    yield
    hs_a = sc[:C2, 2 * C2:]
    hs_r = sc[C2:, 2 * C2:]
    m_ab = jnp.where(strict, sc[:C2, :C2], 0.0)
    m_ak = jnp.where(strict, sc[:C2, C2:2 * C2], 0.0)
    p_rb = jnp.where(incl, sc[C2:, :C2], 0.0)
    p_rk = jnp.where(incl, sc[C2:, C2:2 * C2], 0.0)

    rhs = hs_a + _bdot(m_ak, v2)
    bonus = _head_sums2(r * k * rk, bd) * v
    yield

    same, offs = level_masks
    mb = jnp.where(same, m_ab, 0.0)
    x = eye + mb
    pw = mb
    steps = 1
    while steps * 2 < min(8, C):
        pw = _bdot(pw, pw)
        yield
        x = x + _bdot(x, pw)
        yield
        steps *= 2
    for off_mask in offs:
        xb = x.astype(BF16)
        t = _dot(xb, jnp.where(off_mask, m_ab, 0.0).astype(BF16)).astype(BF16)
        yield
        x = x + _dot(t, xb)
        yield

    u = _bdot(x, rhs)
    yield
    uv = jnp.concatenate([u.astype(BF16), v2], axis=0)
    o2 = hs_r + _dot(jnp.concatenate([p_rb, p_rk], axis=1).astype(BF16), uv)
    bk = jnp.concatenate([stack(b * e_tail), stack(k * e_tail)], axis=0)
    uv_t = jnp.concatenate([u.T, v2.astype(F32).T], axis=1).astype(BF16)
    s_new = s_prev * jnp.exp(g_last) + _dot(uv_t, bk)
    yield
    o = o2[:C] + o2[C:]

    mean = _head_sums2(o, bd) * (1.0 / HEAD_DIM)
    yield
    d = o - mean
    var = _head_sums2(d * d, bd) * (1.0 / HEAD_DIM)
    yield
    y = d * lax.rsqrt(var + LNX_EPS) * lnw + lnb + bonus
    return (y * _silu(gate)), s_new


def _run_interleaved(gens):
    results = [None] * len(gens)
    live = list(range(len(gens)))
    while live:
        for i in list(live):
            try:
                next(gens[i])
            except StopIteration as stop:
                results[i] = stop.value
                live.remove(i)
    return results


def _wkv_kernel(r_ref, lw_ref, k_ref, v_ref, kk_ref, b_ref, g_ref, s0_ref, rk_ref, lnw_ref, lnb_ref,
                o_ref, sfin_ref, s_sc, *, chunk, valid_len, npair):
    c = pl.program_id(2)
    n_c = pl.num_programs(2)
    C = chunk
    C2 = 2 * C

    @pl.when(c == 0)
    def _():
        s_sc[...] = s0_ref[...]

    i2 = _iota((C2, C2), 0)
    j2 = _iota((C2, C2), 1)
    base = min(8, C)
    offs = []
    n = base
    while n < C:
        offs.append(((i2 // (2 * n)) == (j2 // (2 * n))) & ((i2 // n) != (j2 // n)))
        n *= 2
    consts = ((_iota((C, C), 0) >= _iota((C, C), 1)).astype(BF16),
              _iota((1, LANES), 1) < HEAD_DIM,
              i2 > j2, i2 >= j2, (i2 == j2).astype(F32),
              ((i2 // base) == (j2 // base), offs),
              _head_block_ones(BF16))
    ok = None
    if valid_len is not None:
        ok = (c * C + _iota((C, 1), 0)) < valid_len

    gens = []
    for p in range(npair):
        sl = slice(p * LANES, (p + 1) * LANES)
        lw, k, kk, b = lw_ref[:, sl], k_ref[:, sl], kk_ref[:, sl], b_ref[:, sl]
        if ok is not None:
            lw = jnp.where(ok, lw, 0.0)
            k = jnp.where(ok, k, 0.0)
            kk = jnp.where(ok, kk, 0.0)
            b = jnp.where(ok, b, 0.0)
        gens.append(_wkv_pair(r_ref[:, sl], lw, k, v_ref[:, sl], kk, b, g_ref[:, sl], s_sc[p],
                              rk_ref[:, sl], lnw_ref[:, sl], lnb_ref[:, sl], consts, C))
    for p, (y, s_new) in enumerate(_run_interleaved(gens)):
        o_ref[:, p * LANES:(p + 1) * LANES] = y.astype(o_ref.dtype)
        s_sc[p] = s_new

    @pl.when(c == n_c - 1)
    def _():
        sfin_ref[...] = s_sc[...]


def _wkv(r, lw, k, v, kk, b, gate, gate_col, s0, rk, lnw, lnb, *, chunk, valid_len, pairs_per_step):
    bsz, t, rw = r.shape
    pw = pairs_per_step
    wid = pw * LANES
    ngrp = rw // wid
    nchunk = t // chunk
    gblk = gate_col // wid
    seq = pl.BlockSpec((None, chunk, wid), lambda bi, g, c: (bi, c, g))
    par = pl.BlockSpec((1, wid), lambda bi, g, c: (0, g))
    st = pl.BlockSpec((None, pw, LANES, LANES), lambda bi, g, c: (bi, g, 0, 0))
    return pl.pallas_call(
        functools.partial(_wkv_kernel, chunk=chunk, valid_len=valid_len, npair=pw),
        out_shape=[jax.ShapeDtypeStruct((bsz, t, rw), BF16),
                   jax.ShapeDtypeStruct((bsz, rw // LANES, LANES, LANES), F32)],
        grid=(bsz, ngrp, nchunk),
        in_specs=[seq] * 6 + [pl.BlockSpec((None, chunk, wid), lambda bi, g, c: (bi, c, gblk + g)),
                              st, par, par, par],
        out_specs=[seq, st],
        scratch_shapes=[pltpu.VMEM((pw, LANES, LANES), F32)],
        compiler_params=_cp(("parallel", "parallel", "arbitrary")),
        name="wkv",
    )(r, lw, k, v, kk, b, gate, s0, rk, lnw, lnb)


def _pack_state(s):
    bsz, h, n, _ = s.shape
    s = s.reshape(bsz, h // 2, 2, n, n)
    z = jnp.zeros_like(s[:, :, 0])
    top = jnp.concatenate([s[:, :, 0], z], axis=-1)
    bot = jnp.concatenate([z, s[:, :, 1]], axis=-1)
    return jnp.concatenate([top, bot], axis=-2)


def _unpack_state(sp):
    bsz, p = sp.shape[:2]
    n = HEAD_DIM
    s = jnp.stack([sp[:, :, :n, :n], sp[:, :, n:, n:]], axis=2)
    return s.reshape(bsz, 2 * p, n, n)


def _fox_prep_kernel(*refs, augment):
    q_ref, k_ref, f_ref, fb_ref, qg_ref, kg_ref = refs[:6]
    q = q_ref[...]
    k = k_ref[...]
    qn = q * lax.rsqrt(_head_sums(q * q) * (1.0 / HEAD_DIM) + NORM_EPS) * qg_ref[...]
    kn = k * lax.rsqrt(_head_sums(k * k) * (1.0 / HEAD_DIM) + NORM_EPS) * kg_ref[...]
    logf = -_softplus(-(f_ref[...] + fb_ref[...]))
    if not augment:
        qn_ref, kn_ref, lf_ref = refs[6:]
        qn_ref[...] = qn
        kn_ref[...] = kn
        lf_ref[...] = logf
        return
    v_ref, pl_ref, sq_ref, sk_ref, cq_ref, ck_ref, cv_ref, kn_ref, lf_ref, qa_ref, ka_ref, va_ref, carry = refs[6:]

    @pl.when(pl.program_id(0) == 0)
    def _():
        carry[...] = jnp.zeros_like(carry)

    tm = q.shape[0]
    tri = (_iota((tm, tm), 0) >= _iota((tm, tm), 1)).astype(F32)
    c = carry[...] + _dot(tri, logf, HI)
    carry[...] = c[tm - 1:tm, :]
    c3 = jnp.concatenate(_split3(c * LOG2E), axis=-1)
    place = pl_ref[...]

    def spread(x):
        xb = x.astype(BF16)
        return jnp.concatenate([_dot(xb[:, i * LANES:(i + 1) * LANES], place) for i in range(x.shape[1] // LANES)],
                               axis=-1)

    kn_ref[...] = kn
    lf_ref[...] = logf
    qa_ref[...] = (spread(qn * (LOG2E * HEAD_DIM ** -0.5)) + _dot(c3, sq_ref[...]) + cq_ref[...]).astype(BF16)
    ka_ref[...] = (spread(kn) + _dot(c3, sk_ref[...]) + ck_ref[...]).astype(BF16)
    va_ref[...] = (spread(v_ref[...]) + cv_ref[...]).astype(BF16)


def _fox_aug_consts(fw):
    nh = fw // HEAD_DIM
    rows = jnp.arange(LANES)
    cols = (rows // HEAD_DIM) * LANES + rows % HEAD_DIM
    place = jnp.zeros((LANES, 2 * LANES), F32).at[rows, cols].set(1.0).astype(BF16)
    h = jnp.arange(nh)
    sq = jnp.zeros((3 * LANES, nh * LANES), F32)
    sk = jnp.zeros((3 * LANES, nh * LANES), F32)
    cq = jnp.zeros((1, nh * LANES), F32)
    ck = jnp.zeros((1, nh * LANES), F32)
    cv = jnp.zeros((1, nh * LANES), F32).at[0, h * LANES + HEAD_DIM].set(1.0)
    for part in range(3):
        sq = sq.at[part * LANES + h, h * LANES + HEAD_DIM + part].set(1.0)
        sk = sk.at[part * LANES + h, h * LANES + HEAD_DIM + 3 + part].set(-1.0)
        cq = cq.at[0, h * LANES + HEAD_DIM + 3 + part].set(1.0)
        ck = ck.at[0, h * LANES + HEAD_DIM + part].set(1.0)
    return place, sq.astype(BF16), sk.astype(BF16), cq, ck, cv


def _fox_prep(z, fb, qg, kg, *, fw, q_col, k_col, v_col, f_col, aug):
    m = z.shape[0]
    tm = _pick_tile(m, 256)
    qb, kb, vblk, fblk = q_col // fw, k_col // fw, v_col // fw, f_col // LANES
    in_specs = [pl.BlockSpec((tm, fw), lambda i: (i, qb)),
                pl.BlockSpec((tm, fw), lambda i: (i, kb)),
                pl.BlockSpec((tm, LANES), lambda i: (i, fblk)),
                pl.BlockSpec((1, LANES), lambda i: (0, 0)),
                pl.BlockSpec((1, fw), lambda i: (0, 0)),
                pl.BlockSpec((1, fw), lambda i: (0, 0))]
    args = [z, z, z, fb, qg, kg]
    row = lambda w: pl.BlockSpec((tm, w), lambda i: (i, 0))
    if aug is None:
        return pl.pallas_call(
            functools.partial(_fox_prep_kernel, augment=False),
            out_shape=[jax.ShapeDtypeStruct((m, fw), F32), jax.ShapeDtypeStruct((m, fw), F32),
                       jax.ShapeDtypeStruct((m, LANES), F32)],
            grid=(m // tm,), in_specs=in_specs, out_specs=[row(fw), row(fw), row(LANES)],
            compiler_params=_cp(("parallel",)), name="fox_prep_s",
        )(*args)
    in_specs.append(pl.BlockSpec((tm, fw), lambda i: (i, vblk)))
    args.append(z)
    for a in aug:
        in_specs.append(pl.BlockSpec(a.shape, lambda i: (0, 0)))
        args.append(a)
    nh = fw // HEAD_DIM
    wide = jax.ShapeDtypeStruct((m, nh * LANES), BF16)
    return pl.pallas_call(
        functools.partial(_fox_prep_kernel, augment=True),
        out_shape=[jax.ShapeDtypeStruct((m, fw), F32), jax.ShapeDtypeStruct((m, LANES), F32), wide, wide, wide],
        grid=(m // tm,), in_specs=in_specs,
        out_specs=[row(fw), row(LANES), row(nh * LANES), row(nh * LANES), row(nh * LANES)],
        scratch_shapes=[pltpu.VMEM((1, LANES), F32)],
        compiler_params=_cp(("arbitrary",)), name="fox_prep_p",
    )(*args)


def _flash_kernel(q_ref, k_ref, v_ref, g_ref, o_ref, m_sc, acc_sc, sa_sc, sb_sc, *, tq, tk):
    qi = pl.program_id(1)
    nsub = tq // tk
    m_sc[...] = jnp.full_like(m_sc, NEG_BIG)
    acc_sc[...] = jnp.zeros_like(acc_sc)
    heads = [slice(hh * LANES, (hh + 1) * LANES) for hh in range(2)]

    def key_off(g, j):
        return pl.multiple_of(g * tq + j * tk, tk)

    def scores(g, buf):
        for hh, sl in enumerate(heads):
            for j in range(nsub):
                buf[hh, j] = _dot_nt(q_ref[:, sl], k_ref[pl.ds(key_off(g, j), tk), sl])
                yield

    def update(hh, g, buf, masks):
        sl = heads[hh]
        ss = [buf[hh, j] for j in range(nsub)]
        if masks is not None:
            ss = [jnp.where(mk, s, NEG_BIG) for s, mk in zip(ss, masks)]
        m_old = m_sc[hh]
        mx = ss[0]
        for s in ss[1:]:
            mx = jnp.maximum(mx, s)
        m_new = jnp.maximum(m_old, jnp.broadcast_to(jnp.max(mx, axis=-1, keepdims=True), m_old.shape))
        yield
        m_rep = jnp.concatenate([m_new] * (tk // LANES), axis=-1)
        acc = acc_sc[hh] * jnp.exp2(m_old - m_new)
        for j, s in enumerate(ss):
            acc = acc + _dot(jnp.exp2(s - m_rep).astype(BF16), v_ref[pl.ds(key_off(g, j), tk), sl])
            yield
        m_sc[hh] = m_new
        acc_sc[hh] = acc

    def step(g, cur, nxt, masks):
        gens = [] if nxt is None else [scores(g + 1, nxt)]
        gens += [update(hh, g, cur, masks) for hh in range(2)]
        _run_interleaved(gens)

    rows = _iota((tq, tk), 0)
    cols = _iota((tq, tk), 1)
    causal = [rows >= cols + j * tk for j in range(nsub)]

    _run_interleaved([scores(0, sa_sc)])

    def body(i, carry):
        step(2 * i, sa_sc, sb_sc, None)
        step(2 * i + 1, sb_sc, sa_sc, None)
        return carry

    lax.fori_loop(0, qi // 2, body, 0)

    @pl.when(qi % 2 == 1)
    def _():
        step(qi - 1, sa_sc, sb_sc, None)
        step(qi, sb_sc, None, causal)

    @pl.when(qi % 2 == 0)
    def _():
        step(qi, sa_sc, None, causal)

    outs = []
    for hh in range(2):
        acc = acc_sc[hh]
        outs.append(acc / acc[:, HEAD_DIM:HEAD_DIM + 1])
    o = jnp.where(_iota((1, LANES), 1) < HEAD_DIM, outs[0], pltpu.roll(outs[1], HEAD_DIM, 1))
    o_ref[...] = (o * _silu(g_ref[...])).astype(o_ref.dtype)


def _flash(qa, ka, va, z, g_col):
    lp = qa.shape[0]
    npair = qa.shape[1] // (2 * LANES)
    tk = 256
    tq = _pick_tile(lp, 768, tk)
    gblk = g_col // LANES
    return pl.pallas_call(
        functools.partial(_flash_kernel, tq=tq, tk=tk),
        out_shape=jax.ShapeDtypeStruct((lp, npair * LANES), BF16),
        grid=(npair, lp // tq),
        in_specs=[pl.BlockSpec((tq, 2 * LANES), lambda p, i: (i, p)),
                  pl.BlockSpec((lp, 2 * LANES), lambda p, i: (0, p)),
                  pl.BlockSpec((lp, 2 * LANES), lambda p, i: (0, p)),
                  pl.BlockSpec((tq, LANES), lambda p, i: (i, gblk + p))],
        out_specs=pl.BlockSpec((tq, LANES), lambda p, i: (i, p)),
        scratch_shapes=[pltpu.VMEM((2, tq, LANES), F32), pltpu.VMEM((2, tq, LANES), F32),
                        pltpu.VMEM((2, tq // tk, tq, tk), F32), pltpu.VMEM((2, tq // tk, tq, tk), F32)],
        compiler_params=_cp(("parallel", "arbitrary")),
        name="fox_flash",
    )(qa, ka, va, z)


def _lfpool_kernel(x_ref, o_ref):
    n = x_ref.shape[1]
    jr = _iota((n, 2 * n), 0)
    jc = _iota((n, 2 * n), 1)
    tri2 = ((jr > jc) | (jc == n)).astype(F32)
    o_ref[...] = _dot(x_ref[...], tri2, HI)


def _lfpool(lft):
    rws, n = lft.shape
    tm = _pick_tile(rws, 2048)
    return pl.pallas_call(
        _lfpool_kernel,
        out_shape=jax.ShapeDtypeStruct((rws, 2 * n), F32),
        grid=(rws // tm,),
        in_specs=[pl.BlockSpec((tm, n), lambda i: (i, 0))],
        out_specs=pl.BlockSpec((tm, 2 * n), lambda i: (i, 0)),
        compiler_params=_cp(("parallel",)), name="fox_lfpool",
    )(lft)


def _decode_kernel(*refs, ts, nheads, pps):
    q_ref = refs[1]
    k_refs = refs[2:2 + pps]
    v_refs = refs[2 + pps:2 + 2 * pps]
    f_refs = refs[2 + 2 * pps:2 + 3 * pps]
    kn_ref, vn_ref, lfn_ref, g_ref, o_ref, m_sc, l_sc, acc_sc, r_sc = refs[2 + 3 * pps:]
    j = pl.program_id(1)
    nj = pl.num_programs(1)
    nrow = nheads * ts
    page = k_refs[0].shape[-1]
    fw = nheads * HEAD_DIM

    @pl.when(j == 0)
    def _():
        m_sc[...] = jnp.full_like(m_sc, NEG_BIG)
        l_sc[...] = jnp.zeros_like(l_sc)
        acc_sc[...] = jnp.zeros_like(acc_sc)
        r_sc[...] = jnp.zeros_like(r_sc)

    q = q_ref[...]

    def per_row(x):
        return jnp.concatenate([x] * ts, axis=0)

    def update(blocks):
        ss = [_dot(q, kt.astype(BF16)) + bias for kt, _, bias in blocks]
        m_old = m_sc[...]
        m_new = m_old
        for s in ss:
            m_new = jnp.maximum(m_new, jnp.max(s, axis=-1, keepdims=True))
        alpha = jnp.exp(m_old - m_new)
        l_new = alpha * l_sc[...]
        acc = acc_sc[...] * alpha
        for s, (_, vt, _) in zip(ss, blocks):
            p = jnp.exp(s - m_new)
            l_new = l_new + jnp.sum(p, axis=-1, keepdims=True)
            acc = acc + _dot_nt(p.astype(BF16), vt.astype(BF16))
        l_sc[...] = l_new
        acc_sc[...] = acc
        m_sc[...] = m_new

    run = r_sc[...]
    blocks = []
    for k_ref, v_ref, f_ref in zip(k_refs, v_refs, f_refs):
        f = f_ref[...]
        blocks.append((k_ref[...].reshape(fw, page), v_ref[...].reshape(fw, page),
                       per_row(f[:, :page] + run)))
        run = run + f[:, page:page + 1]
    r_sc[...] = run
    update(blocks)

    @pl.when(j == nj - 1)
    def _():
        nn = kn_ref.shape[-1]
        upper = (_iota((nn, nn), 0) <= _iota((nn, nn), 1)).astype(F32)
        cum = per_row(_dot(lfn_ref[...], upper, HI))
        tq = _iota((nrow, nn), 0) // nheads
        tk = _iota((nrow, nn), 1)
        update([(kn_ref[...], vn_ref[...], jnp.where((tk <= tq) & (tk < ts), -cum, NEG_BIG))])
        out = acc_sc[...] / l_sc[...]
        own = (_iota((nrow, fw), 0) % nheads) == (_iota((nrow, fw), 1) // HEAD_DIM)
        sel = (_iota((8, nrow), 1) // nheads == _iota((8, nrow), 0)).astype(F32)
        o = _dot(sel, jnp.where(own, out, 0.0), HI)
        o_ref[...] = (o * _silu(g_ref[...])).astype(o_ref.dtype)


def _decode(page_table, qbd, kt_pool, vt_pool, lfo, layer, knew_t, vnew_t, lfn_t, gate, *, ts, nheads, pps):
    bsz, nrow, fw = qbd.shape
    npg = page_table.shape[1]
    page = kt_pool.shape[-1]
    nn = knew_t.shape[-1]

    def pool(shape, which):
        nd = len(shape)
        return pl.BlockSpec((None, None) + shape,
                            lambda b, j, pt: (layer, pt[b, npg - 1 - pps * j - which]) + (0,) * nd)

    per_b = lambda w0, w1: pl.BlockSpec((None, w0, w1), lambda b, j, pt: (b, 0, 0))
    kv = (nheads, HEAD_DIM, page)
    each = range(pps)
    grid_spec = pltpu.PrefetchScalarGridSpec(
        num_scalar_prefetch=1, grid=(bsz, npg // pps),
        in_specs=([per_b(nrow, fw)] + [pool(kv, w) for w in each] + [pool(kv, w) for w in each]
                  + [pool((nheads, 2 * page), w) for w in each]
                  + [per_b(fw, nn), per_b(fw, nn), per_b(nheads, nn), per_b(8, fw)]),
        out_specs=per_b(8, fw),
        scratch_shapes=[pltpu.VMEM((nrow, 1), F32), pltpu.VMEM((nrow, 1), F32), pltpu.VMEM((nrow, fw), F32),
                        pltpu.VMEM((nheads, 1), F32)])
    return pl.pallas_call(
        functools.partial(_decode_kernel, ts=ts, nheads=nheads, pps=pps),
        out_shape=jax.ShapeDtypeStruct((bsz, 8, fw), BF16),
        grid_spec=grid_spec,
        compiler_params=_cp(("parallel", "arbitrary")),
        name="fox_decode",
    )(page_table, qbd, *([kt_pool] * pps), *([vt_pool] * pps), *([lfo] * pps), knew_t, vnew_t, lfn_t, gate)


def _merge_kernel(x_ref, or_ref, of_ref, mr_ref, mf_ref, wpr_ref, wpf_ref, wo_ref, o_ref):
    a = _dot(or_ref[...], wpr_ref[...])
    b = _dot(of_ref[...], wpf_ref[...])
    mixed = jax.nn.sigmoid(mr_ref[...]) * a + jax.nn.sigmoid(mf_ref[...]) * b
    o_ref[...] = x_ref[...] + _dot(mixed.astype(BF16), wo_ref[...])


def _merge(x, o_r, o_f, z, mr_col, mf_col, wpr, wpf, wo):
    m, d = x.shape
    rw = o_r.shape[1]
    fw = o_f.shape[1]
    tm = _pick_tile(m, 256)
    whole = lambda a: pl.BlockSpec(a.shape, lambda i: (0, 0), pipeline_mode=pl.Buffered(1))
    return pl.pallas_call(
        _merge_kernel,
        out_shape=jax.ShapeDtypeStruct((m, d), F32),
        grid=(m // tm,),
        in_specs=[pl.BlockSpec((tm, d), lambda i: (i, 0)),
                  pl.BlockSpec((tm, rw), lambda i: (i, 0)),
                  pl.BlockSpec((tm, fw), lambda i: (i, 0)),
                  pl.BlockSpec((tm, d), lambda i: (i, mr_col // d)),
                  pl.BlockSpec((tm, d), lambda i: (i, mf_col // d)),
                  whole(wpr), whole(wpf), whole(wo)],
        out_specs=pl.BlockSpec((tm, d), lambda i: (i, 0)),
        compiler_params=_cp(("parallel",)),
        name="merge",
    )(x, o_r, o_f, z, z, wpr, wpf, wo)


def _pad_cols(a, width):
    return jnp.pad(a, [(0, 0)] * (a.ndim - 1) + [(0, width - a.shape[-1])])


def kernel(x_prompt, x_sample, cache_k, cache_v, cache_logf, state_wkv, state_shift, page_table, meta_tokens, norm_gain, w_in, r_mu, r_w0, r_w2, r_a0, r_a2, r_v0, r_v1, r_v2, r_kk, r_ka, r_rk, r_lnx_w, r_lnx_b, f_bias, f_qgain, f_kgain, w_proj_r, w_proj_f, w_out):
    bsz, seq, d = x_prompt.shape
    db, ts, _ = x_sample.shape
    depth = w_in.shape[0]
    rw = r_w0.shape[1]
    rh = r_rk.shape[1]
    fh = f_bias.shape[1]
    fw = fh * HEAD_DIM
    lora_w = r_w2.shape[1]
    lora_a = r_a2.shape[1]
    shift_w = 3 * rw + lora_w + lora_a
    page = cache_k.shape[2]
    npool = cache_k.shape[1]
    npg = page_table.shape[1]
    assert bsz == 1 and rw == rh * HEAD_DIM and fw == rw and d == 2 * rw
    assert lora_w <= LANES and lora_a <= LANES and fh <= LANES and ts * fh <= LANES
    pps = next(n for n in (8, 4, 2, 1) if npg % n == 0)

    s_gr = shift_w
    s_q = s_gr + rw
    s_k = s_q + fw
    s_v = s_k + fw
    s_f = s_v + fw
    s_gf = s_f + fh
    s_mr = s_gf + fw
    s_mf = s_mr + d
    c_gr = 3 * rw
    c_q = c_gr + rw
    c_k = c_q + fw
    c_v = c_k + fw
    c_gf = c_v + fw
    c_mr = c_gf + fw
    c_mf = c_mr + d
    c_wa = c_mf + d
    c_f = c_wa + 2 * LANES
    n_pack = _round_up(c_f + LANES, INPROJ_TN)

    def pack_w(w):
        segs = [w[:, :3 * rw], w[:, s_gr:s_gr + rw], w[:, s_q:s_q + fw], w[:, s_k:s_k + fw], w[:, s_v:s_v + fw],
                w[:, s_gf:s_gf + fw], w[:, s_mr:s_mr + d], w[:, s_mf:s_mf + d],
                _pad_cols(w[:, 3 * rw:3 * rw + lora_w], LANES),
                _pad_cols(w[:, 3 * rw + lora_w:shift_w], LANES),
                _pad_cols(w[:, s_f:s_f + fh], n_pack - c_f)]
        return jnp.concatenate(segs, axis=1).astype(BF16)

    def pack_shift(a):
        return a[..., :3 * rw], jnp.concatenate([_pad_cols(a[..., 3 * rw:3 * rw + lora_w], LANES),
                                                 _pad_cols(a[..., 3 * rw + lora_w:], LANES)], axis=-1)

    def unpack_shift(zrow):
        return jnp.concatenate([zrow[..., :3 * rw], zrow[..., c_wa:c_wa + lora_w],
                                zrow[..., c_wa + LANES:c_wa + LANES + lora_a]], axis=-1)

    length = seq + N_META
    lp = _round_up(length, 768) if length > 768 else _round_up(length, 256)
    chunk = 64
    xp = jnp.concatenate([meta_tokens.astype(F32), x_prompt[0], jnp.zeros((lp - length, d), F32)], axis=0)
    xs = x_sample.reshape(db * ts, d)
    aug = _fox_aug_consts(fw)
    s0_p = jnp.zeros((1, rh // 2, LANES, LANES), F32)

    kt_pool = jnp.transpose(cache_k, (0, 1, 3, 4, 2))
    vt_pool = jnp.transpose(cache_v, (0, 1, 3, 4, 2))
    lft = jnp.swapaxes(cache_logf, 2, 3).reshape(depth * npool * fh, page)
    lfo = _lfpool(lft).reshape(depth, npool, fh, 2 * page)

    vf_p = None
    vf_s = None
    outs = {n: [] for n in ("kp", "vp", "lp", "sp", "hp", "ks", "vs", "ls", "ss", "hs")}
    for l in range(depth):
        w_pack = pack_w(w_in[l])
        gain = norm_gain[l][None, :]
        mu3, muw = pack_shift(r_mu[l][None, :])
        if l == 0:
            v0 = jnp.zeros((1, rw), F32)
            v1 = jnp.zeros((rw, LANES), BF16)
            v2 = jnp.zeros((LANES, rw), BF16)
        else:
            v0 = r_v0[l - 1][None, :]
            v1 = _pad_cols(r_v1[l - 1], LANES).astype(BF16)
            v2 = jnp.pad(r_v2[l - 1], ((0, LANES - r_v2.shape[1]), (0, 0))).astype(BF16)
        prm = (mu3, muw, r_w0[l][None, :],
               jnp.pad(r_w2[l], ((0, LANES - lora_w), (0, 0))).astype(BF16),
               r_a0[l][None, :],
               jnp.pad(r_a2[l], ((0, LANES - lora_a), (0, 0))).astype(BF16),
               r_kk[l][None, :], r_ka[l][None, :], v0, v1, v2)
        rk = r_rk[l].reshape(1, rw)
        lnw = r_lnx_w[l][None, :]
        lnb = r_lnx_b[l][None, :]
        fb = _pad_cols(f_bias[l][None, :], LANES)
        qg = jnp.tile(f_qgain[l], fh)[None, :]
        kg = jnp.tile(f_kgain[l], fh)[None, :]
        wpr = w_proj_r[l].astype(BF16)
        wpf = w_proj_f[l].astype(BF16)
        wo = w_out[l].astype(BF16)

        z = _inproj(xp, gain, w_pack)
        r, lw, k2, v, kk, b = _rwkv_prep(z, None, None, vf_p, prm, rw=rw, wa_col=c_wa, chained=True)
        if l == 0:
            vf_p = v
        o_r, s_fin = _wkv(r[None], lw[None], k2[None], v[None], kk[None], b[None], z[None], c_gr,
                          s0_p, rk, lnw, lnb, chunk=chunk, valid_len=length, pairs_per_step=rw // LANES)
        kn, lf, qa, ka, va = _fox_prep(z, fb, qg, kg, fw=fw, q_col=c_q, k_col=c_k, v_col=c_v, f_col=c_f, aug=aug)
        o_f = _flash(qa, ka, va, z, c_gf)
        xp = _merge(xp, o_r[0], o_f, z, c_mr, c_mf, wpr, wpf, wo)
        outs["kp"].append(kn[:length].reshape(1, length, fh, HEAD_DIM))
        outs["vp"].append(z[:length, c_v:c_v + fw].reshape(1, length, fh, HEAD_DIM))
        outs["lp"].append(lf[:length, :fh][None])
        outs["sp"].append(_unpack_state(s_fin))
        outs["hp"].append(unpack_shift(z[length - 1:length]))

        zs = _inproj(xs, gain, w_pack)
        zs3 = zs.reshape(db, ts, n_pack)
        sh3, shw = pack_shift(state_shift[l])
        prev3 = jnp.concatenate([sh3[:, None], zs3[:, :-1, :3 * rw]], axis=1).reshape(db * ts, 3 * rw)
        prevw = jnp.concatenate([shw[:, None], zs3[:, :-1, c_wa:c_wa + 2 * LANES]], axis=1).reshape(db * ts, 2 * LANES)
        res = _rwkv_prep(zs, prev3, prevw, vf_s, prm, rw=rw, wa_col=c_wa, chained=False)
        if l == 0:
            vf_s = res[3]
        pad_t = lambda a: jnp.pad(a.reshape(db, ts, -1), ((0, 0), (0, 8 - ts), (0, 0)))
        r, lw, k2, v, kk, b = (pad_t(a) for a in res)
        gate_r = pad_t(zs[:, c_gr:c_gr + rw])
        o_r, s_fin = _wkv(r, lw, k2, v, kk, b, gate_r, 0, _pack_state(state_wkv[l]), rk, lnw, lnb,
                          chunk=8, valid_len=None, pairs_per_step=rw // LANES)
        o_r = o_r[:, :ts].reshape(db * ts, rw)
        qn, kn, lf = _fox_prep(zs, fb, qg, kg, fw=fw, q_col=c_q, k_col=c_k, v_col=c_v, f_col=c_f, aug=None)
        vn = zs[:, c_v:c_v + fw]
        q4 = qn.reshape(db, ts, fh, HEAD_DIM) * HEAD_DIM ** -0.5
        qbd = jnp.einsum("bthd,hg->bthgd", q4, jnp.eye(fh, dtype=F32)).reshape(db, ts * fh, fw).astype(BF16)
        pad_k = lambda a: _pad_cols(jnp.swapaxes(a.reshape(db, ts, -1), 1, 2), page)
        gate_f = pad_t(zs[:, c_gf:c_gf + fw])
        o_f = _decode(page_table, qbd, kt_pool, vt_pool, lfo, l, pad_k(kn), pad_k(vn), pad_k(lf[:, :fh]), gate_f,
                      ts=ts, nheads=fh, pps=pps)
        o_f = o_f[:, :ts].reshape(db * ts, fw)
        xs = _merge(xs, o_r, o_f, zs, c_mr, c_mf, wpr, wpf, wo)
        outs["ks"].append(kn.reshape(db, ts, fh, HEAD_DIM))
        outs["vs"].append(vn.reshape(db, ts, fh, HEAD_DIM))
        outs["ls"].append(lf[:, :fh].reshape(db, ts, fh))
        outs["ss"].append(_unpack_state(s_fin))
        outs["hs"].append(unpack_shift(zs3[:, -1]))

    y_prompt = xp[N_META:length][None]
    y_sample = xs.reshape(db, ts, d)
    st = lambda n: jnp.stack(outs[n])
    return (y_prompt, y_sample, st("kp"), st("vp"), st("lp"), st("sp"), st("hp"),
            st("ks"), st("vs"), st("ls"), st("ss"), st("hs"))
```

```python
import functools
import math

import jax
import jax.numpy as jnp
from jax import lax
from jax.experimental import pallas as pl
from jax.experimental.pallas import tpu as pltpu

F32 = jnp.float32
BF16 = jnp.bfloat16
HI = lax.Precision.HIGHEST

LANES = 128
HEAD_DIM = 64
N_META = 16
NORM_EPS = 1e-6
LNX_EPS = 1e-5 * HEAD_DIM
DECAY_OFFSET = 0.5
NEG_BIG = -1e30
LOG2E = math.log2(math.e)
VMEM_LIMIT = 56 * 1024 * 1024
INPROJ_TN = 1280


def _cp(sem, vmem=VMEM_LIMIT):
    return pltpu.CompilerParams(dimension_semantics=sem, vmem_limit_bytes=vmem)


def _round_up(x, m):
    return (x + m - 1) // m * m


def _pick_tile(n, cap, mult=128):
    if n <= cap:
        return n
    best = mult
    t = mult
    while t <= cap:
        if n % t == 0:
            best = t
        t += mult
    return best


def _dot(a, b, precision=None):
    return jnp.dot(a, b, preferred_element_type=F32, precision=precision)


def _dot_nt(a, b, precision=None):
    return lax.dot_general(a, b, (((1,), (1,)), ((), ())), preferred_element_type=F32, precision=precision)


def _bdot(a, b):
    return _dot(a.astype(BF16), b.astype(BF16))


def _iota(shape, dim):
    return lax.broadcasted_iota(jnp.int32, shape, dim)


def _split2(x):
    hi = x.astype(BF16)
    return hi, (x - hi.astype(F32)).astype(BF16)


def _split3(x):
    hi = x.astype(BF16)
    r1 = x - hi.astype(F32)
    mid = r1.astype(BF16)
    return hi, mid, (r1 - mid.astype(F32)).astype(BF16)


def _head_block_ones(dtype=F32):
    return (_iota((LANES, LANES), 0) // HEAD_DIM == _iota((LANES, LANES), 1) // HEAD_DIM).astype(dtype)


def _head_sums(x):
    bd = _head_block_ones()
    parts = [_dot(x[:, i * LANES:(i + 1) * LANES], bd, HI) for i in range(x.shape[1] // LANES)]
    return parts[0] if len(parts) == 1 else jnp.concatenate(parts, axis=-1)


def _head_sums2(x, bd):
    hi, lo = _split2(x)
    return _dot(hi, bd) + _dot(lo, bd)


def _softplus(x):
    return jnp.maximum(x, 0.0) + jnp.log(1.0 + jnp.exp(-jnp.abs(x)))


def _silu(x):
    return x * jax.nn.sigmoid(x)


def _inproj_kernel(x_ref, g_ref, w_ref, o_ref, xn_ref):
    @pl.when(pl.program_id(1) == 0)
    def _():
        x = x_ref[...]
        ms = jnp.mean(x * x, axis=-1, keepdims=True)
        xn_ref[...] = (x * lax.rsqrt(ms + NORM_EPS) * g_ref[...]).astype(BF16)

    o_ref[...] = _dot(xn_ref[...], w_ref[...])


def _inproj(x, gain, w, layer):
    m, d = x.shape
    n = w.shape[2]
    tm = _pick_tile(m, 768)
    tn = _pick_tile(n, INPROJ_TN)
    return pl.pallas_call(
        _inproj_kernel,
        out_shape=jax.ShapeDtypeStruct((m, n), F32),
        grid=(m // tm, n // tn),
        in_specs=[pl.BlockSpec((tm, d), lambda i, j: (i, 0)),
                  pl.BlockSpec((1, d), lambda i, j: (0, 0)),
                  pl.BlockSpec((None, d, tn), lambda i, j: (layer, 0, j))],
        out_specs=pl.BlockSpec((tm, tn), lambda i, j: (i, j)),
        scratch_shapes=[pltpu.VMEM((tm, d), BF16)],
        compiler_params=_cp(("parallel", "arbitrary")),
        name="inproj",
    )(x, gain, w)


def _pack_kernel(w_ref, o_ref, *, segs, tail):
    o_ref[:, tail:] = jnp.zeros((o_ref.shape[0], o_ref.shape[1] - tail), o_ref.dtype)
    for src, dst, size in segs:
        o_ref[:, dst:dst + size] = w_ref[:, src:src + size].astype(o_ref.dtype)


def _pack_weights(w, segs, tail, n_pack):
    depth, d, n_in = w.shape
    tm = _pick_tile(d, 256)
    return pl.pallas_call(
        functools.partial(_pack_kernel, segs=segs, tail=tail),
        out_shape=jax.ShapeDtypeStruct((depth, d, n_pack), BF16),
        grid=(depth, d // tm),
        in_specs=[pl.BlockSpec((None, tm, n_in), lambda l, i: (l, i, 0))],
        out_specs=pl.BlockSpec((None, tm, n_pack), lambda l, i: (l, i, 0)),
        compiler_params=_cp(("parallel", "parallel")),
        name="pack_w",
    )(w)


def _rwkv_prep_math(z3, zw, p3, pw, vf, prm, has_vres):
    (mu3, muw, w0, w2, a0, a2, kkg, kag, v0, v1, v2) = prm
    rw = w0.shape[1]
    x3 = z3 + (p3 - z3) * mu3
    xw = zw + (pw - zw) * muw
    r = x3[:, :rw]
    k = x3[:, rw:2 * rw]
    v = x3[:, 2 * rw:]
    wd = xw[:, :LANES]
    ad = xw[:, LANES:]
    w_raw = w0 + _dot(jnp.tanh(wd).astype(BF16), w2)
    w_log = -_softplus(-w_raw) - DECAY_OFFSET
    lw = -jnp.exp(w_log)
    a = jax.nn.sigmoid(a0 + _dot(ad.astype(BF16), a2))
    if has_vres:
        lora = _dot(_dot(v.astype(BF16), v1).astype(BF16), v2)
        v = v + (vf - v) * jax.nn.sigmoid(v0 + lora)
    kk = k * kkg
    ss = _head_sums(kk * kk)
    kk = kk / jnp.maximum(jnp.sqrt(ss), 1e-12)
    k2 = k * (1.0 + (a - 1.0) * kag)
    return r, lw, k2, v, kk, kk * a


def _rwkv_prep_kernel(*refs, has_vres, chained):
    z3_ref, zw_ref, p3_ref, pw_ref = refs[:4]
    pos = 4
    vf = None
    if has_vres:
        vf = refs[pos][...]
        pos += 1
    prm = [r[...] for r in refs[pos:pos + 11]]
    outs = refs[pos + 11:]
    z3 = z3_ref[...]
    zw = zw_ref[...]
    if chained:
        first = pl.program_id(0) == 0
        row0 = _iota((z3.shape[0], 1), 0) == 0
        last3 = jnp.where(first, 0.0, p3_ref[7:8, :])
        lastw = jnp.where(first, 0.0, pw_ref[7:8, :])
        p3 = jnp.where(row0, last3, pltpu.roll(z3, 1, 0))
        pw = jnp.where(row0, lastw, pltpu.roll(zw, 1, 0))
    else:
        p3 = p3_ref[...]
        pw = pw_ref[...]
    res = _rwkv_prep_math(z3, zw, p3, pw, vf, prm, has_vres)
    for o_ref, val in zip(outs, res):
        o_ref[...] = val


def _rwkv_prep(z, prev3, prevw, vfirst, prm, *, rw, wa_col, chained):
    m = z.shape[0]
    tm = _pick_tile(m, 256)
    has_vres = vfirst is not None
    wa_blk = wa_col // (2 * LANES)
    if chained:
        sub = tm // 8
        p3_spec = pl.BlockSpec((8, 3 * rw), lambda i: (jnp.maximum(i * sub - 1, 0), 0))
        pw_spec = pl.BlockSpec((8, 2 * LANES), lambda i: (jnp.maximum(i * sub - 1, 0), wa_blk))
        prev_args = (z, z)
    else:
        p3_spec = pl.BlockSpec((tm, 3 * rw), lambda i: (i, 0))
        pw_spec = pl.BlockSpec((tm, 2 * LANES), lambda i: (i, 0))
        prev_args = (prev3, prevw)
    in_specs = [pl.BlockSpec((tm, 3 * rw), lambda i: (i, 0)),
                pl.BlockSpec((tm, 2 * LANES), lambda i: (i, wa_blk)),
                p3_spec, pw_spec]
    args = [z, z, *prev_args]
    if has_vres:
        in_specs.append(pl.BlockSpec((tm, rw), lambda i: (i, 0)))
        args.append(vfirst)
    for p in prm:
        in_specs.append(pl.BlockSpec(p.shape, lambda i: (0, 0)))
        args.append(p)
    out = pl.pallas_call(
        functools.partial(_rwkv_prep_kernel, has_vres=has_vres, chained=chained),
        out_shape=[jax.ShapeDtypeStruct((m, rw), F32)] * 6,
        grid=(m // tm,),
        in_specs=in_specs,
        out_specs=[pl.BlockSpec((tm, rw), lambda i: (i, 0))] * 6,
        compiler_params=_cp(("parallel",)),
        name="rwkv_prep",
    )(*args)
    return out


def _wkv_pair(r, lw, k, v, kk, b, gate, s_prev, rk, lnw, lnb, consts, C):
    tri_b, lane_lo, strict, incl, eye, level_masks, bd = consts
    C2 = 2 * C
    g3 = _dot(tri_b, jnp.concatenate(_split3(lw), axis=-1))
    yield
    g = g3[:, :LANES] + g3[:, LANES:2 * LANES] + g3[:, 2 * LANES:]
    g_last = g[C - 1:C, :]
    e_g = jnp.exp(g)
    e_ng = jnp.exp(-g)
    e_tail = jnp.exp(g_last - g)

    def stack(x):
        return jnp.concatenate([jnp.where(lane_lo, x, 0.0), jnp.where(lane_lo, 0.0, x)], axis=0).astype(BF16)

    left = jnp.concatenate([stack(-kk * jnp.exp(g - lw)), stack(r * e_g)], axis=0)
    right = jnp.concatenate([stack(b * e_ng), stack(k * e_ng), s_prev.astype(BF16)], axis=0)
    v2 = stack(v)
    sc = _dot_nt(left, right)
    yield
    hs_a = sc[:C2, 2 * C2:]
    hs_r = sc[C2:, 2 * C2:]
    m_ab = jnp.where(strict, sc[:C2, :C2], 0.0)
    m_ak = jnp.where(strict, sc[:C2, C2:2 * C2], 0.0)
    p_rb = jnp.where(incl, sc[C2:, :C2], 0.0)
    p_rk = jnp.where(incl, sc[C2:, C2:2 * C2], 0.0)

    rhs = hs_a + _bdot(m_ak, v2)
    bonus = _head_sums2(r * k * rk, bd) * v
    yield

    same, offs = level_masks
    mb = jnp.where(same, m_ab, 0.0)
    x = eye + mb
    pw = mb
    steps = 1
    while steps * 2 < min(8, C):
        pw = _bdot(pw, pw)
        yield
        x = x + _bdot(x, pw)
        yield
        steps *= 2
    for off_mask in offs:
        xb = x.astype(BF16)
        t = _dot(xb, jnp.where(off_mask, m_ab, 0.0).astype(BF16)).astype(BF16)
        yield
        x = x + _dot(t, xb)
        yield

    u = _bdot(x, rhs)
    yield
    uv = jnp.concatenate([u.astype(BF16), v2], axis=0)
    o2 = hs_r + _dot(jnp.concatenate([p_rb, p_rk], axis=1).astype(BF16), uv)
    bk = jnp.concatenate([stack(b * e_tail), stack(k * e_tail)], axis=0)
    uv_t = jnp.concatenate([u.T, v2.astype(F32).T], axis=1).astype(BF16)
    s_new = s_prev * jnp.exp(g_last) + _dot(uv_t, bk)
    yield
    o = o2[:C] + o2[C:]

    mean = _head_sums2(o, bd) * (1.0 / HEAD_DIM)
    yield
    d = o - mean
    var = _head_sums2(d * d, bd) * (1.0 / HEAD_DIM)
    yield
    y = d * lax.rsqrt(var + LNX_EPS) * lnw + lnb + bonus
    return (y * _silu(gate)), s_new


def _run_interleaved(gens):
    results = [None] * len(gens)
    live = list(range(len(gens)))
    while live:
        for i in list(live):
            try:
                next(gens[i])
            except StopIteration as stop:
                results[i] = stop.value
                live.remove(i)
    return results


def _wkv_kernel(r_ref, lw_ref, k_ref, v_ref, kk_ref, b_ref, g_ref, s0_ref, rk_ref, lnw_ref, lnb_ref,
                o_ref, sfin_ref, s_sc, *, chunk, valid_len, npair):
    c = pl.program_id(2)
    n_c = pl.num_programs(2)
    C = chunk
    C2 = 2 * C

    @pl.when(c == 0)
    def _():
        s_sc[...] = s0_ref[...]

    i2 = _iota((C2, C2), 0)
    j2 = _iota((C2, C2), 1)
    base = min(8, C)
    offs = []
    n = base
    while n < C:
        offs.append(((i2 // (2 * n)) == (j2 // (2 * n))) & ((i2 // n) != (j2 // n)))
        n *= 2
    consts = ((_iota((C, C), 0) >= _iota((C, C), 1)).astype(BF16),
              _iota((1, LANES), 1) < HEAD_DIM,
              i2 > j2, i2 >= j2, (i2 == j2).astype(F32),
              ((i2 // base) == (j2 // base), offs),
              _head_block_ones(BF16))
    ok = None
    if valid_len is not None:
        ok = (c * C + _iota((C, 1), 0)) < valid_len

    gens = []
    for p in range(npair):
        sl = slice(p * LANES, (p + 1) * LANES)
        lw, k, kk, b = lw_ref[:, sl], k_ref[:, sl], kk_ref[:, sl], b_ref[:, sl]
        if ok is not None:
            lw = jnp.where(ok, lw, 0.0)
            k = jnp.where(ok, k, 0.0)
            kk = jnp.where(ok, kk, 0.0)
            b = jnp.where(ok, b, 0.0)
        gens.append(_wkv_pair(r_ref[:, sl], lw, k, v_ref[:, sl], kk, b, g_ref[:, sl], s_sc[p],
                              rk_ref[:, sl], lnw_ref[:, sl], lnb_ref[:, sl], consts, C))
    for p, (y, s_new) in enumerate(_run_interleaved(gens)):
        o_ref[:, p * LANES:(p + 1) * LANES] = y.astype(o_ref.dtype)
        s_sc[p] = s_new

    @pl.when(c == n_c - 1)
    def _():
        sfin_ref[...] = s_sc[...]


def _wkv(r, lw, k, v, kk, b, gate, gate_col, s0, rk, lnw, lnb, *, chunk, valid_len, pairs_per_step):
    bsz, t, rw = r.shape
    pw = pairs_per_step
    wid = pw * LANES
    ngrp = rw // wid
    nchunk = t // chunk
    gblk = gate_col // wid
    seq = pl.BlockSpec((None, chunk, wid), lambda bi, g, c: (bi, c, g))
    par = pl.BlockSpec((1, wid), lambda bi, g, c: (0, g))
    st = pl.BlockSpec((None, pw, LANES, LANES), lambda bi, g, c: (bi, g, 0, 0))
    return pl.pallas_call(
        functools.partial(_wkv_kernel, chunk=chunk, valid_len=valid_len, npair=pw),
        out_shape=[jax.ShapeDtypeStruct((bsz, t, rw), BF16),
                   jax.ShapeDtypeStruct((bsz, rw // LANES, LANES, LANES), F32)],
        grid=(bsz, ngrp, nchunk),
        in_specs=[seq] * 6 + [pl.BlockSpec((None, chunk, wid), lambda bi, g, c: (bi, c, gblk + g)),
                              st, par, par, par],
        out_specs=[seq, st],
        scratch_shapes=[pltpu.VMEM((pw, LANES, LANES), F32)],
        compiler_params=_cp(("parallel", "parallel", "arbitrary")),
        name="wkv",
    )(r, lw, k, v, kk, b, gate, s0, rk, lnw, lnb)


def _pack_state(s):
    bsz, h, n, _ = s.shape
    s = s.reshape(bsz, h // 2, 2, n, n)
    z = jnp.zeros_like(s[:, :, 0])
    top = jnp.concatenate([s[:, :, 0], z], axis=-1)
    bot = jnp.concatenate([z, s[:, :, 1]], axis=-1)
    return jnp.concatenate([top, bot], axis=-2)


def _unpack_state(sp):
    bsz, p = sp.shape[:2]
    n = HEAD_DIM
    s = jnp.stack([sp[:, :, :n, :n], sp[:, :, n:, n:]], axis=2)
    return s.reshape(bsz, 2 * p, n, n)


def _fox_prep_kernel(*refs, augment):
    q_ref, k_ref, f_ref, fb_ref, qg_ref, kg_ref = refs[:6]
    q = q_ref[...]
    k = k_ref[...]
    qn = q * lax.rsqrt(_head_sums(q * q) * (1.0 / HEAD_DIM) + NORM_EPS) * qg_ref[...]
    kn = k * lax.rsqrt(_head_sums(k * k) * (1.0 / HEAD_DIM) + NORM_EPS) * kg_ref[...]
    logf = -_softplus(-(f_ref[...] + fb_ref[...]))
    if not augment:
        qn_ref, kn_ref, lf_ref = refs[6:]
        qn_ref[...] = qn
        kn_ref[...] = kn
        lf_ref[...] = logf
        return
    v_ref, pl_ref, sq_ref, sk_ref, cq_ref, ck_ref, cv_ref, kn_ref, lf_ref, qa_ref, ka_ref, va_ref, carry = refs[6:]

    @pl.when(pl.program_id(0) == 0)
    def _():
        carry[...] = jnp.zeros_like(carry)

    tm = q.shape[0]
    tri = (_iota((tm, tm), 0) >= _iota((tm, tm), 1)).astype(F32)
    c = carry[...] + _dot(tri, logf, HI)
    carry[...] = c[tm - 1:tm, :]
    c3 = jnp.concatenate(_split3(c * LOG2E), axis=-1)
    place = pl_ref[...]

    def spread(x):
        xb = x.astype(BF16)
        return jnp.concatenate([_dot(xb[:, i * LANES:(i + 1) * LANES], place) for i in range(x.shape[1] // LANES)],
                               axis=-1)

    kn_ref[...] = kn
    lf_ref[...] = logf
    qa_ref[...] = (spread(qn * (LOG2E * HEAD_DIM ** -0.5)) + _dot(c3, sq_ref[...]) + cq_ref[...]).astype(BF16)
    ka_ref[...] = (spread(kn) + _dot(c3, sk_ref[...]) + ck_ref[...]).astype(BF16)
    va_ref[...] = (spread(v_ref[...]) + cv_ref[...]).astype(BF16)


def _fox_aug_consts(fw):
    nh = fw // HEAD_DIM
    rows = jnp.arange(LANES)
    cols = (rows // HEAD_DIM) * LANES + rows % HEAD_DIM
    place = jnp.zeros((LANES, 2 * LANES), F32).at[rows, cols].set(1.0).astype(BF16)
    h = jnp.arange(nh)
    sq = jnp.zeros((3 * LANES, nh * LANES), F32)
    sk = jnp.zeros((3 * LANES, nh * LANES), F32)
    cq = jnp.zeros((1, nh * LANES), F32)
    ck = jnp.zeros((1, nh * LANES), F32)
    cv = jnp.zeros((1, nh * LANES), F32).at[0, h * LANES + HEAD_DIM].set(1.0)
    for part in range(3):
        sq = sq.at[part * LANES + h, h * LANES + HEAD_DIM + part].set(1.0)
        sk = sk.at[part * LANES + h, h * LANES + HEAD_DIM + 3 + part].set(-1.0)
        cq = cq.at[0, h * LANES + HEAD_DIM + 3 + part].set(1.0)
        ck = ck.at[0, h * LANES + HEAD_DIM + part].set(1.0)
    return place, sq.astype(BF16), sk.astype(BF16), cq, ck, cv


def _fox_prep(z, fb, qg, kg, *, fw, q_col, k_col, v_col, f_col, aug):
    m = z.shape[0]
    tm = _pick_tile(m, 256)
    qb, kb, vblk, fblk = q_col // fw, k_col // fw, v_col // fw, f_col // LANES
    in_specs = [pl.BlockSpec((tm, fw), lambda i: (i, qb)),
                pl.BlockSpec((tm, fw), lambda i: (i, kb)),
                pl.BlockSpec((tm, LANES), lambda i: (i, fblk)),
                pl.BlockSpec((1, LANES), lambda i: (0, 0)),
                pl.BlockSpec((1, fw), lambda i: (0, 0)),
                pl.BlockSpec((1, fw), lambda i: (0, 0))]
    args = [z, z, z, fb, qg, kg]
    row = lambda w: pl.BlockSpec((tm, w), lambda i: (i, 0))
    if aug is None:
        return pl.pallas_call(
            functools.partial(_fox_prep_kernel, augment=False),
            out_shape=[jax.ShapeDtypeStruct((m, fw), F32), jax.ShapeDtypeStruct((m, fw), F32),
                       jax.ShapeDtypeStruct((m, LANES), F32)],
            grid=(m // tm,), in_specs=in_specs, out_specs=[row(fw), row(fw), row(LANES)],
            compiler_params=_cp(("parallel",)), name="fox_prep_s",
        )(*args)
    in_specs.append(pl.BlockSpec((tm, fw), lambda i: (i, vblk)))
    args.append(z)
    for a in aug:
        in_specs.append(pl.BlockSpec(a.shape, lambda i: (0, 0)))
        args.append(a)
    nh = fw // HEAD_DIM
    wide = jax.ShapeDtypeStruct((m, nh * LANES), BF16)
    return pl.pallas_call(
        functools.partial(_fox_prep_kernel, augment=True),
        out_shape=[jax.ShapeDtypeStruct((m, fw), F32), jax.ShapeDtypeStruct((m, LANES), F32), wide, wide, wide],
        grid=(m // tm,), in_specs=in_specs,
        out_specs=[row(fw), row(LANES), row(nh * LANES), row(nh * LANES), row(nh * LANES)],
        scratch_shapes=[pltpu.VMEM((1, LANES), F32)],
        compiler_params=_cp(("arbitrary",)), name="fox_prep_p",
    )(*args)


def _flash_kernel(q_ref, k_ref, v_ref, g_ref, o_ref, m_sc, acc_sc, sa_sc, sb_sc, *, tq, tk):
    qi = pl.program_id(1)
    nsub = tq // tk
    m_sc[...] = jnp.full_like(m_sc, NEG_BIG)
    acc_sc[...] = jnp.zeros_like(acc_sc)
    heads = [slice(hh * LANES, (hh + 1) * LANES) for hh in range(2)]

    def key_off(g, j):
        return pl.multiple_of(g * tq + j * tk, tk)

    def scores(g, buf):
        for hh, sl in enumerate(heads):
            for j in range(nsub):
                buf[hh, j] = _dot_nt(q_ref[:, sl], k_ref[pl.ds(key_off(g, j), tk), sl])
                yield

    def update(hh, g, buf, masks):
        sl = heads[hh]
        ss = [buf[hh, j] for j in range(nsub)]
        if masks is not None:
            ss = [jnp.where(mk, s, NEG_BIG) for s, mk in zip(ss, masks)]
        m_old = m_sc[hh]
        mx = ss[0]
        for s in ss[1:]:
            mx = jnp.maximum(mx, s)
        m_new = jnp.maximum(m_old, jnp.broadcast_to(jnp.max(mx, axis=-1, keepdims=True), m_old.shape))
        yield
        m_rep = jnp.concatenate([m_new] * (tk // LANES), axis=-1)
        acc = acc_sc[hh] * jnp.exp2(m_old - m_new)
        for j, s in enumerate(ss):
            acc = acc + _dot(jnp.exp2(s - m_rep).astype(BF16), v_ref[pl.ds(key_off(g, j), tk), sl])
            yield
        m_sc[hh] = m_new
        acc_sc[hh] = acc

    def step(g, cur, nxt, masks):
        gens = [] if nxt is None else [scores(g + 1, nxt)]
        gens += [update(hh, g, cur, masks) for hh in range(2)]
        _run_interleaved(gens)

    rows = _iota((tq, tk), 0)
    cols = _iota((tq, tk), 1)
    causal = [rows >= cols + j * tk for j in range(nsub)]

    _run_interleaved([scores(0, sa_sc)])

    def body(i, carry):
        step(2 * i, sa_sc, sb_sc, None)
        step(2 * i + 1, sb_sc, sa_sc, None)
        return carry

    lax.fori_loop(0, qi // 2, body, 0)

    @pl.when(qi % 2 == 1)
    def _():
        step(qi - 1, sa_sc, sb_sc, None)
        step(qi, sb_sc, None, causal)

    @pl.when(qi % 2 == 0)
    def _():
        step(qi, sa_sc, None, causal)

    outs = []
    for hh in range(2):
        acc = acc_sc[hh]
        outs.append(acc / acc[:, HEAD_DIM:HEAD_DIM + 1])
    o = jnp.where(_iota((1, LANES), 1) < HEAD_DIM, outs[0], pltpu.roll(outs[1], HEAD_DIM, 1))
    o_ref[...] = (o * _silu(g_ref[...])).astype(o_ref.dtype)


def _flash(qa, ka, va, z, g_col):
    lp = qa.shape[0]
    npair = qa.shape[1] // (2 * LANES)
    tk = 256
    tq = _pick_tile(lp, 768, tk)
    gblk = g_col // LANES
    return pl.pallas_call(
        functools.partial(_flash_kernel, tq=tq, tk=tk),
        out_shape=jax.ShapeDtypeStruct((lp, npair * LANES), BF16),
        grid=(npair, lp // tq),
        in_specs=[pl.BlockSpec((tq, 2 * LANES), lambda p, i: (i, p)),
                  pl.BlockSpec((lp, 2 * LANES), lambda p, i: (0, p)),
                  pl.BlockSpec((lp, 2 * LANES), lambda p, i: (0, p)),
                  pl.BlockSpec((tq, LANES), lambda p, i: (i, gblk + p))],
        out_specs=pl.BlockSpec((tq, LANES), lambda p, i: (i, p)),
        scratch_shapes=[pltpu.VMEM((2, tq, LANES), F32), pltpu.VMEM((2, tq, LANES), F32),
                        pltpu.VMEM((2, tq // tk, tq, tk), F32), pltpu.VMEM((2, tq // tk, tq, tk), F32)],
        compiler_params=_cp(("parallel", "arbitrary")),
        name="fox_flash",
    )(qa, ka, va, z)


def _lfpool_kernel(x_ref, o_ref):
    n = x_ref.shape[1]
    jr = _iota((n, 2 * n), 0)
    jc = _iota((n, 2 * n), 1)
    tri2 = ((jr > jc) | (jc == n)).astype(F32)
    o_ref[...] = _dot(x_ref[...], tri2, HI)


def _lfpool(lft):
    rws, n = lft.shape
    tm = _pick_tile(rws, 2048)
    return pl.pallas_call(
        _lfpool_kernel,
        out_shape=jax.ShapeDtypeStruct((rws, 2 * n), F32),
        grid=(rws // tm,),
        in_specs=[pl.BlockSpec((tm, n), lambda i: (i, 0))],
        out_specs=pl.BlockSpec((tm, 2 * n), lambda i: (i, 0)),
        compiler_params=_cp(("parallel",)), name="fox_lfpool",
    )(lft)


def _decode_kernel(*refs, ts, nheads, pps):
    q_ref = refs[1]
    k_refs = refs[2:2 + pps]
    v_refs = refs[2 + pps:2 + 2 * pps]
    f_refs = refs[2 + 2 * pps:2 + 3 * pps]
    kn_ref, vn_ref, lfn_ref, g_ref, o_ref, m_sc, l_sc, acc_sc, r_sc = refs[2 + 3 * pps:]
    j = pl.program_id(1)
    nj = pl.num_programs(1)
    nrow = nheads * ts
    page = k_refs[0].shape[-1]
    fw = nheads * HEAD_DIM

    @pl.when(j == 0)
    def _():
        m_sc[...] = jnp.full_like(m_sc, NEG_BIG)
        l_sc[...] = jnp.zeros_like(l_sc)
        acc_sc[...] = jnp.zeros_like(acc_sc)
        r_sc[...] = jnp.zeros_like(r_sc)

    q = q_ref[...]

    def per_row(x):
        return jnp.concatenate([x] * ts, axis=0)

    def update(blocks):
        ss = [_dot(q, kt.astype(BF16)) + bias for kt, _, bias in blocks]
        m_run, l_run, acc = m_sc[...], l_sc[...], acc_sc[...]
        half = (len(blocks) + 1) // 2
        for lo in range(0, len(blocks), half):
            m_new = m_run
            for s in ss[lo:lo + half]:
                m_new = jnp.maximum(m_new, jnp.max(s, axis=-1, keepdims=True))
            alpha = jnp.exp(m_run - m_new)
            l_run = alpha * l_run
            acc = acc * alpha
            for s, (_, vt, _) in zip(ss[lo:lo + half], blocks[lo:lo + half]):
                p = jnp.exp(s - m_new)
                l_run = l_run + jnp.sum(p, axis=-1, keepdims=True)
                acc = acc + _dot_nt(p.astype(BF16), vt.astype(BF16))
            m_run = m_new
        l_sc[...] = l_run
        acc_sc[...] = acc
        m_sc[...] = m_run

    run = r_sc[...]
    blocks = []
    for k_ref, v_ref, f_ref in zip(k_refs, v_refs, f_refs):
        f = f_ref[...]
        blocks.append((k_ref[...].reshape(fw, page), v_ref[...].reshape(fw, page),
                       per_row(f[:, :page] + run)))
        run = run + f[:, page:page + 1]
    r_sc[...] = run
    update(blocks)

    @pl.when(j == nj - 1)
    def _():
        nn = kn_ref.shape[-1]
        upper = (_iota((nn, nn), 0) <= _iota((nn, nn), 1)).astype(F32)
        cum = per_row(_dot(lfn_ref[...], upper, HI))
        tq = _iota((nrow, nn), 0) // nheads
        tk = _iota((nrow, nn), 1)
        update([(kn_ref[...], vn_ref[...], jnp.where((tk <= tq) & (tk < ts), -cum, NEG_BIG))])
        out = acc_sc[...] / l_sc[...]
        own = (_iota((nrow, fw), 0) % nheads) == (_iota((nrow, fw), 1) // HEAD_DIM)
        sel = (_iota((8, nrow), 1) // nheads == _iota((8, nrow), 0)).astype(F32)
        o = _dot(sel, jnp.where(own, out, 0.0), HI)
        o_ref[...] = (o * _silu(g_ref[...])).astype(o_ref.dtype)


def _decode(page_table, qbd, kt_pool, vt_pool, lfo, layer, knew_t, vnew_t, lfn_t, gate, *, ts, nheads, pps):
    bsz, nrow, fw = qbd.shape
    npg = page_table.shape[1]
    page = kt_pool.shape[-1]
    nn = knew_t.shape[-1]

    def pool(shape, which):
        nd = len(shape)
        return pl.BlockSpec((None, None) + shape,
                            lambda b, j, pt: (layer, pt[b, npg - 1 - pps * j - which]) + (0,) * nd)

    per_b = lambda w0, w1: pl.BlockSpec((None, w0, w1), lambda b, j, pt: (b, 0, 0))
    kv = (nheads, HEAD_DIM, page)
    each = range(pps)
    grid_spec = pltpu.PrefetchScalarGridSpec(
        num_scalar_prefetch=1, grid=(bsz, npg // pps),
        in_specs=([per_b(nrow, fw)] + [pool(kv, w) for w in each] + [pool(kv, w) for w in each]
                  + [pool((nheads, 2 * page), w) for w in each]
                  + [per_b(fw, nn), per_b(fw, nn), per_b(nheads, nn), per_b(8, fw)]),
        out_specs=per_b(8, fw),
        scratch_shapes=[pltpu.VMEM((nrow, 1), F32), pltpu.VMEM((nrow, 1), F32), pltpu.VMEM((nrow, fw), F32),
                        pltpu.VMEM((nheads, 1), F32)])
    return pl.pallas_call(
        functools.partial(_decode_kernel, ts=ts, nheads=nheads, pps=pps),
        out_shape=jax.ShapeDtypeStruct((bsz, 8, fw), BF16),
        grid_spec=grid_spec,
        compiler_params=_cp(("parallel", "arbitrary")),
        name="fox_decode",
    )(page_table, qbd, *([kt_pool] * pps), *([vt_pool] * pps), *([lfo] * pps), knew_t, vnew_t, lfn_t, gate)


def _merge_kernel(x_ref, or_ref, of_ref, mr_ref, mf_ref, wpr_ref, wpf_ref, wo_ref, o_ref):
    a = _dot(or_ref[...], wpr_ref[...])
    b = _dot(of_ref[...], wpf_ref[...])
    mixed = jax.nn.sigmoid(mr_ref[...]) * a + jax.nn.sigmoid(mf_ref[...]) * b
    o_ref[...] = x_ref[...] + _dot(mixed.astype(BF16), wo_ref[...])


def _merge(x, o_r, o_f, z, mr_col, mf_col, wpr, wpf, wo):
    m, d = x.shape
    rw = o_r.shape[1]
    fw = o_f.shape[1]
    tm = _pick_tile(m, 256)
    whole = lambda a: pl.BlockSpec(a.shape, lambda i: (0, 0), pipeline_mode=pl.Buffered(1))
    return pl.pallas_call(
        _merge_kernel,
        out_shape=jax.ShapeDtypeStruct((m, d), F32),
        grid=(m // tm,),
        in_specs=[pl.BlockSpec((tm, d), lambda i: (i, 0)),
                  pl.BlockSpec((tm, rw), lambda i: (i, 0)),
                  pl.BlockSpec((tm, fw), lambda i: (i, 0)),
                  pl.BlockSpec((tm, d), lambda i: (i, mr_col // d)),
                  pl.BlockSpec((tm, d), lambda i: (i, mf_col // d)),
                  whole(wpr), whole(wpf), whole(wo)],
        out_specs=pl.BlockSpec((tm, d), lambda i: (i, 0)),
        compiler_params=_cp(("parallel",)),
        name="merge",
    )(x, o_r, o_f, z, z, wpr, wpf, wo)


def _pad_cols(a, width):
    return jnp.pad(a, [(0, 0)] * (a.ndim - 1) + [(0, width - a.shape[-1])])


def kernel(x_prompt, x_sample, cache_k, cache_v, cache_logf, state_wkv, state_shift, page_table, meta_tokens, norm_gain, w_in, r_mu, r_w0, r_w2, r_a0, r_a2, r_v0, r_v1, r_v2, r_kk, r_ka, r_rk, r_lnx_w, r_lnx_b, f_bias, f_qgain, f_kgain, w_proj_r, w_proj_f, w_out):
    bsz, seq, d = x_prompt.shape
    db, ts, _ = x_sample.shape
    depth = w_in.shape[0]
    rw = r_w0.shape[1]
    rh = r_rk.shape[1]
    fh = f_bias.shape[1]
    fw = fh * HEAD_DIM
    lora_w = r_w2.shape[1]
    lora_a = r_a2.shape[1]
    shift_w = 3 * rw + lora_w + lora_a
    page = cache_k.shape[2]
    npool = cache_k.shape[1]
    npg = page_table.shape[1]
    assert bsz == 1 and rw == rh * HEAD_DIM and fw == rw and d == 2 * rw
    assert lora_w <= LANES and lora_a <= LANES and fh <= LANES and ts * fh <= LANES
    pps = next(n for n in (8, 4, 2, 1) if npg % n == 0)

    s_gr = shift_w
    s_q = s_gr + rw
    s_k = s_q + fw
    s_v = s_k + fw
    s_f = s_v + fw
    s_gf = s_f + fh
    s_mr = s_gf + fw
    s_mf = s_mr + d
    c_gr = 3 * rw
    c_q = c_gr + rw
    c_k = c_q + fw
    c_v = c_k + fw
    c_gf = c_v + fw
    c_mr = c_gf + fw
    c_mf = c_mr + d
    c_wa = c_mf + d
    c_f = c_wa + 2 * LANES
    n_pack = _round_up(c_f + LANES, INPROJ_TN)

    segs = ((0, 0, 3 * rw), (s_gr, c_gr, rw), (s_q, c_q, fw), (s_k, c_k, fw), (s_v, c_v, fw), (s_gf, c_gf, fw),
            (s_mr, c_mr, d), (s_mf, c_mf, d), (3 * rw, c_wa, lora_w), (3 * rw + lora_w, c_wa + LANES, lora_a),
            (s_f, c_f, fh))
    w_pack = _pack_weights(w_in, segs, c_wa, n_pack)

    def pack_shift(a):
        return a[..., :3 * rw], jnp.concatenate([_pad_cols(a[..., 3 * rw:3 * rw + lora_w], LANES),
                                                 _pad_cols(a[..., 3 * rw + lora_w:], LANES)], axis=-1)

    def unpack_shift(zrow):
        return jnp.concatenate([zrow[..., :3 * rw], zrow[..., c_wa:c_wa + lora_w],
                                zrow[..., c_wa + LANES:c_wa + LANES + lora_a]], axis=-1)

    length = seq + N_META
    lp = _round_up(length, 768) if length > 768 else _round_up(length, 256)
    chunk = 64
    xp = jnp.concatenate([meta_tokens.astype(F32), x_prompt[0], jnp.zeros((lp - length, d), F32)], axis=0)
    xs = x_sample.reshape(db * ts, d)
    aug = _fox_aug_consts(fw)
    s0_p = jnp.zeros((1, rh // 2, LANES, LANES), F32)

    kt_pool = jnp.transpose(cache_k, (0, 1, 3, 4, 2))
    vt_pool = jnp.transpose(cache_v, (0, 1, 3, 4, 2))
    lft = jnp.swapaxes(cache_logf, 2, 3).reshape(depth * npool * fh, page)
    lfo = _lfpool(lft).reshape(depth, npool, fh, 2 * page)

    vf_p = None
    vf_s = None
    outs = {n: [] for n in ("kp", "vp", "lp", "sp", "hp", "ks", "vs", "ls", "ss", "hs")}
    for l in range(depth):
        gain = norm_gain[l][None, :]
        mu3, muw = pack_shift(r_mu[l][None, :])
        if l == 0:
            v0 = jnp.zeros((1, rw), F32)
            v1 = jnp.zeros((rw, LANES), BF16)
            v2 = jnp.zeros((LANES, rw), BF16)
        else:
            v0 = r_v0[l - 1][None, :]
            v1 = _pad_cols(r_v1[l - 1], LANES).astype(BF16)
            v2 = jnp.pad(r_v2[l - 1], ((0, LANES - r_v2.shape[1]), (0, 0))).astype(BF16)
        prm = (mu3, muw, r_w0[l][None, :],
               jnp.pad(r_w2[l], ((0, LANES - lora_w), (0, 0))).astype(BF16),
               r_a0[l][None, :],
               jnp.pad(r_a2[l], ((0, LANES - lora_a), (0, 0))).astype(BF16),
               r_kk[l][None, :], r_ka[l][None, :], v0, v1, v2)
        rk = r_rk[l].reshape(1, rw)
        lnw = r_lnx_w[l][None, :]
        lnb = r_lnx_b[l][None, :]
        fb = _pad_cols(f_bias[l][None, :], LANES)
        qg = jnp.tile(f_qgain[l], fh)[None, :]
        kg = jnp.tile(f_kgain[l], fh)[None, :]
        wpr = w_proj_r[l].astype(BF16)
        wpf = w_proj_f[l].astype(BF16)
        wo = w_out[l].astype(BF16)

        z = _inproj(xp, gain, w_pack, l)
        r, lw, k2, v, kk, b = _rwkv_prep(z, None, None, vf_p, prm, rw=rw, wa_col=c_wa, chained=True)
        if l == 0:
            vf_p = v
        o_r, s_fin = _wkv(r[None], lw[None], k2[None], v[None], kk[None], b[None], z[None], c_gr,
                          s0_p, rk, lnw, lnb, chunk=chunk, valid_len=length, pairs_per_step=rw // LANES)
        kn, lf, qa, ka, va = _fox_prep(z, fb, qg, kg, fw=fw, q_col=c_q, k_col=c_k, v_col=c_v, f_col=c_f, aug=aug)
        o_f = _flash(qa, ka, va, z, c_gf)
        xp = _merge(xp, o_r[0], o_f, z, c_mr, c_mf, wpr, wpf, wo)
        outs["kp"].append(kn[:length].reshape(1, length, fh, HEAD_DIM))
        outs["vp"].append(z[:length, c_v:c_v + fw].reshape(1, length, fh, HEAD_DIM))
        outs["lp"].append(lf[:length, :fh][None])
        outs["sp"].append(_unpack_state(s_fin))
        outs["hp"].append(unpack_shift(z[length - 1:length]))

        zs = _inproj(xs, gain, w_pack, l)
        zs3 = zs.reshape(db, ts, n_pack)
        sh3, shw = pack_shift(state_shift[l])
        prev3 = jnp.concatenate([sh3[:, None], zs3[:, :-1, :3 * rw]], axis=1).reshape(db * ts, 3 * rw)
        prevw = jnp.concatenate([shw[:, None], zs3[:, :-1, c_wa:c_wa + 2 * LANES]], axis=1).reshape(db * ts, 2 * LANES)
        res = _rwkv_prep(zs, prev3, prevw, vf_s, prm, rw=rw, wa_col=c_wa, chained=False)
        if l == 0:
            vf_s = res[3]
        pad_t = lambda a: jnp.pad(a.reshape(db, ts, -1), ((0, 0), (0, 8 - ts), (0, 0)))
        r, lw, k2, v, kk, b = (pad_t(a) for a in res)
        gate_r = pad_t(zs[:, c_gr:c_gr + rw])
        o_r, s_fin = _wkv(r, lw, k2, v, kk, b, gate_r, 0, _pack_state(state_wkv[l]), rk, lnw, lnb,
                          chunk=8, valid_len=None, pairs_per_step=rw // LANES)
        o_r = o_r[:, :ts].reshape(db * ts, rw)
        qn, kn, lf = _fox_prep(zs, fb, qg, kg, fw=fw, q_col=c_q, k_col=c_k, v_col=c_v, f_col=c_f, aug=None)
        vn = zs[:, c_v:c_v + fw]
        q4 = qn.reshape(db, ts, fh, HEAD_DIM) * HEAD_DIM ** -0.5
        qbd = jnp.einsum("bthd,hg->bthgd", q4, jnp.eye(fh, dtype=F32)).reshape(db, ts * fh, fw).astype(BF16)
        pad_k = lambda a: _pad_cols(jnp.swapaxes(a.reshape(db, ts, -1), 1, 2), page)
        gate_f = pad_t(zs[:, c_gf:c_gf + fw])
        o_f = _decode(page_table, qbd, kt_pool, vt_pool, lfo, l, pad_k(kn), pad_k(vn), pad_k(lf[:, :fh]), gate_f,
                      ts=ts, nheads=fh, pps=pps)
        o_f = o_f[:, :ts].reshape(db * ts, fw)
        xs = _merge(xs, o_r, o_f, zs, c_mr, c_mf, wpr, wpf, wo)
        outs["ks"].append(kn.reshape(db, ts, fh, HEAD_DIM))
        outs["vs"].append(vn.reshape(db, ts, fh, HEAD_DIM))
        outs["ls"].append(lf[:, :fh].reshape(db, ts, fh))
        outs["ss"].append(_unpack_state(s_fin))
        outs["hs"].append(unpack_shift(zs3[:, -1]))

    y_prompt = xp[N_META:length][None]
    y_sample = xs.reshape(db, ts, d)
    st = lambda n: jnp.stack(outs[n])
    return (y_prompt, y_sample, st("kp"), st("vp"), st("lp"), st("sp"), st("hp"),
            st("ks"), st("vs"), st("ls"), st("ss"), st("hs"))
```

```python
import functools
import math

import jax
import jax.numpy as jnp
from jax import lax
from jax.experimental import pallas as pl
from jax.experimental.pallas import tpu as pltpu

F32 = jnp.float32
BF16 = jnp.bfloat16
HI = lax.Precision.HIGHEST

LANES = 128
HEAD_DIM = 64
N_META = 16
NORM_EPS = 1e-6
LNX_EPS = 1e-5 * HEAD_DIM
DECAY_OFFSET = 0.5
NEG_BIG = -1e30
LOG2E = math.log2(math.e)
VMEM_LIMIT = 56 * 1024 * 1024
INPROJ_TN = 1280


def _cp(sem, vmem=VMEM_LIMIT):
    return pltpu.CompilerParams(dimension_semantics=sem, vmem_limit_bytes=vmem)


def _round_up(x, m):
    return (x + m - 1) // m * m


def _pick_tile(n, cap, mult=128):
    if n <= cap:
        return n
    best = mult
    t = mult
    while t <= cap:
        if n % t == 0:
            best = t
        t += mult
    return best


def _dot(a, b, precision=None):
    return jnp.dot(a, b, preferred_element_type=F32, precision=precision)


def _dot_nt(a, b, precision=None):
    return lax.dot_general(a, b, (((1,), (1,)), ((), ())), preferred_element_type=F32, precision=precision)


def _bdot(a, b):
    return _dot(a.astype(BF16), b.astype(BF16))


def _iota(shape, dim):
    return lax.broadcasted_iota(jnp.int32, shape, dim)


def _split2(x):
    hi = x.astype(BF16)
    return hi, (x - hi.astype(F32)).astype(BF16)


def _split3(x):
    hi = x.astype(BF16)
    r1 = x - hi.astype(F32)
    mid = r1.astype(BF16)
    return hi, mid, (r1 - mid.astype(F32)).astype(BF16)


def _head_block_ones(dtype=F32):
    return (_iota((LANES, LANES), 0) // HEAD_DIM == _iota((LANES, LANES), 1) // HEAD_DIM).astype(dtype)


def _head_sums(x):
    bd = _head_block_ones()
    parts = [_dot(x[:, i * LANES:(i + 1) * LANES], bd, HI) for i in range(x.shape[1] // LANES)]
    return parts[0] if len(parts) == 1 else jnp.concatenate(parts, axis=-1)


def _head_sums2(x, bd):
    hi, lo = _split2(x)
    return _dot(hi, bd) + _dot(lo, bd)


def _softplus(x):
    return jnp.maximum(x, 0.0) + jnp.log(1.0 + jnp.exp(-jnp.abs(x)))


def _silu(x):
    return x * jax.nn.sigmoid(x)


def _inproj_kernel(x_ref, g_ref, w_ref, o_ref, xn_ref):
    @pl.when(pl.program_id(1) == 0)
    def _():
        x = x_ref[...]
        ms = jnp.mean(x * x, axis=-1, keepdims=True)
        xn_ref[...] = (x * lax.rsqrt(ms + NORM_EPS) * g_ref[...]).astype(BF16)

    o_ref[...] = _dot(xn_ref[...], w_ref[...])


def _inproj(x, gain, w, layer):
    m, d = x.shape
    n = w.shape[2]
    tm = _pick_tile(m, 768)
    tn = _pick_tile(n, INPROJ_TN)
    return pl.pallas_call(
        _inproj_kernel,
        out_shape=jax.ShapeDtypeStruct((m, n), F32),
        grid=(m // tm, n // tn),
        in_specs=[pl.BlockSpec((tm, d), lambda i, j: (i, 0)),
                  pl.BlockSpec((1, d), lambda i, j: (0, 0)),
                  pl.BlockSpec((None, d, tn), lambda i, j: (layer, 0, j))],
        out_specs=pl.BlockSpec((tm, tn), lambda i, j: (i, j)),
        scratch_shapes=[pltpu.VMEM((tm, d), BF16)],
        compiler_params=_cp(("parallel", "arbitrary")),
        name="inproj",
    )(x, gain, w)


def _pack_kernel(wt_ref, o_ref, *, segs, tail):
    o_ref[:, tail:] = jnp.zeros((o_ref.shape[0], o_ref.shape[1] - tail), o_ref.dtype)
    for src, dst, size in segs:
        rows = _round_up(size, LANES)
        xt = wt_ref[src:src + rows, :].T
        if rows != size:
            xt = jnp.where(_iota((1, rows), 1) < size, xt, 0.0)
        o_ref[:, dst:dst + rows] = xt.astype(o_ref.dtype)


def _pack_weights(wt, segs, tail, n_pack):
    depth, n_in, d = wt.shape
    assert all(src + _round_up(size, LANES) <= n_in for src, _, size in segs)
    tm = _pick_tile(d, 256)
    return pl.pallas_call(
        functools.partial(_pack_kernel, segs=segs, tail=tail),
        out_shape=jax.ShapeDtypeStruct((depth, d, n_pack), BF16),
        grid=(depth, d // tm),
        in_specs=[pl.BlockSpec((None, n_in, tm), lambda l, i: (l, 0, i))],
        out_specs=pl.BlockSpec((None, tm, n_pack), lambda l, i: (l, i, 0)),
        compiler_params=_cp(("parallel", "parallel")),
        name="pack_w",
    )(wt)


def _rwkv_prep_math(z3, zw, p3, pw, vf, prm, has_vres):
    (mu3, muw, w0, w2, a0, a2, kkg, kag, v0, v1, v2) = prm
    rw = w0.shape[1]
    x3 = z3 + (p3 - z3) * mu3
    xw = zw + (pw - zw) * muw
    r = x3[:, :rw]
    k = x3[:, rw:2 * rw]
    v = x3[:, 2 * rw:]
    wd = xw[:, :LANES]
    ad = xw[:, LANES:]
    w_raw = w0 + _dot(jnp.tanh(wd).astype(BF16), w2)
    w_log = -_softplus(-w_raw) - DECAY_OFFSET
    lw = -jnp.exp(w_log)
    a = jax.nn.sigmoid(a0 + _dot(ad.astype(BF16), a2))
    if has_vres:
        lora = _dot(_dot(v.astype(BF16), v1).astype(BF16), v2)
        v = v + (vf - v) * jax.nn.sigmoid(v0 + lora)
    kk = k * kkg
    ss = _head_sums(kk * kk)
    kk = kk / jnp.maximum(jnp.sqrt(ss), 1e-12)
    k2 = k * (1.0 + (a - 1.0) * kag)
    return r, lw, k2, v, kk, kk * a


def _rwkv_prep_kernel(*refs, has_vres, chained):
    z3_ref, zw_ref, p3_ref, pw_ref = refs[:4]
    pos = 4
    vf = None
    if has_vres:
        vf = refs[pos][...]
        pos += 1
    prm = [r[...] for r in refs[pos:pos + 11]]
    outs = refs[pos + 11:]
    z3 = z3_ref[...]
    zw = zw_ref[...]
    if chained:
        first = pl.program_id(0) == 0
        row0 = _iota((z3.shape[0], 1), 0) == 0
        last3 = jnp.where(first, 0.0, p3_ref[7:8, :])
        lastw = jnp.where(first, 0.0, pw_ref[7:8, :])
        p3 = jnp.where(row0, last3, pltpu.roll(z3, 1, 0))
        pw = jnp.where(row0, lastw, pltpu.roll(zw, 1, 0))
    else:
        p3 = p3_ref[...]
        pw = pw_ref[...]
    res = _rwkv_prep_math(z3, zw, p3, pw, vf, prm, has_vres)
    for o_ref, val in zip(outs, res):
        o_ref[...] = val


def _rwkv_prep(z, prev3, prevw, vfirst, prm, *, rw, wa_col, chained):
    m = z.shape[0]
    tm = _pick_tile(m, 256)
    has_vres = vfirst is not None
    wa_blk = wa_col // (2 * LANES)
    if chained:
        sub = tm // 8
        p3_spec = pl.BlockSpec((8, 3 * rw), lambda i: (jnp.maximum(i * sub - 1, 0), 0))
        pw_spec = pl.BlockSpec((8, 2 * LANES), lambda i: (jnp.maximum(i * sub - 1, 0), wa_blk))
        prev_args = (z, z)
    else:
        p3_spec = pl.BlockSpec((tm, 3 * rw), lambda i: (i, 0))
        pw_spec = pl.BlockSpec((tm, 2 * LANES), lambda i: (i, 0))
        prev_args = (prev3, prevw)
    in_specs = [pl.BlockSpec((tm, 3 * rw), lambda i: (i, 0)),
                pl.BlockSpec((tm, 2 * LANES), lambda i: (i, wa_blk)),
                p3_spec, pw_spec]
    args = [z, z, *prev_args]
    if has_vres:
        in_specs.append(pl.BlockSpec((tm, rw), lambda i: (i, 0)))
        args.append(vfirst)
    for p in prm:
        in_specs.append(pl.BlockSpec(p.shape, lambda i: (0, 0)))
        args.append(p)
    out = pl.pallas_call(
        functools.partial(_rwkv_prep_kernel, has_vres=has_vres, chained=chained),
        out_shape=[jax.ShapeDtypeStruct((m, rw), F32)] * 6,
        grid=(m // tm,),
        in_specs=in_specs,
        out_specs=[pl.BlockSpec((tm, rw), lambda i: (i, 0))] * 6,
        compiler_params=_cp(("parallel",)),
        name="rwkv_prep",
    )(*args)
    return out


def _wkv_pair(r, lw, k, v, kk, b, gate, s_prev, rk, lnw, lnb, consts, C):
    tri_b, lane_lo, strict, incl, eye, level_masks, bd = consts
    C2 = 2 * C
    g3 = _dot(tri_b, jnp.concatenate(_split3(lw), axis=-1))
    yield
    g = g3[:, :LANES] + g3[:, LANES:2 * LANES] + g3[:, 2 * LANES:]
    g_last = g[C - 1:C, :]
    e_g = jnp.exp(g)
    e_ng = jnp.exp(-g)
    e_tail = jnp.exp(g_last - g)

    def stack(x):
        return jnp.concatenate([jnp.where(lane_lo, x, 0.0), jnp.where(lane_lo, 0.0, x)], axis=0).astype(BF16)

    left = jnp.concatenate([stack(-kk * jnp.exp(g - lw)), stack(r * e_g)], axis=0)
    right = jnp.concatenate([stack(b * e_ng), stack(k * e_ng), s_prev.astype(BF16)], axis=0)
    v2 = stack(v)
    sc = _dot_nt(left, right)
    yield
    hs_a = sc[:C2, 2 * C2:]
    hs_r = sc[C2:, 2 * C2:]
    m_ab = jnp.where(strict, sc[:C2, :C2], 0.0)
    m_ak = jnp.where(strict, sc[:C2, C2:2 * C2], 0.0)
    p_rb = jnp.where(incl, sc[C2:, :C2], 0.0)
    p_rk = jnp.where(incl, sc[C2:, C2:2 * C2], 0.0)

    rhs = hs_a + _bdot(m_ak, v2)
    bonus = _head_sums2(r * k * rk, bd) * v
    yield

    same, offs = level_masks
    mb = jnp.where(same, m_ab, 0.0)
    x = eye + mb
    pw = mb
    steps = 1
    while steps * 2 < min(8, C):
        pw = _bdot(pw, pw)
        yield
        x = x + _bdot(x, pw)
        yield
        steps *= 2
    for off_mask in offs:
        xb = x.astype(BF16)
        t = _dot(xb, jnp.where(off_mask, m_ab, 0.0).astype(BF16)).astype(BF16)
        yield
        x = x + _dot(t, xb)
        yield

    u = _bdot(x, rhs)
    yield
    uv = jnp.concatenate([u.astype(BF16), v2], axis=0)
    o2 = hs_r + _dot(jnp.concatenate([p_rb, p_rk], axis=1).astype(BF16), uv)
    bk = jnp.concatenate([stack(b * e_tail), stack(k * e_tail)], axis=0)
    uv_t = jnp.concatenate([u.T, v2.astype(F32).T], axis=1).astype(BF16)
    s_new = s_prev * jnp.exp(g_last) + _dot(uv_t, bk)
    yield
    o = o2[:C] + o2[C:]

    mean = _head_sums2(o, bd) * (1.0 / HEAD_DIM)
    yield
    d = o - mean
    var = _head_sums2(d * d, bd) * (1.0 / HEAD_DIM)
    yield
    y = d * lax.rsqrt(var + LNX_EPS) * lnw + lnb + bonus
    return (y * _silu(gate)), s_new


def _run_interleaved(gens):
    results = [None] * len(gens)
    live = list(range(len(gens)))
    while live:
        for i in list(live):
            try:
                next(gens[i])
            except StopIteration as stop:
                results[i] = stop.value
                live.remove(i)
    return results


def _wkv_kernel(r_ref, lw_ref, k_ref, v_ref, kk_ref, b_ref, g_ref, s0_ref, rk_ref, lnw_ref, lnb_ref,
                o_ref, sfin_ref, s_sc, *, chunk, valid_len, npair):
    c = pl.program_id(2)
    n_c = pl.num_programs(2)
    C = chunk
    C2 = 2 * C

    @pl.when(c == 0)
    def _():
        s_sc[...] = s0_ref[...]

    i2 = _iota((C2, C2), 0)
    j2 = _iota((C2, C2), 1)
    base = min(8, C)
    offs = []
    n = base
    while n < C:
        offs.append(((i2 // (2 * n)) == (j2 // (2 * n))) & ((i2 // n) != (j2 // n)))
        n *= 2
    consts = ((_iota((C, C), 0) >= _iota((C, C), 1)).astype(BF16),
              _iota((1, LANES), 1) < HEAD_DIM,
              i2 > j2, i2 >= j2, (i2 == j2).astype(F32),
              ((i2 // base) == (j2 // base), offs),
              _head_block_ones(BF16))
    ok = None
    if valid_len is not None:
        ok = (c * C + _iota((C, 1), 0)) < valid_len

    gens = []
    for p in range(npair):
        sl = slice(p * LANES, (p + 1) * LANES)
        lw, k, kk, b = lw_ref[:, sl], k_ref[:, sl], kk_ref[:, sl], b_ref[:, sl]
        if ok is not None:
            lw = jnp.where(ok, lw, 0.0)
            k = jnp.where(ok, k, 0.0)
            kk = jnp.where(ok, kk, 0.0)
            b = jnp.where(ok, b, 0.0)
        gens.append(_wkv_pair(r_ref[:, sl], lw, k, v_ref[:, sl], kk, b, g_ref[:, sl], s_sc[p],
                              rk_ref[:, sl], lnw_ref[:, sl], lnb_ref[:, sl], consts, C))
    for p, (y, s_new) in enumerate(_run_interleaved(gens)):
        o_ref[:, p * LANES:(p + 1) * LANES] = y.astype(o_ref.dtype)
        s_sc[p] = s_new

    @pl.when(c == n_c - 1)
    def _():
        sfin_ref[...] = s_sc[...]


def _wkv(r, lw, k, v, kk, b, gate, gate_col, s0, rk, lnw, lnb, *, chunk, valid_len, pairs_per_step):
    bsz, t, rw = r.shape
    pw = pairs_per_step
    wid = pw * LANES
    ngrp = rw // wid
    nchunk = t // chunk
    gblk = gate_col // wid
    seq = pl.BlockSpec((None, chunk, wid), lambda bi, g, c: (bi, c, g))
    par = pl.BlockSpec((1, wid), lambda bi, g, c: (0, g))
    st = pl.BlockSpec((None, pw, LANES, LANES), lambda bi, g, c: (bi, g, 0, 0))
    return pl.pallas_call(
        functools.partial(_wkv_kernel, chunk=chunk, valid_len=valid_len, npair=pw),
        out_shape=[jax.ShapeDtypeStruct((bsz, t, rw), BF16),
                   jax.ShapeDtypeStruct((bsz, rw // LANES, LANES, LANES), F32)],
        grid=(bsz, ngrp, nchunk),
        in_specs=[seq] * 6 + [pl.BlockSpec((None, chunk, wid), lambda bi, g, c: (bi, c, gblk + g)),
                              st, par, par, par],
        out_specs=[seq, st],
        scratch_shapes=[pltpu.VMEM((pw, LANES, LANES), F32)],
        compiler_params=_cp(("parallel", "parallel", "arbitrary")),
        name="wkv",
    )(r, lw, k, v, kk, b, gate, s0, rk, lnw, lnb)


def _pack_state(s):
    bsz, h, n, _ = s.shape
    s = s.reshape(bsz, h // 2, 2, n, n)
    z = jnp.zeros_like(s[:, :, 0])
    top = jnp.concatenate([s[:, :, 0], z], axis=-1)
    bot = jnp.concatenate([z, s[:, :, 1]], axis=-1)
    return jnp.concatenate([top, bot], axis=-2)


def _unpack_state(sp):
    bsz, p = sp.shape[:2]
    n = HEAD_DIM
    s = jnp.stack([sp[:, :, :n, :n], sp[:, :, n:, n:]], axis=2)
    return s.reshape(bsz, 2 * p, n, n)


def _fox_prep_kernel(*refs, augment):
    q_ref, k_ref, f_ref, fb_ref, qg_ref, kg_ref = refs[:6]
    q = q_ref[...]
    k = k_ref[...]
    qn = q * lax.rsqrt(_head_sums(q * q) * (1.0 / HEAD_DIM) + NORM_EPS) * qg_ref[...]
    kn = k * lax.rsqrt(_head_sums(k * k) * (1.0 / HEAD_DIM) + NORM_EPS) * kg_ref[...]
    logf = -_softplus(-(f_ref[...] + fb_ref[...]))
    if not augment:
        qn_ref, kn_ref, lf_ref = refs[6:]
        qn_ref[...] = qn
        kn_ref[...] = kn
        lf_ref[...] = logf
        return
    v_ref, pl_ref, sq_ref, sk_ref, cq_ref, ck_ref, cv_ref, kn_ref, lf_ref, qa_ref, ka_ref, va_ref, carry = refs[6:]

    @pl.when(pl.program_id(0) == 0)
    def _():
        carry[...] = jnp.zeros_like(carry)

    tm = q.shape[0]
    tri = (_iota((tm, tm), 0) >= _iota((tm, tm), 1)).astype(F32)
    c = carry[...] + _dot(tri, logf, HI)
    carry[...] = c[tm - 1:tm, :]
    c3 = jnp.concatenate(_split3(c * LOG2E), axis=-1)
    place = pl_ref[...]

    def spread(x):
        xb = x.astype(BF16)
        return jnp.concatenate([_dot(xb[:, i * LANES:(i + 1) * LANES], place) for i in range(x.shape[1] // LANES)],
                               axis=-1)

    kn_ref[...] = kn
    lf_ref[...] = logf
    qa_ref[...] = (spread(qn * (LOG2E * HEAD_DIM ** -0.5)) + _dot(c3, sq_ref[...]) + cq_ref[...]).astype(BF16)
    ka_ref[...] = (spread(kn) + _dot(c3, sk_ref[...]) + ck_ref[...]).astype(BF16)
    va_ref[...] = (spread(v_ref[...]) + cv_ref[...]).astype(BF16)


def _fox_aug_consts(fw):
    nh = fw // HEAD_DIM
    rows = jnp.arange(LANES)
    cols = (rows // HEAD_DIM) * LANES + rows % HEAD_DIM
    place = jnp.zeros((LANES, 2 * LANES), F32).at[rows, cols].set(1.0).astype(BF16)
    h = jnp.arange(nh)
    sq = jnp.zeros((3 * LANES, nh * LANES), F32)
    sk = jnp.zeros((3 * LANES, nh * LANES), F32)
    cq = jnp.zeros((1, nh * LANES), F32)
    ck = jnp.zeros((1, nh * LANES), F32)
    cv = jnp.zeros((1, nh * LANES), F32).at[0, h * LANES + HEAD_DIM].set(1.0)
    for part in range(3):
        sq = sq.at[part * LANES + h, h * LANES + HEAD_DIM + part].set(1.0)
        sk = sk.at[part * LANES + h, h * LANES + HEAD_DIM + 3 + part].set(-1.0)
        cq = cq.at[0, h * LANES + HEAD_DIM + 3 + part].set(1.0)
        ck = ck.at[0, h * LANES + HEAD_DIM + part].set(1.0)
    return place, sq.astype(BF16), sk.astype(BF16), cq, ck, cv


def _fox_prep(z, fb, qg, kg, *, fw, q_col, k_col, v_col, f_col, aug, valid_rows=None):
    m = z.shape[0]
    tm = _pick_tile(m, 256)
    qb, kb, vblk, fblk = q_col // fw, k_col // fw, v_col // fw, f_col // LANES
    in_specs = [pl.BlockSpec((tm, fw), lambda i: (i, qb)),
                pl.BlockSpec((tm, fw), lambda i: (i, kb)),
                pl.BlockSpec((tm, LANES), lambda i: (i, fblk)),
                pl.BlockSpec((1, LANES), lambda i: (0, 0)),
                pl.BlockSpec((1, fw), lambda i: (0, 0)),
                pl.BlockSpec((1, fw), lambda i: (0, 0))]
    args = [z, z, z, fb, qg, kg]
    row = lambda w: pl.BlockSpec((tm, w), lambda i: (i, 0))
    if aug is None:
        return pl.pallas_call(
            functools.partial(_fox_prep_kernel, augment=False),
            out_shape=[jax.ShapeDtypeStruct((m, fw), F32), jax.ShapeDtypeStruct((m, fw), F32),
                       jax.ShapeDtypeStruct((m, LANES), F32)],
            grid=(m // tm,), in_specs=in_specs, out_specs=[row(fw), row(fw), row(LANES)],
            compiler_params=_cp(("parallel",)), name="fox_prep_s",
        )(*args)
    in_specs.append(pl.BlockSpec((tm, fw), lambda i: (i, vblk)))
    args.append(z)
    for a in aug:
        in_specs.append(pl.BlockSpec(a.shape, lambda i: (0, 0)))
        args.append(a)
    nh = fw // HEAD_DIM
    wide = jax.ShapeDtypeStruct((m, nh * LANES), BF16)
    return pl.pallas_call(
        functools.partial(_fox_prep_kernel, augment=True),
        out_shape=[jax.ShapeDtypeStruct((valid_rows or m, fw), F32), jax.ShapeDtypeStruct((m, LANES), F32),
                   wide, wide, wide],
        grid=(m // tm,), in_specs=in_specs,
        out_specs=[row(fw), row(LANES), row(nh * LANES), row(nh * LANES), row(nh * LANES)],
        scratch_shapes=[pltpu.VMEM((1, LANES), F32)],
        compiler_params=_cp(("arbitrary",)), name="fox_prep_p",
    )(*args)


def _flash_kernel(q_ref, k_ref, v_ref, g_ref, o_ref, m_sc, acc_sc, sa_sc, sb_sc, *, tq, tk):
    qi = pl.program_id(1)
    nsub = tq // tk
    m_sc[...] = jnp.full_like(m_sc, NEG_BIG)
    acc_sc[...] = jnp.zeros_like(acc_sc)
    heads = [slice(hh * LANES, (hh + 1) * LANES) for hh in range(2)]

    def key_off(g, j):
        return pl.multiple_of(g * tq + j * tk, tk)

    def scores(g, buf):
        for hh, sl in enumerate(heads):
            for j in range(nsub):
                buf[hh, j] = _dot_nt(q_ref[:, sl], k_ref[pl.ds(key_off(g, j), tk), sl])
                yield

    def update(hh, g, buf, masks):
        sl = heads[hh]
        ss = [buf[hh, j] for j in range(nsub)]
        if masks is not None:
            ss = [jnp.where(mk, s, NEG_BIG) for s, mk in zip(ss, masks)]
        m_old = m_sc[hh]
        mx = ss[0]
        for s in ss[1:]:
            mx = jnp.maximum(mx, s)
        m_new = jnp.maximum(m_old, jnp.broadcast_to(jnp.max(mx, axis=-1, keepdims=True), m_old.shape))
        yield
        m_rep = jnp.concatenate([m_new] * (tk // LANES), axis=-1)
        acc = acc_sc[hh] * jnp.exp2(m_old - m_new)
        for j, s in enumerate(ss):
            acc = acc + _dot(jnp.exp2(s - m_rep).astype(BF16), v_ref[pl.ds(key_off(g, j), tk), sl])
            yield
        m_sc[hh] = m_new
        acc_sc[hh] = acc

    def step(g, cur, nxt, masks):
        gens = [] if nxt is None else [scores(g + 1, nxt)]
        gens += [update(hh, g, cur, masks) for hh in range(2)]
        _run_interleaved(gens)

    rows = _iota((tq, tk), 0)
    cols = _iota((tq, tk), 1)
    causal = [rows >= cols + j * tk for j in range(nsub)]

    _run_interleaved([scores(0, sa_sc)])

    def body(i, carry):
        step(2 * i, sa_sc, sb_sc, None)
        step(2 * i + 1, sb_sc, sa_sc, None)
        return carry

    lax.fori_loop(0, qi // 2, body, 0)

    @pl.when(qi % 2 == 1)
    def _():
        step(qi - 1, sa_sc, sb_sc, None)
        step(qi, sb_sc, None, causal)

    @pl.when(qi % 2 == 0)
    def _():
        step(qi, sa_sc, None, causal)

    outs = []
    for hh in range(2):
        acc = acc_sc[hh]
        outs.append(acc / acc[:, HEAD_DIM:HEAD_DIM + 1])
    o = jnp.where(_iota((1, LANES), 1) < HEAD_DIM, outs[0], pltpu.roll(outs[1], HEAD_DIM, 1))
    o_ref[...] = (o * _silu(g_ref[...])).astype(o_ref.dtype)


def _flash(qa, ka, va, z, g_col):
    lp = qa.shape[0]
    npair = qa.shape[1] // (2 * LANES)
    tk = 256
    tq = _pick_tile(lp, 768, tk)
    gblk = g_col // LANES
    return pl.pallas_call(
        functools.partial(_flash_kernel, tq=tq, tk=tk),
        out_shape=jax.ShapeDtypeStruct((lp, npair * LANES), BF16),
        grid=(npair, lp // tq),
        in_specs=[pl.BlockSpec((tq, 2 * LANES), lambda p, i: (i, p)),
                  pl.BlockSpec((lp, 2 * LANES), lambda p, i: (0, p)),
                  pl.BlockSpec((lp, 2 * LANES), lambda p, i: (0, p)),
                  pl.BlockSpec((tq, LANES), lambda p, i: (i, gblk + p))],
        out_specs=pl.BlockSpec((tq, LANES), lambda p, i: (i, p)),
        scratch_shapes=[pltpu.VMEM((2, tq, LANES), F32), pltpu.VMEM((2, tq, LANES), F32),
                        pltpu.VMEM((2, tq // tk, tq, tk), F32), pltpu.VMEM((2, tq // tk, tq, tk), F32)],
        compiler_params=_cp(("parallel", "arbitrary")),
        name="fox_flash",
    )(qa, ka, va, z)


def _lfpool_kernel(x_ref, o_ref):
    n = x_ref.shape[1]
    jr = _iota((n, 2 * n), 0)
    jc = _iota((n, 2 * n), 1)
    tri2 = ((jr > jc) | (jc == n)).astype(F32)
    o_ref[...] = _dot(x_ref[...], tri2, HI)


def _lfpool(lft):
    rws, n = lft.shape
    tm = _pick_tile(rws, 2048)
    return pl.pallas_call(
        _lfpool_kernel,
        out_shape=jax.ShapeDtypeStruct((rws, 2 * n), F32),
        grid=(rws // tm,),
        in_specs=[pl.BlockSpec((tm, n), lambda i: (i, 0))],
        out_specs=pl.BlockSpec((tm, 2 * n), lambda i: (i, 0)),
        compiler_params=_cp(("parallel",)), name="fox_lfpool",
    )(lft)


def _decode_kernel(*refs, ts, nheads, pps):
    q_ref = refs[1]
    k_refs = refs[2:2 + pps]
    v_refs = refs[2 + pps:2 + 2 * pps]
    f_refs = refs[2 + 2 * pps:2 + 3 * pps]
    kn_ref, vn_ref, lfn_ref, g_ref, o_ref, m_sc, l_sc, acc_sc, r_sc = refs[2 + 3 * pps:]
    j = pl.program_id(1)
    nj = pl.num_programs(1)
    nrow = nheads * ts
    page = k_refs[0].shape[-1]
    fw = nheads * HEAD_DIM

    @pl.when(j == 0)
    def _():
        m_sc[...] = jnp.full_like(m_sc, NEG_BIG)
        l_sc[...] = jnp.zeros_like(l_sc)
        acc_sc[...] = jnp.zeros_like(acc_sc)
        r_sc[...] = jnp.zeros_like(r_sc)

    q = q_ref[...]

    def per_row(x):
        return jnp.concatenate([x] * ts, axis=0)

    def update(blocks):
        ss = [_dot(q, kt.astype(BF16)) + bias for kt, _, bias in blocks]
        m_run, l_run, acc = m_sc[...], l_sc[...], acc_sc[...]
        half = (len(blocks) + 1) // 2
        for lo in range(0, len(blocks), half):
            m_new = m_run
            for s in ss[lo:lo + half]:
                m_new = jnp.maximum(m_new, jnp.max(s, axis=-1, keepdims=True))
            alpha = jnp.exp(m_run - m_new)
            l_run = alpha * l_run
            acc = acc * alpha
            for s, (_, vt, _) in zip(ss[lo:lo + half], blocks[lo:lo + half]):
                p = jnp.exp(s - m_new)
                l_run = l_run + jnp.sum(p, axis=-1, keepdims=True)
                acc = acc + _dot_nt(p.astype(BF16), vt.astype(BF16))
            m_run = m_new
        l_sc[...] = l_run
        acc_sc[...] = acc
        m_sc[...] = m_run

    run = r_sc[...]
    blocks = []
    for k_ref, v_ref, f_ref in zip(k_refs, v_refs, f_refs):
        f = f_ref[...]
        blocks.append((k_ref[...].reshape(fw, page), v_ref[...].reshape(fw, page),
                       per_row(f[:, :page] + run)))
        run = run + f[:, page:page + 1]
    r_sc[...] = run
    update(blocks)

    @pl.when(j == nj - 1)
    def _():
        nn = kn_ref.shape[-1]
        upper = (_iota((nn, nn), 0) <= _iota((nn, nn), 1)).astype(F32)
        cum = per_row(_dot(lfn_ref[...], upper, HI))
        tq = _iota((nrow, nn), 0) // nheads
        tk = _iota((nrow, nn), 1)
        update([(kn_ref[...], vn_ref[...], jnp.where((tk <= tq) & (tk < ts), -cum, NEG_BIG))])
        out = acc_sc[...] / l_sc[...]
        own = (_iota((nrow, fw), 0) % nheads) == (_iota((nrow, fw), 1) // HEAD_DIM)
        sel = (_iota((8, nrow), 1) // nheads == _iota((8, nrow), 0)).astype(F32)
        o = _dot(sel, jnp.where(own, out, 0.0), HI)
        o_ref[...] = (o * _silu(g_ref[...])).astype(o_ref.dtype)


def _decode(page_table, qbd, kt_pool, vt_pool, lfo, layer, knew_t, vnew_t, lfn_t, gate, *, ts, nheads, pps):
    bsz, nrow, fw = qbd.shape
    npg = page_table.shape[1]
    page = kt_pool.shape[-1]
    nn = knew_t.shape[-1]

    def pool(shape, which):
        nd = len(shape)
        return pl.BlockSpec((None, None) + shape,
                            lambda b, j, pt: (layer, pt[b, npg - 1 - pps * j - which]) + (0,) * nd)

    per_b = lambda w0, w1: pl.BlockSpec((None, w0, w1), lambda b, j, pt: (b, 0, 0))
    kv = (nheads, HEAD_DIM, page)
    each = range(pps)
    grid_spec = pltpu.PrefetchScalarGridSpec(
        num_scalar_prefetch=1, grid=(bsz, npg // pps),
        in_specs=([per_b(nrow, fw)] + [pool(kv, w) for w in each] + [pool(kv, w) for w in each]
                  + [pool((nheads, 2 * page), w) for w in each]
                  + [per_b(fw, nn), per_b(fw, nn), per_b(nheads, nn), per_b(8, fw)]),
        out_specs=per_b(8, fw),
        scratch_shapes=[pltpu.VMEM((nrow, 1), F32), pltpu.VMEM((nrow, 1), F32), pltpu.VMEM((nrow, fw), F32),
                        pltpu.VMEM((nheads, 1), F32)])
    return pl.pallas_call(
        functools.partial(_decode_kernel, ts=ts, nheads=nheads, pps=pps),
        out_shape=jax.ShapeDtypeStruct((bsz, 8, fw), BF16),
        grid_spec=grid_spec,
        compiler_params=_cp(("parallel", "arbitrary")),
        name="fox_decode",
    )(page_table, qbd, *([kt_pool] * pps), *([vt_pool] * pps), *([lfo] * pps), knew_t, vnew_t, lfn_t, gate)


def _merge_kernel(x_ref, or_ref, of_ref, mr_ref, mf_ref, wpr_ref, wpf_ref, wo_ref, o_ref):
    a = _dot(or_ref[...], wpr_ref[...])
    b = _dot(of_ref[...], wpf_ref[...])
    mixed = jax.nn.sigmoid(mr_ref[...]) * a + jax.nn.sigmoid(mf_ref[...]) * b
    o_ref[...] = x_ref[...] + _dot(mixed.astype(BF16), wo_ref[...])


def _merge(x, o_r, o_f, z, mr_col, mf_col, wpr, wpf, wo):
    m, d = x.shape
    rw = o_r.shape[1]
    fw = o_f.shape[1]
    tm = _pick_tile(m, 256)
    whole = lambda a: pl.BlockSpec(a.shape, lambda i: (0, 0), pipeline_mode=pl.Buffered(1))
    return pl.pallas_call(
        _merge_kernel,
        out_shape=jax.ShapeDtypeStruct((m, d), F32),
        grid=(m // tm,),
        in_specs=[pl.BlockSpec((tm, d), lambda i: (i, 0)),
                  pl.BlockSpec((tm, rw), lambda i: (i, 0)),
                  pl.BlockSpec((tm, fw), lambda i: (i, 0)),
                  pl.BlockSpec((tm, d), lambda i: (i, mr_col // d)),
                  pl.BlockSpec((tm, d), lambda i: (i, mf_col // d)),
                  whole(wpr), whole(wpf), whole(wo)],
        out_specs=pl.BlockSpec((tm, d), lambda i: (i, 0)),
        compiler_params=_cp(("parallel",)),
        name="merge",
    )(x, o_r, o_f, z, z, wpr, wpf, wo)


def _pad_cols(a, width):
    return jnp.pad(a, [(0, 0)] * (a.ndim - 1) + [(0, width - a.shape[-1])])


def kernel(x_prompt, x_sample, cache_k, cache_v, cache_logf, state_wkv, state_shift, page_table, meta_tokens, norm_gain, w_in, r_mu, r_w0, r_w2, r_a0, r_a2, r_v0, r_v1, r_v2, r_kk, r_ka, r_rk, r_lnx_w, r_lnx_b, f_bias, f_qgain, f_kgain, w_proj_r, w_proj_f, w_out):
    bsz, seq, d = x_prompt.shape
    db, ts, _ = x_sample.shape
    depth = w_in.shape[0]
    rw = r_w0.shape[1]
    rh = r_rk.shape[1]
    fh = f_bias.shape[1]
    fw = fh * HEAD_DIM
    lora_w = r_w2.shape[1]
    lora_a = r_a2.shape[1]
    shift_w = 3 * rw + lora_w + lora_a
    page = cache_k.shape[2]
    npool = cache_k.shape[1]
    npg = page_table.shape[1]
    assert bsz == 1 and rw == rh * HEAD_DIM and fw == rw and d == 2 * rw
    assert lora_w <= LANES and lora_a <= LANES and fh <= LANES and ts * fh <= LANES
    pps = next(n for n in (8, 4, 2, 1) if npg % n == 0)

    s_gr = shift_w
    s_q = s_gr + rw
    s_k = s_q + fw
    s_v = s_k + fw
    s_f = s_v + fw
    s_gf = s_f + fh
    s_mr = s_gf + fw
    s_mf = s_mr + d
    c_gr = 3 * rw
    c_q = c_gr + rw
    c_k = c_q + fw
    c_v = c_k + fw
    c_gf = c_v + fw
    c_mr = c_gf + fw
    c_mf = c_mr + d
    c_wa = c_mf + d
    c_f = c_wa + 2 * LANES
    n_pack = _round_up(c_f + LANES, INPROJ_TN)

    segs = ((0, 0, 3 * rw), (s_gr, c_gr, rw), (s_q, c_q, fw), (s_k, c_k, fw), (s_v, c_v, fw), (s_gf, c_gf, fw),
            (s_mr, c_mr, d), (s_mf, c_mf, d), (3 * rw, c_wa, lora_w), (3 * rw + lora_w, c_wa + LANES, lora_a),
            (s_f, c_f, fh))
    w_pack = _pack_weights(jnp.swapaxes(w_in, 1, 2), segs, c_wa, n_pack)

    def pack_shift(a):
        return a[..., :3 * rw], jnp.concatenate([_pad_cols(a[..., 3 * rw:3 * rw + lora_w], LANES),
                                                 _pad_cols(a[..., 3 * rw + lora_w:], LANES)], axis=-1)

    def unpack_shift(zrow):
        return jnp.concatenate([zrow[..., :3 * rw], zrow[..., c_wa:c_wa + lora_w],
                                zrow[..., c_wa + LANES:c_wa + LANES + lora_a]], axis=-1)

    length = seq + N_META
    lp = _round_up(length, 768) if length > 768 else _round_up(length, 256)
    chunk = 64
    xp = jnp.concatenate([meta_tokens.astype(F32), x_prompt[0], jnp.zeros((lp - length, d), F32)], axis=0)
    xs = x_sample.reshape(db * ts, d)
    aug = _fox_aug_consts(fw)
    s0_p = jnp.zeros((1, rh // 2, LANES, LANES), F32)

    kt_pool = jnp.transpose(cache_k, (0, 1, 3, 4, 2))
    vt_pool = jnp.transpose(cache_v, (0, 1, 3, 4, 2))
    lft = jnp.swapaxes(cache_logf, 2, 3).reshape(depth * npool * fh, page)
    lfo = _lfpool(lft).reshape(depth, npool, fh, 2 * page)

    vf_p = None
    vf_s = None
    outs = {n: [] for n in ("kp", "vp", "lp", "sp", "hp", "ks", "vs", "ls", "ss", "hs")}
    for l in range(depth):
        gain = norm_gain[l][None, :]
        mu3, muw = pack_shift(r_mu[l][None, :])
        if l == 0:
            v0 = jnp.zeros((1, rw), F32)
            v1 = jnp.zeros((rw, LANES), BF16)
            v2 = jnp.zeros((LANES, rw), BF16)
        else:
            v0 = r_v0[l - 1][None, :]
            v1 = _pad_cols(r_v1[l - 1], LANES).astype(BF16)
            v2 = jnp.pad(r_v2[l - 1], ((0, LANES - r_v2.shape[1]), (0, 0))).astype(BF16)
        prm = (mu3, muw, r_w0[l][None, :],
               jnp.pad(r_w2[l], ((0, LANES - lora_w), (0, 0))).astype(BF16),
               r_a0[l][None, :],
               jnp.pad(r_a2[l], ((0, LANES - lora_a), (0, 0))).astype(BF16),
               r_kk[l][None, :], r_ka[l][None, :], v0, v1, v2)
        rk = r_rk[l].reshape(1, rw)
        lnw = r_lnx_w[l][None, :]
        lnb = r_lnx_b[l][None, :]
        fb = _pad_cols(f_bias[l][None, :], LANES)
        qg = jnp.tile(f_qgain[l], fh)[None, :]
        kg = jnp.tile(f_kgain[l], fh)[None, :]
        wpr = w_proj_r[l].astype(BF16)
        wpf = w_proj_f[l].astype(BF16)
        wo = w_out[l].astype(BF16)

        z = _inproj(xp, gain, w_pack, l)
        r, lw, k2, v, kk, b = _rwkv_prep(z, None, None, vf_p, prm, rw=rw, wa_col=c_wa, chained=True)
        if l == 0:
            vf_p = v
        o_r, s_fin = _wkv(r[None], lw[None], k2[None], v[None], kk[None], b[None], z[None], c_gr,
                          s0_p, rk, lnw, lnb, chunk=chunk, valid_len=length, pairs_per_step=rw // LANES)
        kn, lf, qa, ka, va = _fox_prep(z, fb, qg, kg, fw=fw, q_col=c_q, k_col=c_k, v_col=c_v, f_col=c_f, aug=aug,
                                       valid_rows=length)
        o_f = _flash(qa, ka, va, z, c_gf)
        xp = _merge(xp, o_r[0], o_f, z, c_mr, c_mf, wpr, wpf, wo)
        outs["kp"].append(kn.reshape(1, length, fh, HEAD_DIM))
        outs["vp"].append(z[:length, c_v:c_v + fw].reshape(1, length, fh, HEAD_DIM))
        outs["lp"].append(lf[:length, :fh][None])
        outs["sp"].append(_unpack_state(s_fin))
        outs["hp"].append(unpack_shift(z[length - 1:length]))

        zs = _inproj(xs, gain, w_pack, l)
        zs3 = zs.reshape(db, ts, n_pack)
        sh3, shw = pack_shift(state_shift[l])
        prev3 = jnp.concatenate([sh3[:, None], zs3[:, :-1, :3 * rw]], axis=1).reshape(db * ts, 3 * rw)
        prevw = jnp.concatenate([shw[:, None], zs3[:, :-1, c_wa:c_wa + 2 * LANES]], axis=1).reshape(db * ts, 2 * LANES)
        res = _rwkv_prep(zs, prev3, prevw, vf_s, prm, rw=rw, wa_col=c_wa, chained=False)
        if l == 0:
            vf_s = res[3]
        pad_t = lambda a: jnp.pad(a.reshape(db, ts, -1), ((0, 0), (0, 8 - ts), (0, 0)))
        r, lw, k2, v, kk, b = (pad_t(a) for a in res)
        gate_r = pad_t(zs[:, c_gr:c_gr + rw])
        o_r, s_fin = _wkv(r, lw, k2, v, kk, b, gate_r, 0, _pack_state(state_wkv[l]), rk, lnw, lnb,
                          chunk=8, valid_len=None, pairs_per_step=rw // LANES)
        o_r = o_r[:, :ts].reshape(db * ts, rw)
        qn, kn, lf = _fox_prep(zs, fb, qg, kg, fw=fw, q_col=c_q, k_col=c_k, v_col=c_v, f_col=c_f, aug=None)
        vn = zs[:, c_v:c_v + fw]
        q4 = qn.reshape(db, ts, fh, HEAD_DIM) * HEAD_DIM ** -0.5
        qbd = jnp.einsum("bthd,hg->bthgd", q4, jnp.eye(fh, dtype=F32)).reshape(db, ts * fh, fw).astype(BF16)
        pad_k = lambda a: _pad_cols(jnp.swapaxes(a.reshape(db, ts, -1), 1, 2), page)
        gate_f = pad_t(zs[:, c_gf:c_gf + fw])
        o_f = _decode(page_table, qbd, kt_pool, vt_pool, lfo, l, pad_k(kn), pad_k(vn), pad_k(lf[:, :fh]), gate_f,
                      ts=ts, nheads=fh, pps=pps)
        o_f = o_f[:, :ts].reshape(db * ts, fw)
        xs = _merge(xs, o_r, o_f, zs, c_mr, c_mf, wpr, wpf, wo)
        outs["ks"].append(kn.reshape(db, ts, fh, HEAD_DIM))
        outs["vs"].append(vn.reshape(db, ts, fh, HEAD_DIM))
        outs["ls"].append(lf[:, :fh].reshape(db, ts, fh))
        outs["ss"].append(_unpack_state(s_fin))
        outs["hs"].append(unpack_shift(zs3[:, -1]))

    y_prompt = xp[N_META:length][None]
    y_sample = xs.reshape(db, ts, d)
    st = lambda n: jnp.stack(outs[n])
    return (y_prompt, y_sample, st("kp"), st("vp"), st("lp"), st("sp"), st("hp"),
            st("ks"), st("vs"), st("ls"), st("ss"), st("hs"))
```

```python
import functools
import math

import jax
import jax.numpy as jnp
from jax import lax
from jax.experimental import pallas as pl
from jax.experimental.pallas import tpu as pltpu

F32 = jnp.float32
BF16 = jnp.bfloat16
HI = lax.Precision.HIGHEST

LANES = 128
HEAD_DIM = 64
N_META = 16
NORM_EPS = 1e-6
LNX_EPS = 1e-5 * HEAD_DIM
DECAY_OFFSET = 0.5
NEG_BIG = -1e30
LOG2E = math.log2(math.e)
VMEM_LIMIT = 56 * 1024 * 1024
INPROJ_TN = 1280


def _cp(sem, vmem=VMEM_LIMIT):
    return pltpu.CompilerParams(dimension_semantics=sem, vmem_limit_bytes=vmem)


def _round_up(x, m):
    return (x + m - 1) // m * m


def _pick_tile(n, cap, mult=128):
    if n <= cap:
        return n
    best = mult
    t = mult
    while t <= cap:
        if n % t == 0:
            best = t
        t += mult
    return best


def _dot(a, b, precision=None):
    return jnp.dot(a, b, preferred_element_type=F32, precision=precision)


def _dot_nt(a, b, precision=None):
    return lax.dot_general(a, b, (((1,), (1,)), ((), ())), preferred_element_type=F32, precision=precision)


def _bdot(a, b):
    return _dot(a.astype(BF16), b.astype(BF16))


def _iota(shape, dim):
    return lax.broadcasted_iota(jnp.int32, shape, dim)


def _split2(x):
    hi = x.astype(BF16)
    return hi, (x - hi.astype(F32)).astype(BF16)


def _split3(x):
    hi = x.astype(BF16)
    r1 = x - hi.astype(F32)
    mid = r1.astype(BF16)
    return hi, mid, (r1 - mid.astype(F32)).astype(BF16)


def _head_block_ones(dtype=F32):
    return (_iota((LANES, LANES), 0) // HEAD_DIM == _iota((LANES, LANES), 1) // HEAD_DIM).astype(dtype)


def _head_sums(x):
    bd = _head_block_ones(BF16)
    parts = [_head_sums2(x[:, i * LANES:(i + 1) * LANES], bd) for i in range(x.shape[1] // LANES)]
    return parts[0] if len(parts) == 1 else jnp.concatenate(parts, axis=-1)


def _head_sums2(x, bd):
    hi, lo = _split2(x)
    return _dot(hi, bd) + _dot(lo, bd)


def _softplus(x):
    return jnp.maximum(x, 0.0) + jnp.log(1.0 + jnp.exp(-jnp.abs(x)))


def _silu(x):
    return x * jax.nn.sigmoid(x)


def _inproj_kernel(x_ref, g_ref, w_ref, o_ref, xn_ref):
    @pl.when(pl.program_id(1) == 0)
    def _():
        x = x_ref[...]
        ms = jnp.mean(x * x, axis=-1, keepdims=True)
        xn_ref[...] = (x * lax.rsqrt(ms + NORM_EPS) * g_ref[...]).astype(BF16)

    o_ref[...] = _dot(xn_ref[...], w_ref[...])


def _inproj(x, gain, w, layer):
    m, d = x.shape
    n = w.shape[2]
    tm = _pick_tile(m, 768)
    tn = _pick_tile(n, INPROJ_TN)
    return pl.pallas_call(
        _inproj_kernel,
        out_shape=jax.ShapeDtypeStruct((m, n), F32),
        grid=(m // tm, n // tn),
        in_specs=[pl.BlockSpec((tm, d), lambda i, j: (i, 0)),
                  pl.BlockSpec((1, d), lambda i, j: (0, 0)),
                  pl.BlockSpec((None, d, tn), lambda i, j: (layer, 0, j))],
        out_specs=pl.BlockSpec((tm, tn), lambda i, j: (i, j)),
        scratch_shapes=[pltpu.VMEM((tm, d), BF16)],
        compiler_params=_cp(("parallel", "arbitrary")),
        name="inproj",
    )(x, gain, w)


def _pack_kernel(wt_ref, o_ref, *, segs, tail):
    o_ref[:, tail:] = jnp.zeros((o_ref.shape[0], o_ref.shape[1] - tail), o_ref.dtype)
    for src, dst, size in segs:
        rows = _round_up(size, LANES)
        xt = wt_ref[src:src + rows, :].T
        if rows != size:
            xt = jnp.where(_iota((1, rows), 1) < size, xt, 0.0)
        o_ref[:, dst:dst + rows] = xt.astype(o_ref.dtype)


def _pack_weights(wt, segs, tail, n_pack):
    depth, n_in, d = wt.shape
    assert all(src + _round_up(size, LANES) <= n_in for src, _, size in segs)
    tm = _pick_tile(d, 256)
    return pl.pallas_call(
        functools.partial(_pack_kernel, segs=segs, tail=tail),
        out_shape=jax.ShapeDtypeStruct((depth, d, n_pack), BF16),
        grid=(depth, d // tm),
        in_specs=[pl.BlockSpec((None, n_in, tm), lambda l, i: (l, 0, i))],
        out_specs=pl.BlockSpec((None, tm, n_pack), lambda l, i: (l, i, 0)),
        compiler_params=_cp(("parallel", "parallel")),
        name="pack_w",
    )(wt)


def _rwkv_prep_math(z3, zw, p3, pw, vf, prm, has_vres):
    (mu3, muw, w0, w2, a0, a2, kkg, kag, v0, v1, v2) = prm
    rw = w0.shape[1]
    x3 = z3 + (p3 - z3) * mu3
    xw = zw + (pw - zw) * muw
    r = x3[:, :rw]
    k = x3[:, rw:2 * rw]
    v = x3[:, 2 * rw:]
    wd = xw[:, :LANES]
    ad = xw[:, LANES:]
    w_raw = w0 + _dot(jnp.tanh(wd).astype(BF16), w2)
    w_log = -_softplus(-w_raw) - DECAY_OFFSET
    lw = -jnp.exp(w_log)
    a = jax.nn.sigmoid(a0 + _dot(ad.astype(BF16), a2))
    if has_vres:
        lora = _dot(_dot(v.astype(BF16), v1).astype(BF16), v2)
        v = v + (vf - v) * jax.nn.sigmoid(v0 + lora)
    kk = k * kkg
    ss = _head_sums(kk * kk)
    kk = kk / jnp.maximum(jnp.sqrt(ss), 1e-12)
    k2 = k * (1.0 + (a - 1.0) * kag)
    return r, lw, k2, v, kk, kk * a


def _rwkv_prep_kernel(*refs, has_vres, chained):
    z3_ref, zw_ref, p3_ref, pw_ref = refs[:4]
    pos = 4
    vf = None
    if has_vres:
        vf = refs[pos][...]
        pos += 1
    prm = [r[...] for r in refs[pos:pos + 11]]
    outs = refs[pos + 11:]
    z3 = z3_ref[...]
    zw = zw_ref[...]
    if chained:
        first = pl.program_id(0) == 0
        row0 = _iota((z3.shape[0], 1), 0) == 0
        last3 = jnp.where(first, 0.0, p3_ref[7:8, :])
        lastw = jnp.where(first, 0.0, pw_ref[7:8, :])
        p3 = jnp.where(row0, last3, pltpu.roll(z3, 1, 0))
        pw = jnp.where(row0, lastw, pltpu.roll(zw, 1, 0))
    else:
        p3 = p3_ref[...]
        pw = pw_ref[...]
    res = _rwkv_prep_math(z3, zw, p3, pw, vf, prm, has_vres)
    for o_ref, val in zip(outs, res):
        o_ref[...] = val


def _rwkv_prep(z, prev3, prevw, vfirst, prm, *, rw, wa_col, chained):
    m = z.shape[0]
    tm = _pick_tile(m, 256)
    has_vres = vfirst is not None
    wa_blk = wa_col // (2 * LANES)
    if chained:
        sub = tm // 8
        p3_spec = pl.BlockSpec((8, 3 * rw), lambda i: (jnp.maximum(i * sub - 1, 0), 0))
        pw_spec = pl.BlockSpec((8, 2 * LANES), lambda i: (jnp.maximum(i * sub - 1, 0), wa_blk))
        prev_args = (z, z)
    else:
        p3_spec = pl.BlockSpec((tm, 3 * rw), lambda i: (i, 0))
        pw_spec = pl.BlockSpec((tm, 2 * LANES), lambda i: (i, 0))
        prev_args = (prev3, prevw)
    in_specs = [pl.BlockSpec((tm, 3 * rw), lambda i: (i, 0)),
                pl.BlockSpec((tm, 2 * LANES), lambda i: (i, wa_blk)),
                p3_spec, pw_spec]
    args = [z, z, *prev_args]
    if has_vres:
        in_specs.append(pl.BlockSpec((tm, rw), lambda i: (i, 0)))
        args.append(vfirst)
    for p in prm:
        in_specs.append(pl.BlockSpec(p.shape, lambda i: (0, 0)))
        args.append(p)
    out = pl.pallas_call(
        functools.partial(_rwkv_prep_kernel, has_vres=has_vres, chained=chained),
        out_shape=[jax.ShapeDtypeStruct((m, rw), F32)] * 6,
        grid=(m // tm,),
        in_specs=in_specs,
        out_specs=[pl.BlockSpec((tm, rw), lambda i: (i, 0))] * 6,
        compiler_params=_cp(("parallel",)),
        name="rwkv_prep",
    )(*args)
    return out


def _wkv_chunk_setup(r, lw, k, v, kk, b, rk, consts, C):
    tri_b, lane_lo, strict, incl, eye, level_masks, bd = consts
    C2 = 2 * C
    g3 = _dot(tri_b, jnp.concatenate(_split3(lw), axis=-1))
    yield
    g = g3[:, :LANES] + g3[:, LANES:2 * LANES] + g3[:, 2 * LANES:]
    g_last = g[C - 1:C, :]
    e_g = jnp.exp(g)
    e_ng = jnp.exp(-g)
    e_tail = jnp.exp(g_last - g)

    def stack(x):
        return jnp.concatenate([jnp.where(lane_lo, x, 0.0), jnp.where(lane_lo, 0.0, x)], axis=0).astype(BF16)

    left = jnp.concatenate([stack(-kk * jnp.exp(g - lw)), stack(r * e_g)], axis=0)
    right = jnp.concatenate([stack(b * e_ng), stack(k * e_ng)], axis=0)
    v2 = stack(v)
    sc = _dot_nt(left, right)
    yield
    m_ab = jnp.where(strict, sc[:C2, :C2], 0.0)
    m_ak = jnp.where(strict, sc[:C2, C2:], 0.0)
    p_rb = jnp.where(incl, sc[C2:, :C2], 0.0).astype(BF16)
    p_rk = jnp.where(incl, sc[C2:, C2:], 0.0)
    mv = _bdot(m_ak, v2)
    pv = _bdot(p_rk, v2)
    kv_t = _dot(v2.astype(F32).T.astype(BF16), stack(k * e_tail))
    bonus = _head_sums2(r * k * rk, bd) * v
    yield

    same, offs = level_masks
    mb = jnp.where(same, m_ab, 0.0)
    x = eye + mb
    pw = mb
    steps = 1
    while steps * 2 < min(8, C):
        pw = _bdot(pw, pw)
        yield
        x = x + _bdot(x, pw)
        yield
        steps *= 2
    for off_mask in offs:
        xb = x.astype(BF16)
        t = _dot(xb, jnp.where(off_mask, m_ab, 0.0).astype(BF16)).astype(BF16)
        yield
        x = x + _dot(t, xb)
        yield
    return dict(left=left, x=x.astype(BF16), mv=mv, pv=pv, kv_t=kv_t, p_rb=p_rb, bh=stack(b * e_tail),
                decay=jnp.exp(g_last), bonus=bonus)


def _wkv_chunk_apply(pre, s_prev, gate, lnw, lnb, bd, C):
    C2 = 2 * C
    hs = _dot_nt(pre["left"], s_prev.astype(BF16))
    yield
    u = _dot(pre["x"], (hs[:C2] + pre["mv"]).astype(BF16))
    yield
    o2 = hs[C2:] + pre["pv"] + _dot(pre["p_rb"], u.astype(BF16))
    s_new = s_prev * pre["decay"] + pre["kv_t"] + _dot(u.T.astype(BF16), pre["bh"])
    yield
    o = o2[:C] + o2[C:]
    mean = _head_sums2(o, bd) * (1.0 / HEAD_DIM)
    yield
    d = o - mean
    var = _head_sums2(d * d, bd) * (1.0 / HEAD_DIM)
    yield
    y = d * lax.rsqrt(var + LNX_EPS) * lnw + lnb + pre["bonus"]
    return (y * _silu(gate)), s_new


def _run_interleaved(gens):
    results = [None] * len(gens)
    live = list(range(len(gens)))
    while live:
        for i in list(live):
            try:
                next(gens[i])
            except StopIteration as stop:
                results[i] = stop.value
                live.remove(i)
    return results


def _wkv_kernel(r_ref, lw_ref, k_ref, v_ref, kk_ref, b_ref, g_ref, s0_ref, rk_ref, lnw_ref, lnb_ref,
                o_ref, sfin_ref, s_sc, *, chunk, nsub, valid_len, npair):
    c = pl.program_id(2)
    n_c = pl.num_programs(2)
    C = chunk
    C2 = 2 * C

    @pl.when(c == 0)
    def _():
        s_sc[...] = s0_ref[...]

    i2 = _iota((C2, C2), 0)
    j2 = _iota((C2, C2), 1)
    base = min(8, C)
    offs = []
    n = base
    while n < C:
        offs.append(((i2 // (2 * n)) == (j2 // (2 * n))) & ((i2 // n) != (j2 // n)))
        n *= 2
    bd = _head_block_ones(BF16)
    consts = ((_iota((C, C), 0) >= _iota((C, C), 1)).astype(BF16),
              _iota((1, LANES), 1) < HEAD_DIM,
              i2 > j2, i2 >= j2, (i2 == j2).astype(F32),
              ((i2 // base) == (j2 // base), offs), bd)
    lanes = [slice(p * LANES, (p + 1) * LANES) for p in range(npair)]
    rows = [slice(u * C, (u + 1) * C) for u in range(nsub)]

    setups = []
    for u, rs in enumerate(rows):
        ok = None
        if valid_len is not None:
            ok = ((c * nsub + u) * C + _iota((C, 1), 0)) < valid_len
        for sl in lanes:
            lw, k, kk, b = lw_ref[rs, sl], k_ref[rs, sl], kk_ref[rs, sl], b_ref[rs, sl]
            if ok is not None:
                lw = jnp.where(ok, lw, 0.0)
                k = jnp.where(ok, k, 0.0)
                kk = jnp.where(ok, kk, 0.0)
                b = jnp.where(ok, b, 0.0)
            setups.append(_wkv_chunk_setup(r_ref[rs, sl], lw, k, v_ref[rs, sl], kk, b, rk_ref[:, sl], consts, C))
    pre = _run_interleaved(setups)

    states = [s_sc[p] for p in range(npair)]
    outs = []
    for u, rs in enumerate(rows):
        res = _run_interleaved([_wkv_chunk_apply(pre[u * npair + p], states[p], g_ref[rs, sl], lnw_ref[:, sl],
                                                 lnb_ref[:, sl], bd, C) for p, sl in enumerate(lanes)])
        states = [s_new for _, s_new in res]
        outs.append([y for y, _ in res])
    for u, rs in enumerate(rows):
        for p, sl in enumerate(lanes):
            o_ref[rs, sl] = outs[u][p].astype(o_ref.dtype)
    for p in range(npair):
        s_sc[p] = states[p]

    @pl.when(c == n_c - 1)
    def _():
        sfin_ref[...] = s_sc[...]


def _wkv(r, lw, k, v, kk, b, gate, gate_col, s0, rk, lnw, lnb, *, chunk, chunks_per_step, valid_len,
         pairs_per_step):
    bsz, t, rw = r.shape
    pw = pairs_per_step
    wid = pw * LANES
    ngrp = rw // wid
    rows = chunk * chunks_per_step
    nchunk = t // rows
    gblk = gate_col // wid
    seq = pl.BlockSpec((None, rows, wid), lambda bi, g, c: (bi, c, g))
    par = pl.BlockSpec((1, wid), lambda bi, g, c: (0, g))
    st = pl.BlockSpec((None, pw, LANES, LANES), lambda bi, g, c: (bi, g, 0, 0))
    return pl.pallas_call(
        functools.partial(_wkv_kernel, chunk=chunk, nsub=chunks_per_step, valid_len=valid_len, npair=pw),
        out_shape=[jax.ShapeDtypeStruct((bsz, t, rw), BF16),
                   jax.ShapeDtypeStruct((bsz, rw // LANES, LANES, LANES), F32)],
        grid=(bsz, ngrp, nchunk),
        in_specs=[seq] * 6 + [pl.BlockSpec((None, rows, wid), lambda bi, g, c: (bi, c, gblk + g)),
                              st, par, par, par],
        out_specs=[seq, st],
        scratch_shapes=[pltpu.VMEM((pw, LANES, LANES), F32)],
        compiler_params=_cp(("parallel", "parallel", "arbitrary")),
        name="wkv",
    )(r, lw, k, v, kk, b, gate, s0, rk, lnw, lnb)


def _pack_state(s):
    bsz, h, n, _ = s.shape
    s = s.reshape(bsz, h // 2, 2, n, n)
    z = jnp.zeros_like(s[:, :, 0])
    top = jnp.concatenate([s[:, :, 0], z], axis=-1)
    bot = jnp.concatenate([z, s[:, :, 1]], axis=-1)
    return jnp.concatenate([top, bot], axis=-2)


def _unpack_state(sp):
    bsz, p = sp.shape[:2]
    n = HEAD_DIM
    s = jnp.stack([sp[:, :, :n, :n], sp[:, :, n:, n:]], axis=2)
    return s.reshape(bsz, 2 * p, n, n)


def _fox_prep_kernel(*refs, augment):
    q_ref, k_ref, f_ref, fb_ref, qg_ref, kg_ref = refs[:6]
    q = q_ref[...]
    k = k_ref[...]
    qn = q * lax.rsqrt(_head_sums(q * q) * (1.0 / HEAD_DIM) + NORM_EPS) * qg_ref[...]
    kn = k * lax.rsqrt(_head_sums(k * k) * (1.0 / HEAD_DIM) + NORM_EPS) * kg_ref[...]
    logf = -_softplus(-(f_ref[...] + fb_ref[...]))
    if not augment:
        qn_ref, kn_ref, lf_ref = refs[6:]
        qn_ref[...] = qn
        kn_ref[...] = kn
        lf_ref[...] = logf
        return
    v_ref, pl_ref, sq_ref, sk_ref, cq_ref, ck_ref, cv_ref, kn_ref, lf_ref, qa_ref, ka_ref, va_ref, carry = refs[6:]

    @pl.when(pl.program_id(0) == 0)
    def _():
        carry[...] = jnp.zeros_like(carry)

    tm = q.shape[0]
    tri = (_iota((tm, tm), 0) >= _iota((tm, tm), 1)).astype(F32)
    c = carry[...] + _dot(tri, logf, HI)
    carry[...] = c[tm - 1:tm, :]
    c3 = jnp.concatenate(_split3(c * LOG2E), axis=-1)
    place = pl_ref[...]

    def spread(x):
        xb = x.astype(BF16)
        return jnp.concatenate([_dot(xb[:, i * LANES:(i + 1) * LANES], place) for i in range(x.shape[1] // LANES)],
                               axis=-1)

    kn_ref[...] = kn
    lf_ref[...] = logf
    qa_ref[...] = (spread(qn * (LOG2E * HEAD_DIM ** -0.5)) + _dot(c3, sq_ref[...]) + cq_ref[...]).astype(BF16)
    ka_ref[...] = (spread(kn) + _dot(c3, sk_ref[...]) + ck_ref[...]).astype(BF16)
    va_ref[...] = (spread(v_ref[...]) + cv_ref[...]).astype(BF16)


def _fox_aug_consts(fw):
    nh = fw // HEAD_DIM
    rows = jnp.arange(LANES)
    cols = (rows // HEAD_DIM) * LANES + rows % HEAD_DIM
    place = jnp.zeros((LANES, 2 * LANES), F32).at[rows, cols].set(1.0).astype(BF16)
    h = jnp.arange(nh)
    sq = jnp.zeros((3 * LANES, nh * LANES), F32)
    sk = jnp.zeros((3 * LANES, nh * LANES), F32)
    cq = jnp.zeros((1, nh * LANES), F32)
    ck = jnp.zeros((1, nh * LANES), F32)
    cv = jnp.zeros((1, nh * LANES), F32).at[0, h * LANES + HEAD_DIM].set(1.0)
    for part in range(3):
        sq = sq.at[part * LANES + h, h * LANES + HEAD_DIM + part].set(1.0)
        sk = sk.at[part * LANES + h, h * LANES + HEAD_DIM + 3 + part].set(-1.0)
        cq = cq.at[0, h * LANES + HEAD_DIM + 3 + part].set(1.0)
        ck = ck.at[0, h * LANES + HEAD_DIM + part].set(1.0)
    return place, sq.astype(BF16), sk.astype(BF16), cq, ck, cv


def _fox_prep(z, fb, qg, kg, *, fw, q_col, k_col, v_col, f_col, aug, valid_rows=None):
    m = z.shape[0]
    tm = _pick_tile(m, 256)
    qb, kb, vblk, fblk = q_col // fw, k_col // fw, v_col // fw, f_col // LANES
    in_specs = [pl.BlockSpec((tm, fw), lambda i: (i, qb)),
                pl.BlockSpec((tm, fw), lambda i: (i, kb)),
                pl.BlockSpec((tm, LANES), lambda i: (i, fblk)),
                pl.BlockSpec((1, LANES), lambda i: (0, 0)),
                pl.BlockSpec((1, fw), lambda i: (0, 0)),
                pl.BlockSpec((1, fw), lambda i: (0, 0))]
    args = [z, z, z, fb, qg, kg]
    row = lambda w: pl.BlockSpec((tm, w), lambda i: (i, 0))
    if aug is None:
        return pl.pallas_call(
            functools.partial(_fox_prep_kernel, augment=False),
            out_shape=[jax.ShapeDtypeStruct((m, fw), F32), jax.ShapeDtypeStruct((m, fw), F32),
                       jax.ShapeDtypeStruct((m, LANES), F32)],
            grid=(m // tm,), in_specs=in_specs, out_specs=[row(fw), row(fw), row(LANES)],
            compiler_params=_cp(("parallel",)), name="fox_prep_s",
        )(*args)
    in_specs.append(pl.BlockSpec((tm, fw), lambda i: (i, vblk)))
    args.append(z)
    for a in aug:
        in_specs.append(pl.BlockSpec(a.shape, lambda i: (0, 0)))
        args.append(a)
    nh = fw // HEAD_DIM
    wide = jax.ShapeDtypeStruct((m, nh * LANES), BF16)
    return pl.pallas_call(
        functools.partial(_fox_prep_kernel, augment=True),
        out_shape=[jax.ShapeDtypeStruct((valid_rows or m, fw), F32), jax.ShapeDtypeStruct((m, LANES), F32),
                   wide, wide, wide],
        grid=(m // tm,), in_specs=in_specs,
        out_specs=[row(fw), row(LANES), row(nh * LANES), row(nh * LANES), row(nh * LANES)],
        scratch_shapes=[pltpu.VMEM((1, LANES), F32)],
        compiler_params=_cp(("arbitrary",)), name="fox_prep_p",
    )(*args)


def _flash_kernel(q_ref, k_ref, v_ref, g_ref, o_ref, m_sc, acc_sc, sa_sc, sb_sc, *, tq, tk):
    qi = pl.program_id(1)
    nsub = tq // tk
    m_sc[...] = jnp.full_like(m_sc, NEG_BIG)
    acc_sc[...] = jnp.zeros_like(acc_sc)
    heads = [slice(hh * LANES, (hh + 1) * LANES) for hh in range(2)]

    def key_off(g, j):
        return pl.multiple_of(g * tq + j * tk, tk)

    def scores(g, buf):
        for hh, sl in enumerate(heads):
            for j in range(nsub):
                buf[hh, j] = _dot_nt(q_ref[:, sl], k_ref[pl.ds(key_off(g, j), tk), sl])
                yield

    def update(hh, g, buf, masks):
        sl = heads[hh]
        ss = [buf[hh, j] for j in range(nsub)]
        if masks is not None:
            ss = [jnp.where(mk, s, NEG_BIG) for s, mk in zip(ss, masks)]
        m_old = m_sc[hh]
        mx = ss[0]
        for s in ss[1:]:
            mx = jnp.maximum(mx, s)
        m_new = jnp.maximum(m_old, jnp.broadcast_to(jnp.max(mx, axis=-1, keepdims=True), m_old.shape))
        yield
        m_rep = jnp.concatenate([m_new] * (tk // LANES), axis=-1)
        acc = acc_sc[hh] * jnp.exp2(m_old - m_new)
        for j, s in enumerate(ss):
            acc = acc + _dot(jnp.exp2(s - m_rep).astype(BF16), v_ref[pl.ds(key_off(g, j), tk), sl])
            yield
        m_sc[hh] = m_new
        acc_sc[hh] = acc

    def step(g, cur, nxt, masks):
        gens = [] if nxt is None else [scores(g + 1, nxt)]
        gens += [update(hh, g, cur, masks) for hh in range(2)]
        _run_interleaved(gens)

    rows = _iota((tq, tk), 0)
    cols = _iota((tq, tk), 1)
    causal = [rows >= cols + j * tk for j in range(nsub)]

    _run_interleaved([scores(0, sa_sc)])

    def body(i, carry):
        step(2 * i, sa_sc, sb_sc, None)
        step(2 * i + 1, sb_sc, sa_sc, None)
        return carry

    lax.fori_loop(0, qi // 2, body, 0)

    @pl.when(qi % 2 == 1)
    def _():
        step(qi - 1, sa_sc, sb_sc, None)
        step(qi, sb_sc, None, causal)

    @pl.when(qi % 2 == 0)
    def _():
        step(qi, sa_sc, None, causal)

    outs = []
    for hh in range(2):
        acc = acc_sc[hh]
        outs.append(acc / acc[:, HEAD_DIM:HEAD_DIM + 1])
    o = jnp.where(_iota((1, LANES), 1) < HEAD_DIM, outs[0], pltpu.roll(outs[1], HEAD_DIM, 1))
    o_ref[...] = (o * _silu(g_ref[...])).astype(o_ref.dtype)


def _flash(qa, ka, va, z, g_col):
    lp = qa.shape[0]
    npair = qa.shape[1] // (2 * LANES)
    tk = 256
    tq = _pick_tile(lp, 768, tk)
    gblk = g_col // LANES
    return pl.pallas_call(
        functools.partial(_flash_kernel, tq=tq, tk=tk),
        out_shape=jax.ShapeDtypeStruct((lp, npair * LANES), BF16),
        grid=(npair, lp // tq),
        in_specs=[pl.BlockSpec((tq, 2 * LANES), lambda p, i: (i, p)),
                  pl.BlockSpec((lp, 2 * LANES), lambda p, i: (0, p)),
                  pl.BlockSpec((lp, 2 * LANES), lambda p, i: (0, p)),
                  pl.BlockSpec((tq, LANES), lambda p, i: (i, gblk + p))],
        out_specs=pl.BlockSpec((tq, LANES), lambda p, i: (i, p)),
        scratch_shapes=[pltpu.VMEM((2, tq, LANES), F32), pltpu.VMEM((2, tq, LANES), F32),
                        pltpu.VMEM((2, tq // tk, tq, tk), F32), pltpu.VMEM((2, tq // tk, tq, tk), F32)],
        compiler_params=_cp(("parallel", "arbitrary")),
        name="fox_flash",
    )(qa, ka, va, z)


def _lfpool_kernel(x_ref, o_ref):
    n = x_ref.shape[1]
    jr = _iota((n, 2 * n), 0)
    jc = _iota((n, 2 * n), 1)
    tri2 = ((jr > jc) | (jc == n)).astype(BF16)
    hi, mid, lo = _split3(x_ref[...])
    o_ref[...] = _dot(hi, tri2) + _dot(mid, tri2) + _dot(lo, tri2)


def _lfpool(lft):
    rws, n = lft.shape
    tm = _pick_tile(rws, 2048)
    return pl.pallas_call(
        _lfpool_kernel,
        out_shape=jax.ShapeDtypeStruct((rws, 2 * n), F32),
        grid=(rws // tm,),
        in_specs=[pl.BlockSpec((tm, n), lambda i: (i, 0))],
        out_specs=pl.BlockSpec((tm, 2 * n), lambda i: (i, 0)),
        compiler_params=_cp(("parallel",)), name="fox_lfpool",
    )(lft)


def _decode_kernel(*refs, ts, nheads, pps):
    q_ref = refs[1]
    k_refs = refs[2:2 + pps]
    v_refs = refs[2 + pps:2 + 2 * pps]
    f_refs = refs[2 + 2 * pps:2 + 3 * pps]
    kn_ref, vn_ref, lfn_ref, g_ref, o_ref, m_sc, l_sc, acc_sc, r_sc = refs[2 + 3 * pps:]
    j = pl.program_id(1)
    nj = pl.num_programs(1)
    nrow = nheads * ts
    page = k_refs[0].shape[-1]
    fw = nheads * HEAD_DIM

    @pl.when(j == 0)
    def _():
        m_sc[...] = jnp.full_like(m_sc, NEG_BIG)
        l_sc[...] = jnp.zeros_like(l_sc)
        acc_sc[...] = jnp.zeros_like(acc_sc)
        r_sc[...] = jnp.zeros_like(r_sc)

    q = q_ref[...]

    def per_row(x):
        return jnp.concatenate([x] * ts, axis=0)

    def update(blocks):
        ss = [_dot(q, kt.astype(BF16)) + bias for kt, _, bias in blocks]
        m_run, l_run, acc = m_sc[...], l_sc[...], acc_sc[...]
        half = (len(blocks) + 1) // 2
        for lo in range(0, len(blocks), half):
            m_new = m_run
            for s in ss[lo:lo + half]:
                m_new = jnp.maximum(m_new, jnp.max(s, axis=-1, keepdims=True))
            alpha = jnp.exp(m_run - m_new)
            l_run = alpha * l_run
            acc = acc * alpha
            for s, (_, vt, _) in zip(ss[lo:lo + half], blocks[lo:lo + half]):
                p = jnp.exp(s - m_new)
                l_run = l_run + jnp.sum(p, axis=-1, keepdims=True)
                acc = acc + _dot_nt(p.astype(BF16), vt.astype(BF16))
            m_run = m_new
        l_sc[...] = l_run
        acc_sc[...] = acc
        m_sc[...] = m_run

    run = r_sc[...]
    blocks = []
    for k_ref, v_ref, f_ref in zip(k_refs, v_refs, f_refs):
        f = f_ref[...]
        blocks.append((k_ref[...].reshape(fw, page), v_ref[...].reshape(fw, page),
                       per_row(f[:, :page] + run)))
        run = run + f[:, page:page + 1]
    r_sc[...] = run
    update(blocks)

    @pl.when(j == nj - 1)
    def _():
        nn = kn_ref.shape[-1]
        upper = (_iota((nn, nn), 0) <= _iota((nn, nn), 1)).astype(F32)
        cum = per_row(_dot(lfn_ref[...], upper, HI))
        tq = _iota((nrow, nn), 0) // nheads
        tk = _iota((nrow, nn), 1)
        update([(kn_ref[...], vn_ref[...], jnp.where((tk <= tq) & (tk < ts), -cum, NEG_BIG))])
        out = acc_sc[...] / l_sc[...]
        own = (_iota((nrow, fw), 0) % nheads) == (_iota((nrow, fw), 1) // HEAD_DIM)
        sel = (_iota((8, nrow), 1) // nheads == _iota((8, nrow), 0)).astype(F32)
        o = _dot(sel, jnp.where(own, out, 0.0), HI)
        o_ref[...] = (o * _silu(g_ref[...])).astype(o_ref.dtype)


def _decode(page_table, qbd, kt_pool, vt_pool, lfo, layer, knew_t, vnew_t, lfn_t, gate, *, ts, nheads, pps):
    bsz, nrow, fw = qbd.shape
    npg = page_table.shape[1]
    page = kt_pool.shape[-1]
    nn = knew_t.shape[-1]

    def pool(shape, which):
        nd = len(shape)
        return pl.BlockSpec((None, None) + shape,
                            lambda b, j, pt: (layer, pt[b, npg - 1 - pps * j - which]) + (0,) * nd)

    per_b = lambda w0, w1: pl.BlockSpec((None, w0, w1), lambda b, j, pt: (b, 0, 0))
    kv = (nheads, HEAD_DIM, page)
    each = range(pps)
    grid_spec = pltpu.PrefetchScalarGridSpec(
        num_scalar_prefetch=1, grid=(bsz, npg // pps),
        in_specs=([per_b(nrow, fw)] + [pool(kv, w) for w in each] + [pool(kv, w) for w in each]
                  + [pool((nheads, 2 * page), w) for w in each]
                  + [per_b(fw, nn), per_b(fw, nn), per_b(nheads, nn), per_b(8, fw)]),
        out_specs=per_b(8, fw),
        scratch_shapes=[pltpu.VMEM((nrow, 1), F32), pltpu.VMEM((nrow, 1), F32), pltpu.VMEM((nrow, fw), F32),
                        pltpu.VMEM((nheads, 1), F32)])
    return pl.pallas_call(
        functools.partial(_decode_kernel, ts=ts, nheads=nheads, pps=pps),
        out_shape=jax.ShapeDtypeStruct((bsz, 8, fw), BF16),
        grid_spec=grid_spec,
        compiler_params=_cp(("parallel", "arbitrary")),
        name="fox_decode",
    )(page_table, qbd, *([kt_pool] * pps), *([vt_pool] * pps), *([lfo] * pps), knew_t, vnew_t, lfn_t, gate)


def _merge_kernel(x_ref, or_ref, of_ref, mr_ref, mf_ref, wpr_ref, wpf_ref, wo_ref, o_ref):
    a = _dot(or_ref[...], wpr_ref[...])
    b = _dot(of_ref[...], wpf_ref[...])
    mixed = jax.nn.sigmoid(mr_ref[...]) * a + jax.nn.sigmoid(mf_ref[...]) * b
    o_ref[...] = x_ref[...] + _dot(mixed.astype(BF16), wo_ref[...])


def _merge(x, o_r, o_f, z, mr_col, mf_col, wpr, wpf, wo):
    m, d = x.shape
    rw = o_r.shape[1]
    fw = o_f.shape[1]
    tm = _pick_tile(m, 256)
    whole = lambda a: pl.BlockSpec(a.shape, lambda i: (0, 0), pipeline_mode=pl.Buffered(1))
    return pl.pallas_call(
        _merge_kernel,
        out_shape=jax.ShapeDtypeStruct((m, d), F32),
        grid=(m // tm,),
        in_specs=[pl.BlockSpec((tm, d), lambda i: (i, 0)),
                  pl.BlockSpec((tm, rw), lambda i: (i, 0)),
                  pl.BlockSpec((tm, fw), lambda i: (i, 0)),
                  pl.BlockSpec((tm, d), lambda i: (i, mr_col // d)),
                  pl.BlockSpec((tm, d), lambda i: (i, mf_col // d)),
                  whole(wpr), whole(wpf), whole(wo)],
        out_specs=pl.BlockSpec((tm, d), lambda i: (i, 0)),
        compiler_params=_cp(("parallel",)),
        name="merge",
    )(x, o_r, o_f, z, z, wpr, wpf, wo)


def _pad_cols(a, width):
    return jnp.pad(a, [(0, 0)] * (a.ndim - 1) + [(0, width - a.shape[-1])])


def kernel(x_prompt, x_sample, cache_k, cache_v, cache_logf, state_wkv, state_shift, page_table, meta_tokens, norm_gain, w_in, r_mu, r_w0, r_w2, r_a0, r_a2, r_v0, r_v1, r_v2, r_kk, r_ka, r_rk, r_lnx_w, r_lnx_b, f_bias, f_qgain, f_kgain, w_proj_r, w_proj_f, w_out):
    bsz, seq, d = x_prompt.shape
    db, ts, _ = x_sample.shape
    depth = w_in.shape[0]
    rw = r_w0.shape[1]
    rh = r_rk.shape[1]
    fh = f_bias.shape[1]
    fw = fh * HEAD_DIM
    lora_w = r_w2.shape[1]
    lora_a = r_a2.shape[1]
    shift_w = 3 * rw + lora_w + lora_a
    page = cache_k.shape[2]
    npool = cache_k.shape[1]
    npg = page_table.shape[1]
    assert bsz == 1 and rw == rh * HEAD_DIM and fw == rw and d == 2 * rw
    assert lora_w <= LANES and lora_a <= LANES and fh <= LANES and ts * fh <= LANES
    pps = next(n for n in (8, 4, 2, 1) if npg % n == 0)

    s_gr = shift_w
    s_q = s_gr + rw
    s_k = s_q + fw
    s_v = s_k + fw
    s_f = s_v + fw
    s_gf = s_f + fh
    s_mr = s_gf + fw
    s_mf = s_mr + d
    c_gr = 3 * rw
    c_q = c_gr + rw
    c_k = c_q + fw
    c_v = c_k + fw
    c_gf = c_v + fw
    c_mr = c_gf + fw
    c_mf = c_mr + d
    c_wa = c_mf + d
    c_f = c_wa + 2 * LANES
    n_pack = _round_up(c_f + LANES, INPROJ_TN)

    segs = ((0, 0, 3 * rw), (s_gr, c_gr, rw), (s_q, c_q, fw), (s_k, c_k, fw), (s_v, c_v, fw), (s_gf, c_gf, fw),
            (s_mr, c_mr, d), (s_mf, c_mf, d), (3 * rw, c_wa, lora_w), (3 * rw + lora_w, c_wa + LANES, lora_a),
            (s_f, c_f, fh))
    w_pack = _pack_weights(jnp.swapaxes(w_in, 1, 2), segs, c_wa, n_pack)

    def pack_shift(a):
        return a[..., :3 * rw], jnp.concatenate([_pad_cols(a[..., 3 * rw:3 * rw + lora_w], LANES),
                                                 _pad_cols(a[..., 3 * rw + lora_w:], LANES)], axis=-1)

    def unpack_shift(zrow):
        return jnp.concatenate([zrow[..., :3 * rw], zrow[..., c_wa:c_wa + lora_w],
                                zrow[..., c_wa + LANES:c_wa + LANES + lora_a]], axis=-1)

    length = seq + N_META
    lp = _round_up(length, 768) if length > 768 else _round_up(length, 256)
    chunk = 64
    xp = jnp.concatenate([meta_tokens.astype(F32), x_prompt[0], jnp.zeros((lp - length, d), F32)], axis=0)
    xs = x_sample.reshape(db * ts, d)
    aug = _fox_aug_consts(fw)
    s0_p = jnp.zeros((1, rh // 2, LANES, LANES), F32)

    kt_pool = jnp.transpose(cache_k, (0, 1, 3, 4, 2))
    vt_pool = jnp.transpose(cache_v, (0, 1, 3, 4, 2))
    lft = jnp.swapaxes(cache_logf, 2, 3).reshape(depth * npool * fh, page)
    lfo = _lfpool(lft).reshape(depth, npool, fh, 2 * page)

    vf_p = None
    vf_s = None
    outs = {n: [] for n in ("kp", "vp", "lp", "sp", "hp", "ks", "vs", "ls", "ss", "hs")}
    for l in range(depth):
        gain = norm_gain[l][None, :]
        mu3, muw = pack_shift(r_mu[l][None, :])
        if l == 0:
            v0 = jnp.zeros((1, rw), F32)
            v1 = jnp.zeros((rw, LANES), BF16)
            v2 = jnp.zeros((LANES, rw), BF16)
        else:
            v0 = r_v0[l - 1][None, :]
            v1 = _pad_cols(r_v1[l - 1], LANES).astype(BF16)
            v2 = jnp.pad(r_v2[l - 1], ((0, LANES - r_v2.shape[1]), (0, 0))).astype(BF16)
        prm = (mu3, muw, r_w0[l][None, :],
               jnp.pad(r_w2[l], ((0, LANES - lora_w), (0, 0))).astype(BF16),
               r_a0[l][None, :],
               jnp.pad(r_a2[l], ((0, LANES - lora_a), (0, 0))).astype(BF16),
               r_kk[l][None, :], r_ka[l][None, :], v0, v1, v2)
        rk = r_rk[l].reshape(1, rw)
        lnw = r_lnx_w[l][None, :]
        lnb = r_lnx_b[l][None, :]
        fb = _pad_cols(f_bias[l][None, :], LANES)
        qg = jnp.tile(f_qgain[l], fh)[None, :]
        kg = jnp.tile(f_kgain[l], fh)[None, :]
        wpr = w_proj_r[l].astype(BF16)
        wpf = w_proj_f[l].astype(BF16)
        wo = w_out[l].astype(BF16)

        z = _inproj(xp, gain, w_pack, l)
        r, lw, k2, v, kk, b = _rwkv_prep(z, None, None, vf_p, prm, rw=rw, wa_col=c_wa, chained=True)
        if l == 0:
            vf_p = v
        o_r, s_fin = _wkv(r[None], lw[None], k2[None], v[None], kk[None], b[None], z[None], c_gr,
                          s0_p, rk, lnw, lnb, chunk=chunk, chunks_per_step=2, valid_len=length,
                          pairs_per_step=rw // LANES)
        kn, lf, qa, ka, va = _fox_prep(z, fb, qg, kg, fw=fw, q_col=c_q, k_col=c_k, v_col=c_v, f_col=c_f, aug=aug,
                                       valid_rows=length)
        o_f = _flash(qa, ka, va, z, c_gf)
        xp = _merge(xp, o_r[0], o_f, z, c_mr, c_mf, wpr, wpf, wo)
        outs["kp"].append(kn.reshape(1, length, fh, HEAD_DIM))
        outs["vp"].append(z[:length, c_v:c_v + fw].reshape(1, length, fh, HEAD_DIM))
        outs["lp"].append(lf[:length, :fh][None])
        outs["sp"].append(_unpack_state(s_fin))
        outs["hp"].append(unpack_shift(z[length - 1:length]))

        zs = _inproj(xs, gain, w_pack, l)
        zs3 = zs.reshape(db, ts, n_pack)
        sh3, shw = pack_shift(state_shift[l])
        prev3 = jnp.concatenate([sh3[:, None], zs3[:, :-1, :3 * rw]], axis=1).reshape(db * ts, 3 * rw)
        prevw = jnp.concatenate([shw[:, None], zs3[:, :-1, c_wa:c_wa + 2 * LANES]], axis=1).reshape(db * ts, 2 * LANES)
        res = _rwkv_prep(zs, prev3, prevw, vf_s, prm, rw=rw, wa_col=c_wa, chained=False)
        if l == 0:
            vf_s = res[3]
        pad_t = lambda a: jnp.pad(a.reshape(db, ts, -1), ((0, 0), (0, 8 - ts), (0, 0)))
        r, lw, k2, v, kk, b = (pad_t(a) for a in res)
        gate_r = pad_t(zs[:, c_gr:c_gr + rw])
        o_r, s_fin = _wkv(r, lw, k2, v, kk, b, gate_r, 0, _pack_state(state_wkv[l]), rk, lnw, lnb,
                          chunk=8, chunks_per_step=1, valid_len=None, pairs_per_step=rw // LANES)
        o_r = o_r[:, :ts].reshape(db * ts, rw)
        qn, kn, lf = _fox_prep(zs, fb, qg, kg, fw=fw, q_col=c_q, k_col=c_k, v_col=c_v, f_col=c_f, aug=None)
        vn = zs[:, c_v:c_v + fw]
        q4 = qn.reshape(db, ts, fh, HEAD_DIM) * HEAD_DIM ** -0.5
        qbd = jnp.einsum("bthd,hg->bthgd", q4, jnp.eye(fh, dtype=F32)).reshape(db, ts * fh, fw).astype(BF16)
        pad_k = lambda a: _pad_cols(jnp.swapaxes(a.reshape(db, ts, -1), 1, 2), page)
        gate_f = pad_t(zs[:, c_gf:c_gf + fw])
        o_f = _decode(page_table, qbd, kt_pool, vt_pool, lfo, l, pad_k(kn), pad_k(vn), pad_k(lf[:, :fh]), gate_f,
                      ts=ts, nheads=fh, pps=pps)
        o_f = o_f[:, :ts].reshape(db * ts, fw)
        xs = _merge(xs, o_r, o_f, zs, c_mr, c_mf, wpr, wpf, wo)
        outs["ks"].append(kn.reshape(db, ts, fh, HEAD_DIM))
        outs["vs"].append(vn.reshape(db, ts, fh, HEAD_DIM))
        outs["ls"].append(lf[:, :fh].reshape(db, ts, fh))
        outs["ss"].append(_unpack_state(s_fin))
        outs["hs"].append(unpack_shift(zs3[:, -1]))

    y_prompt = xp[N_META:length][None]
    y_sample = xs.reshape(db, ts, d)
    st = lambda n: jnp.stack(outs[n])
    return (y_prompt, y_sample, st("kp"), st("vp"), st("lp"), st("sp"), st("hp"),
            st("ks"), st("vs"), st("ls"), st("ss"), st("hs"))
```

```python
import functools
import math

import jax
import jax.numpy as jnp
from jax import lax
from jax.experimental import pallas as pl
from jax.experimental.pallas import tpu as pltpu

F32 = jnp.float32
BF16 = jnp.bfloat16
HI = lax.Precision.HIGHEST

LANES = 128
HEAD_DIM = 64
N_META = 16
NORM_EPS = 1e-6
LNX_EPS = 1e-5 * HEAD_DIM
DECAY_OFFSET = 0.5
NEG_BIG = -1e30
LOG2E = math.log2(math.e)
VMEM_LIMIT = 56 * 1024 * 1024
INPROJ_TN = 1280


def _cp(sem, vmem=VMEM_LIMIT):
    return pltpu.CompilerParams(dimension_semantics=sem, vmem_limit_bytes=vmem)


def _round_up(x, m):
    return (x + m - 1) // m * m


def _pick_tile(n, cap, mult=128):
    if n <= cap:
        return n
    best = mult
    t = mult
    while t <= cap:
        if n % t == 0:
            best = t
        t += mult
    return best


def _dot(a, b, precision=None):
    return jnp.dot(a, b, preferred_element_type=F32, precision=precision)


def _dot_nt(a, b, precision=None):
    return lax.dot_general(a, b, (((1,), (1,)), ((), ())), preferred_element_type=F32, precision=precision)


def _bdot(a, b):
    return _dot(a.astype(BF16), b.astype(BF16))


def _iota(shape, dim):
    return lax.broadcasted_iota(jnp.int32, shape, dim)


def _split2(x):
    hi = x.astype(BF16)
    return hi, (x - hi.astype(F32)).astype(BF16)


def _split3(x):
    hi = x.astype(BF16)
    r1 = x - hi.astype(F32)
    mid = r1.astype(BF16)
    return hi, mid, (r1 - mid.astype(F32)).astype(BF16)


def _head_block_ones(dtype=F32):
    return (_iota((LANES, LANES), 0) // HEAD_DIM == _iota((LANES, LANES), 1) // HEAD_DIM).astype(dtype)


def _head_sums(x):
    bd = _head_block_ones(BF16)
    parts = [_head_sums2(x[:, i * LANES:(i + 1) * LANES], bd) for i in range(x.shape[1] // LANES)]
    return parts[0] if len(parts) == 1 else jnp.concatenate(parts, axis=-1)


def _head_sums2(x, bd):
    hi, lo = _split2(x)
    return _dot(hi, bd) + _dot(lo, bd)


def _softplus(x):
    return jnp.maximum(x, 0.0) + jnp.log(1.0 + jnp.exp(-jnp.abs(x)))


def _silu(x):
    return x * jax.nn.sigmoid(x)


def _inproj_kernel(x_ref, g_ref, w_ref, o_ref, xn_ref):
    @pl.when(pl.program_id(1) == 0)
    def _():
        x = x_ref[...]
        ms = jnp.mean(x * x, axis=-1, keepdims=True)
        xn_ref[...] = (x * lax.rsqrt(ms + NORM_EPS) * g_ref[...]).astype(BF16)

    o_ref[...] = _dot(xn_ref[...], w_ref[...])


def _inproj(x, gain, w, layer):
    m, d = x.shape
    n = w.shape[2]
    tm = _pick_tile(m, 768)
    tn = _pick_tile(n, INPROJ_TN)
    return pl.pallas_call(
        _inproj_kernel,
        out_shape=jax.ShapeDtypeStruct((m, n), F32),
        grid=(m // tm, n // tn),
        in_specs=[pl.BlockSpec((tm, d), lambda i, j: (i, 0)),
                  pl.BlockSpec((1, d), lambda i, j: (0, 0)),
                  pl.BlockSpec((None, d, tn), lambda i, j: (layer, 0, j))],
        out_specs=pl.BlockSpec((tm, tn), lambda i, j: (i, j)),
        scratch_shapes=[pltpu.VMEM((tm, d), BF16)],
        compiler_params=_cp(("parallel", "arbitrary")),
        name="inproj",
    )(x, gain, w)


def _pack_kernel(wt_ref, o_ref, *, segs, tail):
    o_ref[:, tail:] = jnp.zeros((o_ref.shape[0], o_ref.shape[1] - tail), o_ref.dtype)
    for src, dst, size in segs:
        rows = _round_up(size, LANES)
        xt = wt_ref[src:src + rows, :].T
        if rows != size:
            xt = jnp.where(_iota((1, rows), 1) < size, xt, 0.0)
        o_ref[:, dst:dst + rows] = xt.astype(o_ref.dtype)


def _pack_weights(wt, segs, tail, n_pack):
    depth, n_in, d = wt.shape
    assert all(src + _round_up(size, LANES) <= n_in for src, _, size in segs)
    tm = _pick_tile(d, 256)
    return pl.pallas_call(
        functools.partial(_pack_kernel, segs=segs, tail=tail),
        out_shape=jax.ShapeDtypeStruct((depth, d, n_pack), BF16),
        grid=(depth, d // tm),
        in_specs=[pl.BlockSpec((None, n_in, tm), lambda l, i: (l, 0, i))],
        out_specs=pl.BlockSpec((None, tm, n_pack), lambda l, i: (l, i, 0)),
        compiler_params=_cp(("parallel", "parallel")),
        name="pack_w",
    )(wt)


def _rwkv_prep_math(z3, zw, p3, pw, vf, prm, has_vres):
    (mu3, muw, w0, w2, a0, a2, kkg, kag, v0, v1, v2) = prm
    rw = w0.shape[1]
    x3 = z3 + (p3 - z3) * mu3
    xw = zw + (pw - zw) * muw
    r = x3[:, :rw]
    k = x3[:, rw:2 * rw]
    v = x3[:, 2 * rw:]
    wd = xw[:, :LANES]
    ad = xw[:, LANES:]
    w_raw = w0 + _dot(jnp.tanh(wd).astype(BF16), w2)
    w_log = -_softplus(-w_raw) - DECAY_OFFSET
    lw = -jnp.exp(w_log)
    a = jax.nn.sigmoid(a0 + _dot(ad.astype(BF16), a2))
    if has_vres:
        lora = _dot(_dot(v.astype(BF16), v1).astype(BF16), v2)
        v = v + (vf - v) * jax.nn.sigmoid(v0 + lora)
    kk = k * kkg
    ss = _head_sums(kk * kk)
    kk = kk * lax.rsqrt(jnp.maximum(ss, 1e-24))
    k2 = k * (1.0 + (a - 1.0) * kag)
    return r, lw, k2, v, kk, kk * a


def _rwkv_prep_kernel(*refs, has_vres, chained):
    z3_ref, zw_ref, p3_ref, pw_ref = refs[:4]
    pos = 4
    vf = None
    if has_vres:
        vf = refs[pos][...]
        pos += 1
    prm = [r[...] for r in refs[pos:pos + 11]]
    outs = refs[pos + 11:]
    z3 = z3_ref[...]
    zw = zw_ref[...]
    if chained:
        first = pl.program_id(0) == 0
        row0 = _iota((z3.shape[0], 1), 0) == 0
        last3 = jnp.where(first, 0.0, p3_ref[7:8, :])
        lastw = jnp.where(first, 0.0, pw_ref[7:8, :])
        p3 = jnp.where(row0, last3, pltpu.roll(z3, 1, 0))
        pw = jnp.where(row0, lastw, pltpu.roll(zw, 1, 0))
    else:
        p3 = p3_ref[...]
        pw = pw_ref[...]
    res = _rwkv_prep_math(z3, zw, p3, pw, vf, prm, has_vres)
    for o_ref, val in zip(outs, res):
        o_ref[...] = val


def _rwkv_prep(z, prev3, prevw, vfirst, prm, *, rw, wa_col, chained):
    m = z.shape[0]
    tm = _pick_tile(m, 256)
    has_vres = vfirst is not None
    wa_blk = wa_col // (2 * LANES)
    if chained:
        sub = tm // 8
        p3_spec = pl.BlockSpec((8, 3 * rw), lambda i: (jnp.maximum(i * sub - 1, 0), 0))
        pw_spec = pl.BlockSpec((8, 2 * LANES), lambda i: (jnp.maximum(i * sub - 1, 0), wa_blk))
        prev_args = (z, z)
    else:
        p3_spec = pl.BlockSpec((tm, 3 * rw), lambda i: (i, 0))
        pw_spec = pl.BlockSpec((tm, 2 * LANES), lambda i: (i, 0))
        prev_args = (prev3, prevw)
    in_specs = [pl.BlockSpec((tm, 3 * rw), lambda i: (i, 0)),
                pl.BlockSpec((tm, 2 * LANES), lambda i: (i, wa_blk)),
                p3_spec, pw_spec]
    args = [z, z, *prev_args]
    if has_vres:
        in_specs.append(pl.BlockSpec((tm, rw), lambda i: (i, 0)))
        args.append(vfirst)
    for p in prm:
        in_specs.append(pl.BlockSpec(p.shape, lambda i: (0, 0)))
        args.append(p)
    out = pl.pallas_call(
        functools.partial(_rwkv_prep_kernel, has_vres=has_vres, chained=chained),
        out_shape=[jax.ShapeDtypeStruct((m, rw), F32)] * 6,
        grid=(m // tm,),
        in_specs=in_specs,
        out_specs=[pl.BlockSpec((tm, rw), lambda i: (i, 0))] * 6,
        compiler_params=_cp(("parallel",)),
        name="rwkv_prep",
    )(*args)
    return out


def _wkv_chunk_setup(r, lw, k, v, kk, b, rk, consts, C):
    tri_b, lane_lo, strict, incl, eye, level_masks, bd = consts
    C2 = 2 * C
    g3 = _dot(tri_b, jnp.concatenate(_split3(lw), axis=-1))
    yield
    g = g3[:, :LANES] + g3[:, LANES:2 * LANES] + g3[:, 2 * LANES:]
    g_last = g[C - 1:C, :]
    e_g = jnp.exp(g)
    e_ng = jnp.exp(-g)
    e_tail = jnp.exp(g_last - g)

    def stack(x):
        return jnp.concatenate([jnp.where(lane_lo, x, 0.0), jnp.where(lane_lo, 0.0, x)], axis=0).astype(BF16)

    left = jnp.concatenate([stack(-kk * jnp.exp(g - lw)), stack(r * e_g)], axis=0)
    right = jnp.concatenate([stack(b * e_ng), stack(k * e_ng)], axis=0)
    v2 = stack(v)
    sc = _dot_nt(left, right)
    yield
    m_ab = jnp.where(strict, sc[:C2, :C2], 0.0)
    m_ak = jnp.where(strict, sc[:C2, C2:], 0.0)
    p_rb = jnp.where(incl, sc[C2:, :C2], 0.0).astype(BF16)
    p_rk = jnp.where(incl, sc[C2:, C2:], 0.0)
    mv = _bdot(m_ak, v2)
    pv = _bdot(p_rk, v2)
    kv_t = _dot(v2.astype(F32).T.astype(BF16), stack(k * e_tail))
    bonus = _head_sums2(r * k * rk, bd) * v
    yield

    same, offs = level_masks
    mb = jnp.where(same, m_ab, 0.0)
    x = eye + mb
    pw = mb
    steps = 1
    while steps * 2 < min(8, C):
        pw = _bdot(pw, pw)
        yield
        x = x + _bdot(x, pw)
        yield
        steps *= 2
    for off_mask in offs:
        xb = x.astype(BF16)
        t = _dot(xb, jnp.where(off_mask, m_ab, 0.0).astype(BF16)).astype(BF16)
        yield
        x = x + _dot(t, xb)
        yield
    return dict(left=left, x=x.astype(BF16), mv=mv, pv=pv, kv_t=kv_t, p_rb=p_rb, bh=stack(b * e_tail),
                decay=jnp.exp(g_last), bonus=bonus)


def _wkv_chunk_apply(pre, s_prev, gate, lnw, lnb, bd, C):
    C2 = 2 * C
    hs = _dot_nt(pre["left"], s_prev.astype(BF16))
    yield
    u = _dot(pre["x"], (hs[:C2] + pre["mv"]).astype(BF16))
    yield
    o2 = hs[C2:] + pre["pv"] + _dot(pre["p_rb"], u.astype(BF16))
    s_new = s_prev * pre["decay"] + pre["kv_t"] + _dot(u.T.astype(BF16), pre["bh"])
    yield
    o = o2[:C] + o2[C:]
    mean = _head_sums2(o, bd) * (1.0 / HEAD_DIM)
    yield
    d = o - mean
    var = _head_sums2(d * d, bd) * (1.0 / HEAD_DIM)
    yield
    y = d * lax.rsqrt(var + LNX_EPS) * lnw + lnb + pre["bonus"]
    return (y * _silu(gate)), s_new


def _run_interleaved(gens):
    results = [None] * len(gens)
    live = list(range(len(gens)))
    while live:
        for i in list(live):
            try:
                next(gens[i])
            except StopIteration as stop:
                results[i] = stop.value
                live.remove(i)
    return results


def _wkv_steps(src, g_ref, s0_ref, rk_ref, lnw_ref, lnb_ref, o_ref, sfin_ref, s_sc, *, chunk, nsub, valid_len, npair):
    r_src, lw_src, k_src, v_src, kk_src, b_src = src
    c = pl.program_id(2)
    n_c = pl.num_programs(2)
    C = chunk
    C2 = 2 * C

    @pl.when(c == 0)
    def _():
        s_sc[...] = s0_ref[...]

    i2 = _iota((C2, C2), 0)
    j2 = _iota((C2, C2), 1)
    base = min(8, C)
    offs = []
    n = base
    while n < C:
        offs.append(((i2 // (2 * n)) == (j2 // (2 * n))) & ((i2 // n) != (j2 // n)))
        n *= 2
    bd = _head_block_ones(BF16)
    consts = ((_iota((C, C), 0) >= _iota((C, C), 1)).astype(BF16),
              _iota((1, LANES), 1) < HEAD_DIM,
              i2 > j2, i2 >= j2, (i2 == j2).astype(F32),
              ((i2 // base) == (j2 // base), offs), bd)
    lanes = [slice(p * LANES, (p + 1) * LANES) for p in range(npair)]
    rows = [slice(u * C, (u + 1) * C) for u in range(nsub)]

    setups = []
    for u, rs in enumerate(rows):
        ok = None
        if valid_len is not None:
            ok = ((c * nsub + u) * C + _iota((C, 1), 0)) < valid_len
        for sl in lanes:
            lw, k, kk, b = lw_src[rs, sl], k_src[rs, sl], kk_src[rs, sl], b_src[rs, sl]
            if ok is not None:
                lw = jnp.where(ok, lw, 0.0)
                k = jnp.where(ok, k, 0.0)
                kk = jnp.where(ok, kk, 0.0)
                b = jnp.where(ok, b, 0.0)
            setups.append(_wkv_chunk_setup(r_src[rs, sl], lw, k, v_src[rs, sl], kk, b, rk_ref[:, sl], consts, C))
    pre = _run_interleaved(setups)

    states = [s_sc[p] for p in range(npair)]
    outs = []
    for u, rs in enumerate(rows):
        res = _run_interleaved([_wkv_chunk_apply(pre[u * npair + p], states[p], g_ref[rs, sl], lnw_ref[:, sl],
                                                 lnb_ref[:, sl], bd, C) for p, sl in enumerate(lanes)])
        states = [s_new for _, s_new in res]
        outs.append([y for y, _ in res])
    for u, rs in enumerate(rows):
        for p, sl in enumerate(lanes):
            o_ref[rs, sl] = outs[u][p].astype(o_ref.dtype)
    for p in range(npair):
        s_sc[p] = states[p]

    @pl.when(c == n_c - 1)
    def _():
        sfin_ref[...] = s_sc[...]


def _wkv_kernel(r_ref, lw_ref, k_ref, v_ref, kk_ref, b_ref, g_ref, s0_ref, rk_ref, lnw_ref, lnb_ref,
                o_ref, sfin_ref, s_sc, **kw):
    _wkv_steps((r_ref, lw_ref, k_ref, v_ref, kk_ref, b_ref), g_ref, s0_ref, rk_ref, lnw_ref, lnb_ref,
               o_ref, sfin_ref, s_sc, **kw)


def _wkv_fused_kernel(*refs, has_vres, emit_v, **kw):
    z3_ref, zw_ref, p3_ref, pw_ref = refs[:4]
    pos = 4
    vf = None
    if has_vres:
        vf = refs[pos][...]
        pos += 1
    prm = [r[...] for r in refs[pos:pos + 11]]
    g_ref, s0_ref, rk_ref, lnw_ref, lnb_ref = refs[pos + 11:pos + 16]
    outs = refs[pos + 16:]
    o_ref = outs[0]
    sfin_ref, s_sc = outs[-2], outs[-1]
    z3 = z3_ref[...]
    zw = zw_ref[...]
    first = pl.program_id(2) == 0
    row0 = _iota((z3.shape[0], 1), 0) == 0
    p3 = jnp.where(row0, jnp.where(first, 0.0, p3_ref[7:8, :]), pltpu.roll(z3, 1, 0))
    pw = jnp.where(row0, jnp.where(first, 0.0, pw_ref[7:8, :]), pltpu.roll(zw, 1, 0))
    src = _rwkv_prep_math(z3, zw, p3, pw, vf, prm, has_vres)
    if emit_v:
        outs[1][...] = src[3]
    _wkv_steps(src, g_ref, s0_ref, rk_ref, lnw_ref, lnb_ref, o_ref, sfin_ref, s_sc, **kw)


def _wkv_fused(z, vfirst, prm, s0, rk, lnw, lnb, *, rw, wa_col, gate_col, chunk, chunks_per_step, valid_len, emit_v):
    t = z.shape[0]
    rows = chunk * chunks_per_step
    npair = rw // LANES
    has_vres = vfirst is not None
    sub = rows // 8
    wa_blk = wa_col // (2 * LANES)
    im = lambda f: (lambda bi, g, c: f(c))
    in_specs = [pl.BlockSpec((rows, 3 * rw), im(lambda c: (c, 0))),
                pl.BlockSpec((rows, 2 * LANES), im(lambda c: (c, wa_blk))),
                pl.BlockSpec((8, 3 * rw), im(lambda c: (jnp.maximum(c * sub - 1, 0), 0))),
                pl.BlockSpec((8, 2 * LANES), im(lambda c: (jnp.maximum(c * sub - 1, 0), wa_blk)))]
    args = [z, z, z, z]
    if has_vres:
        in_specs.append(pl.BlockSpec((rows, rw), im(lambda c: (c, 0))))
        args.append(vfirst)
    for p in prm:
        in_specs.append(pl.BlockSpec(p.shape, im(lambda c: (0, 0))))
        args.append(p)
    par = pl.BlockSpec((1, rw), im(lambda c: (0, 0)))
    st = pl.BlockSpec((None, npair, LANES, LANES), im(lambda c: (0, 0, 0, 0)))
    in_specs += [pl.BlockSpec((rows, rw), im(lambda c: (c, gate_col // rw))), st, par, par, par]
    args += [z, s0, rk, lnw, lnb]
    seq = pl.BlockSpec((rows, rw), im(lambda c: (c, 0)))
    out_shape = [jax.ShapeDtypeStruct((t, rw), BF16)]
    out_specs = [seq]
    if emit_v:
        out_shape.append(jax.ShapeDtypeStruct((t, rw), F32))
        out_specs.append(seq)
    out_shape.append(jax.ShapeDtypeStruct((1, npair, LANES, LANES), F32))
    out_specs.append(st)
    return pl.pallas_call(
        functools.partial(_wkv_fused_kernel, has_vres=has_vres, emit_v=emit_v, chunk=chunk, nsub=chunks_per_step,
                          valid_len=valid_len, npair=npair),
        out_shape=out_shape,
        grid=(1, 1, t // rows),
        in_specs=in_specs,
        out_specs=out_specs,
        scratch_shapes=[pltpu.VMEM((npair, LANES, LANES), F32)],
        compiler_params=_cp(("parallel", "parallel", "arbitrary")),
        name="wkv_fused",
    )(*args)


def _wkv(r, lw, k, v, kk, b, gate, gate_col, s0, rk, lnw, lnb, *, chunk, chunks_per_step, valid_len,
         pairs_per_step):
    bsz, t, rw = r.shape
    pw = pairs_per_step
    wid = pw * LANES
    ngrp = rw // wid
    rows = chunk * chunks_per_step
    nchunk = t // rows
    gblk = gate_col // wid
    seq = pl.BlockSpec((None, rows, wid), lambda bi, g, c: (bi, c, g))
    par = pl.BlockSpec((1, wid), lambda bi, g, c: (0, g))
    st = pl.BlockSpec((None, pw, LANES, LANES), lambda bi, g, c: (bi, g, 0, 0))
    return pl.pallas_call(
        functools.partial(_wkv_kernel, chunk=chunk, nsub=chunks_per_step, valid_len=valid_len, npair=pw),
        out_shape=[jax.ShapeDtypeStruct((bsz, t, rw), BF16),
                   jax.ShapeDtypeStruct((bsz, rw // LANES, LANES, LANES), F32)],
        grid=(bsz, ngrp, nchunk),
        in_specs=[seq] * 6 + [pl.BlockSpec((None, rows, wid), lambda bi, g, c: (bi, c, gblk + g)),
                              st, par, par, par],
        out_specs=[seq, st],
        scratch_shapes=[pltpu.VMEM((pw, LANES, LANES), F32)],
        compiler_params=_cp(("parallel", "parallel", "arbitrary")),
        name="wkv",
    )(r, lw, k, v, kk, b, gate, s0, rk, lnw, lnb)


def _pack_state(s):
    bsz, h, n, _ = s.shape
    s = s.reshape(bsz, h // 2, 2, n, n)
    z = jnp.zeros_like(s[:, :, 0])
    top = jnp.concatenate([s[:, :, 0], z], axis=-1)
    bot = jnp.concatenate([z, s[:, :, 1]], axis=-1)
    return jnp.concatenate([top, bot], axis=-2)


def _unpack_state(sp):
    bsz, p = sp.shape[:2]
    n = HEAD_DIM
    s = jnp.stack([sp[:, :, :n, :n], sp[:, :, n:, n:]], axis=2)
    return s.reshape(bsz, 2 * p, n, n)


def _fox_prep_kernel(*refs, augment):
    q_ref, k_ref, f_ref, fb_ref, qg_ref, kg_ref = refs[:6]
    q = q_ref[...]
    k = k_ref[...]
    qn = q * lax.rsqrt(_head_sums(q * q) * (1.0 / HEAD_DIM) + NORM_EPS) * qg_ref[...]
    kn = k * lax.rsqrt(_head_sums(k * k) * (1.0 / HEAD_DIM) + NORM_EPS) * kg_ref[...]
    logf = -_softplus(-(f_ref[...] + fb_ref[...]))
    if not augment:
        qn_ref, kn_ref, lf_ref = refs[6:]
        qn_ref[...] = qn
        kn_ref[...] = kn
        lf_ref[...] = logf
        return
    v_ref, pl_ref, sq_ref, sk_ref, cq_ref, ck_ref, cv_ref, kn_ref, lf_ref, qa_ref, ka_ref, va_ref, carry = refs[6:]

    @pl.when(pl.program_id(0) == 0)
    def _():
        carry[...] = jnp.zeros_like(carry)

    tm = q.shape[0]
    tri = (_iota((tm, tm), 0) >= _iota((tm, tm), 1)).astype(F32)
    c = carry[...] + _dot(tri, logf, HI)
    carry[...] = c[tm - 1:tm, :]
    c3 = jnp.concatenate(_split3(c * LOG2E), axis=-1)
    place = pl_ref[...]

    def spread(x):
        xb = x.astype(BF16)
        return jnp.concatenate([_dot(xb[:, i * LANES:(i + 1) * LANES], place) for i in range(x.shape[1] // LANES)],
                               axis=-1)

    kn_ref[...] = kn
    lf_ref[...] = logf
    qa_ref[...] = (spread(qn * (LOG2E * HEAD_DIM ** -0.5)) + _dot(c3, sq_ref[...]) + cq_ref[...]).astype(BF16)
    ka_ref[...] = (spread(kn) + _dot(c3, sk_ref[...]) + ck_ref[...]).astype(BF16)
    va_ref[...] = (spread(v_ref[...]) + cv_ref[...]).astype(BF16)


def _fox_aug_consts(fw):
    nh = fw // HEAD_DIM
    rows = jnp.arange(LANES)
    cols = (rows // HEAD_DIM) * LANES + rows % HEAD_DIM
    place = jnp.zeros((LANES, 2 * LANES), F32).at[rows, cols].set(1.0).astype(BF16)
    h = jnp.arange(nh)
    sq = jnp.zeros((3 * LANES, nh * LANES), F32)
    sk = jnp.zeros((3 * LANES, nh * LANES), F32)
    cq = jnp.zeros((1, nh * LANES), F32)
    ck = jnp.zeros((1, nh * LANES), F32)
    cv = jnp.zeros((1, nh * LANES), F32).at[0, h * LANES + HEAD_DIM].set(1.0)
    for part in range(3):
        sq = sq.at[part * LANES + h, h * LANES + HEAD_DIM + part].set(1.0)
        sk = sk.at[part * LANES + h, h * LANES + HEAD_DIM + 3 + part].set(-1.0)
        cq = cq.at[0, h * LANES + HEAD_DIM + 3 + part].set(1.0)
        ck = ck.at[0, h * LANES + HEAD_DIM + part].set(1.0)
    return place, sq.astype(BF16), sk.astype(BF16), cq, ck, cv


def _fox_prep(z, fb, qg, kg, *, fw, q_col, k_col, v_col, f_col, aug, valid_rows=None):
    m = z.shape[0]
    tm = _pick_tile(m, 256)
    qb, kb, vblk, fblk = q_col // fw, k_col // fw, v_col // fw, f_col // LANES
    in_specs = [pl.BlockSpec((tm, fw), lambda i: (i, qb)),
                pl.BlockSpec((tm, fw), lambda i: (i, kb)),
                pl.BlockSpec((tm, LANES), lambda i: (i, fblk)),
                pl.BlockSpec((1, LANES), lambda i: (0, 0)),
                pl.BlockSpec((1, fw), lambda i: (0, 0)),
                pl.BlockSpec((1, fw), lambda i: (0, 0))]
    args = [z, z, z, fb, qg, kg]
    row = lambda w: pl.BlockSpec((tm, w), lambda i: (i, 0))
    if aug is None:
        return pl.pallas_call(
            functools.partial(_fox_prep_kernel, augment=False),
            out_shape=[jax.ShapeDtypeStruct((m, fw), F32), jax.ShapeDtypeStruct((m, fw), F32),
                       jax.ShapeDtypeStruct((m, LANES), F32)],
            grid=(m // tm,), in_specs=in_specs, out_specs=[row(fw), row(fw), row(LANES)],
            compiler_params=_cp(("parallel",)), name="fox_prep_s",
        )(*args)
    in_specs.append(pl.BlockSpec((tm, fw), lambda i: (i, vblk)))
    args.append(z)
    for a in aug:
        in_specs.append(pl.BlockSpec(a.shape, lambda i: (0, 0)))
        args.append(a)
    nh = fw // HEAD_DIM
    wide = jax.ShapeDtypeStruct((m, nh * LANES), BF16)
    return pl.pallas_call(
        functools.partial(_fox_prep_kernel, augment=True),
        out_shape=[jax.ShapeDtypeStruct((valid_rows or m, fw), F32), jax.ShapeDtypeStruct((m, LANES), F32),
                   wide, wide, wide],
        grid=(m // tm,), in_specs=in_specs,
        out_specs=[row(fw), row(LANES), row(nh * LANES), row(nh * LANES), row(nh * LANES)],
        scratch_shapes=[pltpu.VMEM((1, LANES), F32)],
        compiler_params=_cp(("arbitrary",)), name="fox_prep_p",
    )(*args)


def _flash_kernel(q_ref, k_ref, v_ref, g_ref, o_ref, m_sc, acc_sc, sa_sc, sb_sc, *, tq, tk):
    qi = pl.program_id(1)
    nsub = tq // tk
    m_sc[...] = jnp.full_like(m_sc, NEG_BIG)
    acc_sc[...] = jnp.zeros_like(acc_sc)
    heads = [slice(hh * LANES, (hh + 1) * LANES) for hh in range(2)]

    def key_off(g, j):
        return pl.multiple_of(g * tq + j * tk, tk)

    def scores(g, buf):
        for hh, sl in enumerate(heads):
            for j in range(nsub):
                buf[hh, j] = _dot_nt(q_ref[:, sl], k_ref[pl.ds(key_off(g, j), tk), sl])
                yield

    def update(hh, g, buf, masks):
        sl = heads[hh]
        ss = [buf[hh, j] for j in range(nsub)]
        if masks is not None:
            ss = [jnp.where(mk, s, NEG_BIG) for s, mk in zip(ss, masks)]
        m_old = m_sc[hh]
        mx = ss[0]
        for s in ss[1:]:
            mx = jnp.maximum(mx, s)
        m_new = jnp.maximum(m_old, jnp.broadcast_to(jnp.max(mx, axis=-1, keepdims=True), m_old.shape))
        yield
        m_rep = jnp.concatenate([m_new] * (tk // LANES), axis=-1)
        acc = acc_sc[hh] * jnp.exp2(m_old - m_new)
        for j, s in enumerate(ss):
            lo = j * tk if masks is not None else 0
            pv = _dot(jnp.exp2(s[lo:] - m_rep[lo:]).astype(BF16), v_ref[pl.ds(key_off(g, j), tk), sl])
            acc = acc + pv if lo == 0 else jnp.concatenate([acc[:lo], acc[lo:] + pv], axis=0)
            yield
        m_sc[hh] = m_new
        acc_sc[hh] = acc

    def step(g, cur, nxt, masks):
        gens = [] if nxt is None else [scores(g + 1, nxt)]
        gens += [update(hh, g, cur, masks) for hh in range(2)]
        _run_interleaved(gens)

    rows = _iota((tq, tk), 0)
    cols = _iota((tq, tk), 1)
    causal = [rows >= cols + j * tk for j in range(nsub)]

    _run_interleaved([scores(0, sa_sc)])

    def body(i, carry):
        step(2 * i, sa_sc, sb_sc, None)
        step(2 * i + 1, sb_sc, sa_sc, None)
        return carry

    lax.fori_loop(0, qi // 2, body, 0)

    @pl.when(qi % 2 == 1)
    def _():
        step(qi - 1, sa_sc, sb_sc, None)
        step(qi, sb_sc, None, causal)

    @pl.when(qi % 2 == 0)
    def _():
        step(qi, sa_sc, None, causal)

    outs = []
    for hh in range(2):
        acc = acc_sc[hh]
        outs.append(acc / acc[:, HEAD_DIM:HEAD_DIM + 1])
    o = jnp.where(_iota((1, LANES), 1) < HEAD_DIM, outs[0], pltpu.roll(outs[1], HEAD_DIM, 1))
    o_ref[...] = (o * _silu(g_ref[...])).astype(o_ref.dtype)


def _flash(qa, ka, va, z, g_col):
    lp = qa.shape[0]
    npair = qa.shape[1] // (2 * LANES)
    tk = 256
    tq = _pick_tile(lp, 768, tk)
    gblk = g_col // LANES
    return pl.pallas_call(
        functools.partial(_flash_kernel, tq=tq, tk=tk),
        out_shape=jax.ShapeDtypeStruct((lp, npair * LANES), BF16),
        grid=(npair, lp // tq),
        in_specs=[pl.BlockSpec((tq, 2 * LANES), lambda p, i: (i, p)),
                  pl.BlockSpec((lp, 2 * LANES), lambda p, i: (0, p)),
                  pl.BlockSpec((lp, 2 * LANES), lambda p, i: (0, p)),
                  pl.BlockSpec((tq, LANES), lambda p, i: (i, gblk + p))],
        out_specs=pl.BlockSpec((tq, LANES), lambda p, i: (i, p)),
        scratch_shapes=[pltpu.VMEM((2, tq, LANES), F32), pltpu.VMEM((2, tq, LANES), F32),
                        pltpu.VMEM((2, tq // tk, tq, tk), F32), pltpu.VMEM((2, tq // tk, tq, tk), F32)],
        compiler_params=_cp(("parallel", "arbitrary")),
        name="fox_flash",
    )(qa, ka, va, z)


def _lfpool_kernel(x_ref, o_ref):
    n = x_ref.shape[1]
    jr = _iota((n, 2 * n), 0)
    jc = _iota((n, 2 * n), 1)
    tri2 = ((jr > jc) | (jc == n)).astype(BF16)
    hi, mid, lo = _split3(x_ref[...])
    o_ref[...] = _dot(hi, tri2) + _dot(mid, tri2) + _dot(lo, tri2)


def _lfpool(lft):
    rws, n = lft.shape
    tm = _pick_tile(rws, 2048)
    return pl.pallas_call(
        _lfpool_kernel,
        out_shape=jax.ShapeDtypeStruct((rws, 2 * n), F32),
        grid=(rws // tm,),
        in_specs=[pl.BlockSpec((tm, n), lambda i: (i, 0))],
        out_specs=pl.BlockSpec((tm, 2 * n), lambda i: (i, 0)),
        compiler_params=_cp(("parallel",)), name="fox_lfpool",
    )(lft)


def _decode_kernel(*refs, ts, nheads, pps):
    q_ref = refs[1]
    k_refs = refs[2:2 + pps]
    v_refs = refs[2 + pps:2 + 2 * pps]
    f_refs = refs[2 + 2 * pps:2 + 3 * pps]
    kn_ref, vn_ref, lfn_ref, g_ref, o_ref, m_sc, l_sc, acc_sc, r_sc = refs[2 + 3 * pps:]
    j = pl.program_id(1)
    nj = pl.num_programs(1)
    nrow = nheads * ts
    page = k_refs[0].shape[-1]
    fw = nheads * HEAD_DIM

    @pl.when(j == 0)
    def _():
        m_sc[...] = jnp.full_like(m_sc, NEG_BIG)
        l_sc[...] = jnp.zeros_like(l_sc)
        acc_sc[...] = jnp.zeros_like(acc_sc)
        r_sc[...] = jnp.zeros_like(r_sc)

    q = q_ref[...]

    def per_row(x):
        return jnp.concatenate([x] * ts, axis=0)

    def update(blocks):
        ss = [_dot(q, kt.astype(BF16)) + bias for kt, _, bias in blocks]
        m_run, l_run, acc = m_sc[...], l_sc[...], acc_sc[...]
        half = (len(blocks) + 1) // 2
        for lo in range(0, len(blocks), half):
            m_new = m_run
            for s in ss[lo:lo + half]:
                m_new = jnp.maximum(m_new, jnp.max(s, axis=-1, keepdims=True))
            alpha = jnp.exp(m_run - m_new)
            l_run = alpha * l_run
            acc = acc * alpha
            for s, (_, vt, _) in zip(ss[lo:lo + half], blocks[lo:lo + half]):
                p = jnp.exp(s - m_new)
                l_run = l_run + jnp.sum(p, axis=-1, keepdims=True)
                acc = acc + _dot_nt(p.astype(BF16), vt.astype(BF16))
            m_run = m_new
        l_sc[...] = l_run
        acc_sc[...] = acc
        m_sc[...] = m_run

    run = r_sc[...]
    blocks = []
    for k_ref, v_ref, f_ref in zip(k_refs, v_refs, f_refs):
        f = f_ref[...]
        blocks.append((k_ref[...].reshape(fw, page), v_ref[...].reshape(fw, page),
                       per_row(f[:, :page] + run)))
        run = run + f[:, page:page + 1]
    r_sc[...] = run
    update(blocks)

    @pl.when(j == nj - 1)
    def _():
        nn = kn_ref.shape[-1]
        upper = (_iota((nn, nn), 0) <= _iota((nn, nn), 1)).astype(F32)
        cum = per_row(_dot(lfn_ref[...], upper, HI))
        tq = _iota((nrow, nn), 0) // nheads
        tk = _iota((nrow, nn), 1)
        update([(kn_ref[...], vn_ref[...], jnp.where((tk <= tq) & (tk < ts), -cum, NEG_BIG))])
        out = acc_sc[...] / l_sc[...]
        own = (_iota((nrow, fw), 0) % nheads) == (_iota((nrow, fw), 1) // HEAD_DIM)
        sel = (_iota((8, nrow), 1) // nheads == _iota((8, nrow), 0)).astype(F32)
        o = _dot(sel, jnp.where(own, out, 0.0), HI)
        o_ref[...] = (o * _silu(g_ref[...])).astype(o_ref.dtype)


def _decode(page_table, qbd, kt_pool, vt_pool, lfo, layer, knew_t, vnew_t, lfn_t, gate, *, ts, nheads, pps):
    bsz, nrow, fw = qbd.shape
    npg = page_table.shape[1]
    page = kt_pool.shape[-1]
    nn = knew_t.shape[-1]

    def pool(shape, which):
        nd = len(shape)
        return pl.BlockSpec((None, None) + shape,
                            lambda b, j, pt: (layer, pt[b, npg - 1 - pps * j - which]) + (0,) * nd)

    per_b = lambda w0, w1: pl.BlockSpec((None, w0, w1), lambda b, j, pt: (b, 0, 0))
    kv = (nheads, HEAD_DIM, page)
    each = range(pps)
    grid_spec = pltpu.PrefetchScalarGridSpec(
        num_scalar_prefetch=1, grid=(bsz, npg // pps),
        in_specs=([per_b(nrow, fw)] + [pool(kv, w) for w in each] + [pool(kv, w) for w in each]
                  + [pool((nheads, 2 * page), w) for w in each]
                  + [per_b(fw, nn), per_b(fw, nn), per_b(nheads, nn), per_b(8, fw)]),
        out_specs=per_b(8, fw),
        scratch_shapes=[pltpu.VMEM((nrow, 1), F32), pltpu.VMEM((nrow, 1), F32), pltpu.VMEM((nrow, fw), F32),
                        pltpu.VMEM((nheads, 1), F32)])
    return pl.pallas_call(
        functools.partial(_decode_kernel, ts=ts, nheads=nheads, pps=pps),
        out_shape=jax.ShapeDtypeStruct((bsz, 8, fw), BF16),
        grid_spec=grid_spec,
        compiler_params=_cp(("parallel", "arbitrary")),
        name="fox_decode",
    )(page_table, qbd, *([kt_pool] * pps), *([vt_pool] * pps), *([lfo] * pps), knew_t, vnew_t, lfn_t, gate)


def _merge_kernel(x_ref, or_ref, of_ref, mr_ref, mf_ref, wpr_ref, wpf_ref, wo_ref, o_ref):
    a = _dot(or_ref[...], wpr_ref[...])
    b = _dot(of_ref[...], wpf_ref[...])
    mixed = jax.nn.sigmoid(mr_ref[...]) * a + jax.nn.sigmoid(mf_ref[...]) * b
    o_ref[...] = x_ref[...] + _dot(mixed.astype(BF16), wo_ref[...])


def _merge(x, o_r, o_f, z, mr_col, mf_col, wpr, wpf, wo):
    m, d = x.shape
    rw = o_r.shape[1]
    fw = o_f.shape[1]
    tm = _pick_tile(m, 256)
    whole = lambda a: pl.BlockSpec(a.shape, lambda i: (0, 0), pipeline_mode=pl.Buffered(1))
    return pl.pallas_call(
        _merge_kernel,
        out_shape=jax.ShapeDtypeStruct((m, d), F32),
        grid=(m // tm,),
        in_specs=[pl.BlockSpec((tm, d), lambda i: (i, 0)),
                  pl.BlockSpec((tm, rw), lambda i: (i, 0)),
                  pl.BlockSpec((tm, fw), lambda i: (i, 0)),
                  pl.BlockSpec((tm, d), lambda i: (i, mr_col // d)),
                  pl.BlockSpec((tm, d), lambda i: (i, mf_col // d)),
                  whole(wpr), whole(wpf), whole(wo)],
        out_specs=pl.BlockSpec((tm, d), lambda i: (i, 0)),
        compiler_params=_cp(("parallel",)),
        name="merge",
    )(x, o_r, o_f, z, z, wpr, wpf, wo)


def _pad_cols(a, width):
    return jnp.pad(a, [(0, 0)] * (a.ndim - 1) + [(0, width - a.shape[-1])])


def kernel(x_prompt, x_sample, cache_k, cache_v, cache_logf, state_wkv, state_shift, page_table, meta_tokens, norm_gain, w_in, r_mu, r_w0, r_w2, r_a0, r_a2, r_v0, r_v1, r_v2, r_kk, r_ka, r_rk, r_lnx_w, r_lnx_b, f_bias, f_qgain, f_kgain, w_proj_r, w_proj_f, w_out):
    bsz, seq, d = x_prompt.shape
    db, ts, _ = x_sample.shape
    depth = w_in.shape[0]
    rw = r_w0.shape[1]
    rh = r_rk.shape[1]
    fh = f_bias.shape[1]
    fw = fh * HEAD_DIM
    lora_w = r_w2.shape[1]
    lora_a = r_a2.shape[1]
    shift_w = 3 * rw + lora_w + lora_a
    page = cache_k.shape[2]
    npool = cache_k.shape[1]
    npg = page_table.shape[1]
    assert bsz == 1 and rw == rh * HEAD_DIM and fw == rw and d == 2 * rw
    assert lora_w <= LANES and lora_a <= LANES and fh <= LANES and ts * fh <= LANES
    pps = next(n for n in (8, 4, 2, 1) if npg % n == 0)

    s_gr = shift_w
    s_q = s_gr + rw
    s_k = s_q + fw
    s_v = s_k + fw
    s_f = s_v + fw
    s_gf = s_f + fh
    s_mr = s_gf + fw
    s_mf = s_mr + d
    c_gr = 3 * rw
    c_q = c_gr + rw
    c_k = c_q + fw
    c_v = c_k + fw
    c_gf = c_v + fw
    c_mr = c_gf + fw
    c_mf = c_mr + d
    c_wa = c_mf + d
    c_f = c_wa + 2 * LANES
    n_pack = _round_up(c_f + LANES, INPROJ_TN)

    segs = ((0, 0, 3 * rw), (s_gr, c_gr, rw), (s_q, c_q, fw), (s_k, c_k, fw), (s_v, c_v, fw), (s_gf, c_gf, fw),
            (s_mr, c_mr, d), (s_mf, c_mf, d), (3 * rw, c_wa, lora_w), (3 * rw + lora_w, c_wa + LANES, lora_a),
            (s_f, c_f, fh))
    w_pack = _pack_weights(jnp.swapaxes(w_in, 1, 2), segs, c_wa, n_pack)

    def pack_shift(a):
        return a[..., :3 * rw], jnp.concatenate([_pad_cols(a[..., 3 * rw:3 * rw + lora_w], LANES),
                                                 _pad_cols(a[..., 3 * rw + lora_w:], LANES)], axis=-1)

    def unpack_shift(zrow):
        return jnp.concatenate([zrow[..., :3 * rw], zrow[..., c_wa:c_wa + lora_w],
                                zrow[..., c_wa + LANES:c_wa + LANES + lora_a]], axis=-1)

    length = seq + N_META
    lp = _round_up(length, 768) if length > 768 else _round_up(length, 256)
    chunk = 64
    xp = jnp.concatenate([meta_tokens.astype(F32), x_prompt[0], jnp.zeros((lp - length, d), F32)], axis=0)
    xs = x_sample.reshape(db * ts, d)
    aug = _fox_aug_consts(fw)
    s0_p = jnp.zeros((1, rh // 2, LANES, LANES), F32)

    kt_pool = jnp.transpose(cache_k, (0, 1, 3, 4, 2))
    vt_pool = jnp.transpose(cache_v, (0, 1, 3, 4, 2))
    lft = jnp.swapaxes(cache_logf, 2, 3).reshape(depth * npool * fh, page)
    lfo = _lfpool(lft).reshape(depth, npool, fh, 2 * page)

    vf_p = None
    vf_s = None
    outs = {n: [] for n in ("kp", "vp", "lp", "sp", "hp", "ks", "vs", "ls", "ss", "hs")}
    for l in range(depth):
        gain = norm_gain[l][None, :]
        mu3, muw = pack_shift(r_mu[l][None, :])
        if l == 0:
            v0 = jnp.zeros((1, rw), F32)
            v1 = jnp.zeros((rw, LANES), BF16)
            v2 = jnp.zeros((LANES, rw), BF16)
        else:
            v0 = r_v0[l - 1][None, :]
            v1 = _pad_cols(r_v1[l - 1], LANES).astype(BF16)
            v2 = jnp.pad(r_v2[l - 1], ((0, LANES - r_v2.shape[1]), (0, 0))).astype(BF16)
        prm = (mu3, muw, r_w0[l][None, :],
               jnp.pad(r_w2[l], ((0, LANES - lora_w), (0, 0))).astype(BF16),
               r_a0[l][None, :],
               jnp.pad(r_a2[l], ((0, LANES - lora_a), (0, 0))).astype(BF16),
               r_kk[l][None, :], r_ka[l][None, :], v0, v1, v2)
        rk = r_rk[l].reshape(1, rw)
        lnw = r_lnx_w[l][None, :]
        lnb = r_lnx_b[l][None, :]
        fb = _pad_cols(f_bias[l][None, :], LANES)
        qg = jnp.tile(f_qgain[l], fh)[None, :]
        kg = jnp.tile(f_kgain[l], fh)[None, :]
        wpr = w_proj_r[l].astype(BF16)
        wpf = w_proj_f[l].astype(BF16)
        wo = w_out[l].astype(BF16)

        z = _inproj(xp, gain, w_pack, l)
        res = _wkv_fused(z, vf_p, prm, s0_p, rk, lnw, lnb, rw=rw, wa_col=c_wa, gate_col=c_gr, chunk=chunk,
                         chunks_per_step=2, valid_len=length, emit_v=(l == 0 and depth > 1))
        o_r, s_fin = res[0], res[-1]
        if l == 0 and depth > 1:
            vf_p = res[1]
        kn, lf, qa, ka, va = _fox_prep(z, fb, qg, kg, fw=fw, q_col=c_q, k_col=c_k, v_col=c_v, f_col=c_f, aug=aug,
                                       valid_rows=length)
        o_f = _flash(qa, ka, va, z, c_gf)
        xp = _merge(xp, o_r, o_f, z, c_mr, c_mf, wpr, wpf, wo)
        outs["kp"].append(kn.reshape(1, length, fh, HEAD_DIM))
        outs["vp"].append(z[:length, c_v:c_v + fw].reshape(1, length, fh, HEAD_DIM))
        outs["lp"].append(lf[:length, :fh][None])
        outs["sp"].append(_unpack_state(s_fin))
        outs["hp"].append(unpack_shift(z[length - 1:length]))

        zs = _inproj(xs, gain, w_pack, l)
        zs3 = zs.reshape(db, ts, n_pack)
        sh3, shw = pack_shift(state_shift[l])
        prev3 = jnp.concatenate([sh3[:, None], zs3[:, :-1, :3 * rw]], axis=1).reshape(db * ts, 3 * rw)
        prevw = jnp.concatenate([shw[:, None], zs3[:, :-1, c_wa:c_wa + 2 * LANES]], axis=1).reshape(db * ts, 2 * LANES)
        res = _rwkv_prep(zs, prev3, prevw, vf_s, prm, rw=rw, wa_col=c_wa, chained=False)
        if l == 0:
            vf_s = res[3]
        pad_t = lambda a: jnp.pad(a.reshape(db, ts, -1), ((0, 0), (0, 8 - ts), (0, 0)))
        r, lw, k2, v, kk, b = (pad_t(a) for a in res)
        gate_r = pad_t(zs[:, c_gr:c_gr + rw])
        o_r, s_fin = _wkv(r, lw, k2, v, kk, b, gate_r, 0, _pack_state(state_wkv[l]), rk, lnw, lnb,
                          chunk=8, chunks_per_step=1, valid_len=None, pairs_per_step=rw // LANES)
        o_r = o_r[:, :ts].reshape(db * ts, rw)
        qn, kn, lf = _fox_prep(zs, fb, qg, kg, fw=fw, q_col=c_q, k_col=c_k, v_col=c_v, f_col=c_f, aug=None)
        vn = zs[:, c_v:c_v + fw]
        q4 = qn.reshape(db, ts, fh, HEAD_DIM) * HEAD_DIM ** -0.5
        qbd = jnp.einsum("bthd,hg->bthgd", q4, jnp.eye(fh, dtype=F32)).reshape(db, ts * fh, fw).astype(BF16)
        pad_k = lambda a: _pad_cols(jnp.swapaxes(a.reshape(db, ts, -1), 1, 2), page)
        gate_f = pad_t(zs[:, c_gf:c_gf + fw])
        o_f = _decode(page_table, qbd, kt_pool, vt_pool, lfo, l, pad_k(kn), pad_k(vn), pad_k(lf[:, :fh]), gate_f,
                      ts=ts, nheads=fh, pps=pps)
        o_f = o_f[:, :ts].reshape(db * ts, fw)
        xs = _merge(xs, o_r, o_f, zs, c_mr, c_mf, wpr, wpf, wo)
        outs["ks"].append(kn.reshape(db, ts, fh, HEAD_DIM))
        outs["vs"].append(vn.reshape(db, ts, fh, HEAD_DIM))
        outs["ls"].append(lf[:, :fh].reshape(db, ts, fh))
        outs["ss"].append(_unpack_state(s_fin))
        outs["hs"].append(unpack_shift(zs3[:, -1]))

    y_prompt = xp[N_META:length][None]
    y_sample = xs.reshape(db, ts, d)
    st = lambda n: jnp.stack(outs[n])
    return (y_prompt, y_sample, st("kp"), st("vp"), st("lp"), st("sp"), st("hp"),
            st("ks"), st("vs"), st("ls"), st("ss"), st("hs"))
```

```python
import functools
import math

import jax
import jax.numpy as jnp
from jax import lax
from jax.experimental import pallas as pl
from jax.experimental.pallas import tpu as pltpu

F32 = jnp.float32
BF16 = jnp.bfloat16
HI = lax.Precision.HIGHEST

LANES = 128
HEAD_DIM = 64
N_META = 16
NORM_EPS = 1e-6
LNX_EPS = 1e-5 * HEAD_DIM
DECAY_OFFSET = 0.5
NEG_BIG = -1e30
LOG2E = math.log2(math.e)
VMEM_LIMIT = 56 * 1024 * 1024
INPROJ_TN = 1280


def _cp(sem, vmem=VMEM_LIMIT):
    return pltpu.CompilerParams(dimension_semantics=sem, vmem_limit_bytes=vmem)


def _round_up(x, m):
    return (x + m - 1) // m * m


def _pick_tile(n, cap, mult=128):
    if n <= cap:
        return n
    best = mult
    t = mult
    while t <= cap:
        if n % t == 0:
            best = t
        t += mult
    return best


def _dot(a, b, precision=None):
    return jnp.dot(a, b, preferred_element_type=F32, precision=precision)


def _dot_nt(a, b, precision=None):
    return lax.dot_general(a, b, (((1,), (1,)), ((), ())), preferred_element_type=F32, precision=precision)


def _bdot(a, b):
    return _dot(a.astype(BF16), b.astype(BF16))


def _iota(shape, dim):
    return lax.broadcasted_iota(jnp.int32, shape, dim)


def _split2(x):
    hi = x.astype(BF16)
    return hi, (x - hi.astype(F32)).astype(BF16)


def _split3(x):
    hi = x.astype(BF16)
    r1 = x - hi.astype(F32)
    mid = r1.astype(BF16)
    return hi, mid, (r1 - mid.astype(F32)).astype(BF16)


def _head_block_ones(dtype=F32):
    return (_iota((LANES, LANES), 0) // HEAD_DIM == _iota((LANES, LANES), 1) // HEAD_DIM).astype(dtype)


def _head_sums(x):
    bd = _head_block_ones(BF16)
    parts = [_head_sums2(x[:, i * LANES:(i + 1) * LANES], bd) for i in range(x.shape[1] // LANES)]
    return parts[0] if len(parts) == 1 else jnp.concatenate(parts, axis=-1)


def _head_sums2(x, bd):
    hi, lo = _split2(x)
    return _dot(hi, bd) + _dot(lo, bd)


def _softplus(x):
    return jnp.maximum(x, 0.0) + jnp.log(1.0 + jnp.exp(-jnp.abs(x)))


def _silu(x):
    return x * jax.nn.sigmoid(x)


def _inproj_kernel(x_ref, g_ref, w_ref, o_ref, xn_ref):
    @pl.when(pl.program_id(1) == 0)
    def _():
        x = x_ref[...]
        ms = jnp.mean(x * x, axis=-1, keepdims=True)
        xn_ref[...] = (x * lax.rsqrt(ms + NORM_EPS) * g_ref[...]).astype(BF16)

    o_ref[...] = _dot(xn_ref[...], w_ref[...])


def _inproj(x, gain, w, layer):
    m, d = x.shape
    n = w.shape[2]
    tm = _pick_tile(m, 768)
    tn = _pick_tile(n, INPROJ_TN)
    return pl.pallas_call(
        _inproj_kernel,
        out_shape=jax.ShapeDtypeStruct((m, n), F32),
        grid=(m // tm, n // tn),
        in_specs=[pl.BlockSpec((tm, d), lambda i, j: (i, 0)),
                  pl.BlockSpec((1, d), lambda i, j: (0, 0)),
                  pl.BlockSpec((None, d, tn), lambda i, j: (layer, 0, j))],
        out_specs=pl.BlockSpec((tm, tn), lambda i, j: (i, j)),
        scratch_shapes=[pltpu.VMEM((tm, d), BF16)],
        compiler_params=_cp(("parallel", "arbitrary")),
        name="inproj",
    )(x, gain, w)


def _pack_kernel(wt_ref, o_ref, *, segs, tail):
    o_ref[:, tail:] = jnp.zeros((o_ref.shape[0], o_ref.shape[1] - tail), o_ref.dtype)
    for src, dst, size in segs:
        rows = _round_up(size, LANES)
        xt = wt_ref[src:src + rows, :].T
        if rows != size:
            xt = jnp.where(_iota((1, rows), 1) < size, xt, 0.0)
        o_ref[:, dst:dst + rows] = xt.astype(o_ref.dtype)


def _pack_weights(wt, segs, tail, n_pack):
    depth, n_in, d = wt.shape
    assert all(src + _round_up(size, LANES) <= n_in for src, _, size in segs)
    tm = _pick_tile(d, 256)
    return pl.pallas_call(
        functools.partial(_pack_kernel, segs=segs, tail=tail),
        out_shape=jax.ShapeDtypeStruct((depth, d, n_pack), BF16),
        grid=(depth, d // tm),
        in_specs=[pl.BlockSpec((None, n_in, tm), lambda l, i: (l, 0, i))],
        out_specs=pl.BlockSpec((None, tm, n_pack), lambda l, i: (l, i, 0)),
        compiler_params=_cp(("parallel", "parallel")),
        name="pack_w",
    )(wt)


def _rwkv_prep_math(z3, zw, p3, pw, vf, prm, has_vres):
    (mu3, muw, w0, w2, a0, a2, kkg, kag, v0, v1, v2) = prm
    rw = w0.shape[1]
    x3 = z3 + (p3 - z3) * mu3
    xw = zw + (pw - zw) * muw
    r = x3[:, :rw]
    k = x3[:, rw:2 * rw]
    v = x3[:, 2 * rw:]
    wd = xw[:, :LANES]
    ad = xw[:, LANES:]
    w_raw = w0 + _dot(jnp.tanh(wd).astype(BF16), w2)
    w_log = -_softplus(-w_raw) - DECAY_OFFSET
    lw = -jnp.exp(w_log)
    a = jax.nn.sigmoid(a0 + _dot(ad.astype(BF16), a2))
    if has_vres:
        lora = _dot(_dot(v.astype(BF16), v1).astype(BF16), v2)
        v = v + (vf - v) * jax.nn.sigmoid(v0 + lora)
    kk = k * kkg
    ss = _head_sums(kk * kk)
    kk = kk * lax.rsqrt(jnp.maximum(ss, 1e-24))
    k2 = k * (1.0 + (a - 1.0) * kag)
    return r, lw, k2, v, kk, kk * a


def _rwkv_prep_kernel(*refs, has_vres):
    z3_ref, zw_ref, p3_ref, pw_ref = refs[:4]
    pos = 4
    vf = None
    if has_vres:
        vf = refs[pos][...]
        pos += 1
    prm = [r[...] for r in refs[pos:pos + 11]]
    outs = refs[pos + 11:]
    res = _rwkv_prep_math(z3_ref[...], zw_ref[...], p3_ref[...], pw_ref[...], vf, prm, has_vres)
    for o_ref, val in zip(outs, res):
        o_ref[...] = val


def _rwkv_prep(z, prev3, prevw, vfirst, prm, *, rw, wa_col):
    m = z.shape[0]
    tm = _pick_tile(m, 256)
    has_vres = vfirst is not None
    wa_blk = wa_col // (2 * LANES)
    in_specs = [pl.BlockSpec((tm, 3 * rw), lambda i: (i, 0)),
                pl.BlockSpec((tm, 2 * LANES), lambda i: (i, wa_blk)),
                pl.BlockSpec((tm, 3 * rw), lambda i: (i, 0)),
                pl.BlockSpec((tm, 2 * LANES), lambda i: (i, 0))]
    args = [z, z, prev3, prevw]
    if has_vres:
        in_specs.append(pl.BlockSpec((tm, rw), lambda i: (i, 0)))
        args.append(vfirst)
    for p in prm:
        in_specs.append(pl.BlockSpec(p.shape, lambda i: (0, 0)))
        args.append(p)
    out = pl.pallas_call(
        functools.partial(_rwkv_prep_kernel, has_vres=has_vres),
        out_shape=[jax.ShapeDtypeStruct((m, rw), F32)] * 6,
        grid=(m // tm,),
        in_specs=in_specs,
        out_specs=[pl.BlockSpec((tm, rw), lambda i: (i, 0))] * 6,
        compiler_params=_cp(("parallel",)),
        name="rwkv_prep",
    )(*args)
    return out


def _wkv_chunk_setup(r, lw, k, v, kk, b, rk, consts, C):
    tri_b, lane_lo, strict, incl, eye, level_masks, bd = consts
    C2 = 2 * C
    g3 = _dot(tri_b, jnp.concatenate(_split3(lw), axis=-1))
    yield
    g = g3[:, :LANES] + g3[:, LANES:2 * LANES] + g3[:, 2 * LANES:]
    g_last = g[C - 1:C, :]
    e_g = jnp.exp(g)
    e_ng = jnp.exp(-g)
    e_tail = jnp.exp(g_last - g)

    def stack(x):
        return jnp.concatenate([jnp.where(lane_lo, x, 0.0), jnp.where(lane_lo, 0.0, x)], axis=0).astype(BF16)

    left = jnp.concatenate([stack(-kk * jnp.exp(g - lw)), stack(r * e_g)], axis=0)
    right = jnp.concatenate([stack(b * e_ng), stack(k * e_ng)], axis=0)
    v2 = stack(v)
    sc = _dot_nt(left, right)
    yield
    m_ab = jnp.where(strict, sc[:C2, :C2], 0.0)
    m_ak = jnp.where(strict, sc[:C2, C2:], 0.0)
    p_rb = jnp.where(incl, sc[C2:, :C2], 0.0).astype(BF16)
    p_rk = jnp.where(incl, sc[C2:, C2:], 0.0)
    mv = _bdot(m_ak, v2)
    pv = _bdot(p_rk, v2)
    kv_t = _dot(v2.astype(F32).T.astype(BF16), stack(k * e_tail))
    bonus = _head_sums2(r * k * rk, bd) * v
    yield

    same, offs = level_masks
    mb = jnp.where(same, m_ab, 0.0)
    x = eye + mb
    pw = mb
    steps = 1
    while steps * 2 < min(8, C):
        pw = _bdot(pw, pw)
        yield
        x = x + _bdot(x, pw)
        yield
        steps *= 2
    for off_mask in offs:
        xb = x.astype(BF16)
        t = _dot(xb, jnp.where(off_mask, m_ab, 0.0).astype(BF16)).astype(BF16)
        yield
        x = x + _dot(t, xb)
        yield
    return dict(left=left, x=x.astype(BF16), mv=mv, pv=pv, kv_t=kv_t, p_rb=p_rb, bh=stack(b * e_tail),
                decay=jnp.exp(g_last), bonus=bonus)


def _wkv_chunk_apply(pre, s_prev, gate, lnw, lnb, bd, C):
    C2 = 2 * C
    hs = _dot_nt(pre["left"], s_prev.astype(BF16))
    yield
    u = _dot(pre["x"], (hs[:C2] + pre["mv"]).astype(BF16))
    yield
    o2 = hs[C2:] + pre["pv"] + _dot(pre["p_rb"], u.astype(BF16))
    s_new = s_prev * pre["decay"] + pre["kv_t"] + _dot(u.T.astype(BF16), pre["bh"])
    yield
    o = o2[:C] + o2[C:]
    mean = _head_sums2(o, bd) * (1.0 / HEAD_DIM)
    yield
    d = o - mean
    var = _head_sums2(d * d, bd) * (1.0 / HEAD_DIM)
    yield
    y = d * lax.rsqrt(var + LNX_EPS) * lnw + lnb + pre["bonus"]
    return (y * _silu(gate)), s_new


def _run_interleaved(gens):
    results = [None] * len(gens)
    live = list(range(len(gens)))
    while live:
        for i in list(live):
            try:
                next(gens[i])
            except StopIteration as stop:
                results[i] = stop.value
                live.remove(i)
    return results


def _wkv_steps(src, g_ref, s0_ref, rk_ref, lnw_ref, lnb_ref, o_ref, sfin_ref, s_sc, *, chunk, nsub, valid_len, npair):
    r_src, lw_src, k_src, v_src, kk_src, b_src = src
    c = pl.program_id(2)
    n_c = pl.num_programs(2)
    C = chunk
    C2 = 2 * C

    @pl.when(c == 0)
    def _():
        s_sc[...] = s0_ref[...]

    i2 = _iota((C2, C2), 0)
    j2 = _iota((C2, C2), 1)
    base = min(8, C)
    offs = []
    n = base
    while n < C:
        offs.append(((i2 // (2 * n)) == (j2 // (2 * n))) & ((i2 // n) != (j2 // n)))
        n *= 2
    bd = _head_block_ones(BF16)
    consts = ((_iota((C, C), 0) >= _iota((C, C), 1)).astype(BF16),
              _iota((1, LANES), 1) < HEAD_DIM,
              i2 > j2, i2 >= j2, (i2 == j2).astype(F32),
              ((i2 // base) == (j2 // base), offs), bd)
    lanes = [slice(p * LANES, (p + 1) * LANES) for p in range(npair)]
    rows = [slice(u * C, (u + 1) * C) for u in range(nsub)]

    setups = []
    for u, rs in enumerate(rows):
        ok = None
        if valid_len is not None:
            ok = ((c * nsub + u) * C + _iota((C, 1), 0)) < valid_len
        for sl in lanes:
            lw, k, kk, b = lw_src[rs, sl], k_src[rs, sl], kk_src[rs, sl], b_src[rs, sl]
            if ok is not None:
                lw = jnp.where(ok, lw, 0.0)
                k = jnp.where(ok, k, 0.0)
                kk = jnp.where(ok, kk, 0.0)
                b = jnp.where(ok, b, 0.0)
            setups.append(_wkv_chunk_setup(r_src[rs, sl], lw, k, v_src[rs, sl], kk, b, rk_ref[:, sl], consts, C))
    pre = _run_interleaved(setups)

    states = [s_sc[p] for p in range(npair)]
    outs = []
    for u, rs in enumerate(rows):
        res = _run_interleaved([_wkv_chunk_apply(pre[u * npair + p], states[p], g_ref[rs, sl], lnw_ref[:, sl],
                                                 lnb_ref[:, sl], bd, C) for p, sl in enumerate(lanes)])
        states = [s_new for _, s_new in res]
        outs.append([y for y, _ in res])
    for u, rs in enumerate(rows):
        for p, sl in enumerate(lanes):
            o_ref[rs, sl] = outs[u][p].astype(o_ref.dtype)
    for p in range(npair):
        s_sc[p] = states[p]

    @pl.when(c == n_c - 1)
    def _():
        sfin_ref[...] = s_sc[...]


def _wkv_kernel(r_ref, lw_ref, k_ref, v_ref, kk_ref, b_ref, g_ref, s0_ref, rk_ref, lnw_ref, lnb_ref,
                o_ref, sfin_ref, s_sc, **kw):
    _wkv_steps((r_ref, lw_ref, k_ref, v_ref, kk_ref, b_ref), g_ref, s0_ref, rk_ref, lnw_ref, lnb_ref,
               o_ref, sfin_ref, s_sc, **kw)


def _wkv_fused_kernel(*refs, has_vres, emit_v, **kw):
    z3_ref, zw_ref, p3_ref, pw_ref = refs[:4]
    pos = 4
    vf = None
    if has_vres:
        vf = refs[pos][...]
        pos += 1
    prm = [r[...] for r in refs[pos:pos + 11]]
    g_ref, s0_ref, rk_ref, lnw_ref, lnb_ref = refs[pos + 11:pos + 16]
    outs = refs[pos + 16:]
    o_ref = outs[0]
    sfin_ref, s_sc = outs[-2], outs[-1]
    z3 = z3_ref[...]
    zw = zw_ref[...]
    first = pl.program_id(2) == 0
    row0 = _iota((z3.shape[0], 1), 0) == 0
    p3 = jnp.where(row0, jnp.where(first, 0.0, p3_ref[7:8, :]), pltpu.roll(z3, 1, 0))
    pw = jnp.where(row0, jnp.where(first, 0.0, pw_ref[7:8, :]), pltpu.roll(zw, 1, 0))
    src = _rwkv_prep_math(z3, zw, p3, pw, vf, prm, has_vres)
    if emit_v:
        outs[1][...] = src[3]
    _wkv_steps(src, g_ref, s0_ref, rk_ref, lnw_ref, lnb_ref, o_ref, sfin_ref, s_sc, **kw)


def _wkv_fused(z, vfirst, prm, s0, rk, lnw, lnb, *, rw, wa_col, gate_col, chunk, chunks_per_step, valid_len, emit_v):
    t = z.shape[0]
    rows = chunk * chunks_per_step
    npair = rw // LANES
    has_vres = vfirst is not None
    sub = rows // 8
    wa_blk = wa_col // (2 * LANES)
    im = lambda f: (lambda bi, g, c: f(c))
    in_specs = [pl.BlockSpec((rows, 3 * rw), im(lambda c: (c, 0))),
                pl.BlockSpec((rows, 2 * LANES), im(lambda c: (c, wa_blk))),
                pl.BlockSpec((8, 3 * rw), im(lambda c: (jnp.maximum(c * sub - 1, 0), 0))),
                pl.BlockSpec((8, 2 * LANES), im(lambda c: (jnp.maximum(c * sub - 1, 0), wa_blk)))]
    args = [z, z, z, z]
    if has_vres:
        in_specs.append(pl.BlockSpec((rows, rw), im(lambda c: (c, 0))))
        args.append(vfirst)
    for p in prm:
        in_specs.append(pl.BlockSpec(p.shape, im(lambda c: (0, 0))))
        args.append(p)
    par = pl.BlockSpec((1, rw), im(lambda c: (0, 0)))
    st = pl.BlockSpec((None, npair, LANES, LANES), im(lambda c: (0, 0, 0, 0)))
    in_specs += [pl.BlockSpec((rows, rw), im(lambda c: (c, gate_col // rw))), st, par, par, par]
    args += [z, s0, rk, lnw, lnb]
    seq = pl.BlockSpec((rows, rw), im(lambda c: (c, 0)))
    out_shape = [jax.ShapeDtypeStruct((t, rw), BF16)]
    out_specs = [seq]
    if emit_v:
        out_shape.append(jax.ShapeDtypeStruct((t, rw), F32))
        out_specs.append(seq)
    out_shape.append(jax.ShapeDtypeStruct((1, npair, LANES, LANES), F32))
    out_specs.append(st)
    return pl.pallas_call(
        functools.partial(_wkv_fused_kernel, has_vres=has_vres, emit_v=emit_v, chunk=chunk, nsub=chunks_per_step,
                          valid_len=valid_len, npair=npair),
        out_shape=out_shape,
        grid=(1, 1, t // rows),
        in_specs=in_specs,
        out_specs=out_specs,
        scratch_shapes=[pltpu.VMEM((npair, LANES, LANES), F32)],
        compiler_params=_cp(("parallel", "parallel", "arbitrary")),
        name="wkv_fused",
    )(*args)


def _wkv(r, lw, k, v, kk, b, gate, gate_col, s0, rk, lnw, lnb, *, chunk, chunks_per_step, valid_len,
         pairs_per_step):
    bsz, t, rw = r.shape
    pw = pairs_per_step
    wid = pw * LANES
    ngrp = rw // wid
    rows = chunk * chunks_per_step
    nchunk = t // rows
    gblk = gate_col // wid
    seq = pl.BlockSpec((None, rows, wid), lambda bi, g, c: (bi, c, g))
    par = pl.BlockSpec((1, wid), lambda bi, g, c: (0, g))
    st = pl.BlockSpec((None, pw, LANES, LANES), lambda bi, g, c: (bi, g, 0, 0))
    return pl.pallas_call(
        functools.partial(_wkv_kernel, chunk=chunk, nsub=chunks_per_step, valid_len=valid_len, npair=pw),
        out_shape=[jax.ShapeDtypeStruct((bsz, t, rw), BF16),
                   jax.ShapeDtypeStruct((bsz, rw // LANES, LANES, LANES), F32)],
        grid=(bsz, ngrp, nchunk),
        in_specs=[seq] * 6 + [pl.BlockSpec((None, rows, wid), lambda bi, g, c: (bi, c, gblk + g)),
                              st, par, par, par],
        out_specs=[seq, st],
        scratch_shapes=[pltpu.VMEM((pw, LANES, LANES), F32)],
        compiler_params=_cp(("parallel", "parallel", "arbitrary")),
        name="wkv",
    )(r, lw, k, v, kk, b, gate, s0, rk, lnw, lnb)


def _pack_state(s):
    bsz, h, n, _ = s.shape
    s = s.reshape(bsz, h // 2, 2, n, n)
    z = jnp.zeros_like(s[:, :, 0])
    top = jnp.concatenate([s[:, :, 0], z], axis=-1)
    bot = jnp.concatenate([z, s[:, :, 1]], axis=-1)
    return jnp.concatenate([top, bot], axis=-2)


def _unpack_state(sp):
    bsz, p = sp.shape[:2]
    n = HEAD_DIM
    s = jnp.stack([sp[:, :, :n, :n], sp[:, :, n:, n:]], axis=2)
    return s.reshape(bsz, 2 * p, n, n)


def _fox_prep_kernel(*refs, augment):
    q_ref, k_ref, f_ref, fb_ref, qg_ref, kg_ref = refs[:6]
    q = q_ref[...]
    k = k_ref[...]
    qn = q * lax.rsqrt(_head_sums(q * q) * (1.0 / HEAD_DIM) + NORM_EPS) * qg_ref[...]
    kn = k * lax.rsqrt(_head_sums(k * k) * (1.0 / HEAD_DIM) + NORM_EPS) * kg_ref[...]
    logf = -_softplus(-(f_ref[...] + fb_ref[...]))
    if not augment:
        qn_ref, kn_ref, lf_ref = refs[6:]
        qn_ref[...] = qn
        kn_ref[...] = kn
        lf_ref[...] = logf
        return
    v_ref, pl_ref, sq_ref, sk_ref, cq_ref, ck_ref, cv_ref, kn_ref, lf_ref, qa_ref, ka_ref, va_ref, carry = refs[6:]

    @pl.when(pl.program_id(0) == 0)
    def _():
        carry[...] = jnp.zeros_like(carry)

    tm = q.shape[0]
    tri = (_iota((tm, tm), 0) >= _iota((tm, tm), 1)).astype(F32)
    c = carry[...] + _dot(tri, logf, HI)
    carry[...] = c[tm - 1:tm, :]
    c3 = jnp.concatenate(_split3(c * LOG2E), axis=-1)
    place = pl_ref[...]

    def spread(x):
        xb = x.astype(BF16)
        return jnp.concatenate([_dot(xb[:, i * LANES:(i + 1) * LANES], place) for i in range(x.shape[1] // LANES)],
                               axis=-1)

    kn_ref[...] = kn
    lf_ref[...] = logf
    qa_ref[...] = (spread(qn * (LOG2E * HEAD_DIM ** -0.5)) + _dot(c3, sq_ref[...]) + cq_ref[...]).astype(BF16)
    ka_ref[...] = (spread(kn) + _dot(c3, sk_ref[...]) + ck_ref[...]).astype(BF16)
    va_ref[...] = (spread(v_ref[...]) + cv_ref[...]).astype(BF16)


def _fox_aug_consts(fw):
    nh = fw // HEAD_DIM
    rows = jnp.arange(LANES)
    cols = (rows // HEAD_DIM) * LANES + rows % HEAD_DIM
    place = jnp.zeros((LANES, 2 * LANES), F32).at[rows, cols].set(1.0).astype(BF16)
    h = jnp.arange(nh)
    sq = jnp.zeros((3 * LANES, nh * LANES), F32)
    sk = jnp.zeros((3 * LANES, nh * LANES), F32)
    cq = jnp.zeros((1, nh * LANES), F32)
    ck = jnp.zeros((1, nh * LANES), F32)
    cv = jnp.zeros((1, nh * LANES), F32).at[0, h * LANES + HEAD_DIM].set(1.0)
    for part in range(3):
        sq = sq.at[part * LANES + h, h * LANES + HEAD_DIM + part].set(1.0)
        sk = sk.at[part * LANES + h, h * LANES + HEAD_DIM + 3 + part].set(-1.0)
        cq = cq.at[0, h * LANES + HEAD_DIM + 3 + part].set(1.0)
        ck = ck.at[0, h * LANES + HEAD_DIM + part].set(1.0)
    return place, sq.astype(BF16), sk.astype(BF16), cq, ck, cv


def _fox_prep(z, fb, qg, kg, *, fw, q_col, k_col, v_col, f_col, aug, valid_rows=None):
    m = z.shape[0]
    tm = _pick_tile(m, 256)
    qb, kb, vblk, fblk = q_col // fw, k_col // fw, v_col // fw, f_col // LANES
    in_specs = [pl.BlockSpec((tm, fw), lambda i: (i, qb)),
                pl.BlockSpec((tm, fw), lambda i: (i, kb)),
                pl.BlockSpec((tm, LANES), lambda i: (i, fblk)),
                pl.BlockSpec((1, LANES), lambda i: (0, 0)),
                pl.BlockSpec((1, fw), lambda i: (0, 0)),
                pl.BlockSpec((1, fw), lambda i: (0, 0))]
    args = [z, z, z, fb, qg, kg]
    row = lambda w: pl.BlockSpec((tm, w), lambda i: (i, 0))
    if aug is None:
        return pl.pallas_call(
            functools.partial(_fox_prep_kernel, augment=False),
            out_shape=[jax.ShapeDtypeStruct((m, fw), F32), jax.ShapeDtypeStruct((m, fw), F32),
                       jax.ShapeDtypeStruct((m, LANES), F32)],
            grid=(m // tm,), in_specs=in_specs, out_specs=[row(fw), row(fw), row(LANES)],
            compiler_params=_cp(("parallel",)), name="fox_prep_s",
        )(*args)
    in_specs.append(pl.BlockSpec((tm, fw), lambda i: (i, vblk)))
    args.append(z)
    for a in aug:
        in_specs.append(pl.BlockSpec(a.shape, lambda i: (0, 0)))
        args.append(a)
    nh = fw // HEAD_DIM
    wide = jax.ShapeDtypeStruct((m, nh * LANES), BF16)
    return pl.pallas_call(
        functools.partial(_fox_prep_kernel, augment=True),
        out_shape=[jax.ShapeDtypeStruct((valid_rows or m, fw), F32), jax.ShapeDtypeStruct((m, LANES), F32),
                   wide, wide, wide],
        grid=(m // tm,), in_specs=in_specs,
        out_specs=[row(fw), row(LANES), row(nh * LANES), row(nh * LANES), row(nh * LANES)],
        scratch_shapes=[pltpu.VMEM((1, LANES), F32)],
        compiler_params=_cp(("arbitrary",)), name="fox_prep_p",
    )(*args)


def _flash_kernel(q_ref, k_ref, v_ref, g_ref, o_ref, m_sc, acc_sc, sa_sc, sb_sc, *, tq, tk):
    qi = pl.program_id(1)
    nsub = tq // tk
    m_sc[...] = jnp.full_like(m_sc, NEG_BIG)
    acc_sc[...] = jnp.zeros_like(acc_sc)
    heads = [slice(hh * LANES, (hh + 1) * LANES) for hh in range(2)]

    def key_off(g, j):
        return pl.multiple_of(g * tq + j * tk, tk)

    def scores(g, buf):
        for hh, sl in enumerate(heads):
            for j in range(nsub):
                buf[hh, j] = _dot_nt(q_ref[:, sl], k_ref[pl.ds(key_off(g, j), tk), sl])
                yield

    def update(hh, g, buf, masks):
        sl = heads[hh]
        ss = [buf[hh, j] for j in range(nsub)]
        if masks is not None:
            ss = [jnp.where(mk, s, NEG_BIG) for s, mk in zip(ss, masks)]
        m_old = m_sc[hh]
        mx = ss[0]
        for s in ss[1:]:
            mx = jnp.maximum(mx, s)
        m_new = jnp.maximum(m_old, jnp.broadcast_to(jnp.max(mx, axis=-1, keepdims=True), m_old.shape))
        yield
        m_rep = jnp.concatenate([m_new] * (tk // LANES), axis=-1)
        acc = acc_sc[hh] * jnp.exp2(m_old - m_new)
        for j, s in enumerate(ss):
            lo = j * tk if masks is not None else 0
            pv = _dot(jnp.exp2(s[lo:] - m_rep[lo:]).astype(BF16), v_ref[pl.ds(key_off(g, j), tk), sl])
            acc = acc + pv if lo == 0 else jnp.concatenate([acc[:lo], acc[lo:] + pv], axis=0)
            yield
        m_sc[hh] = m_new
        acc_sc[hh] = acc

    def step(g, cur, nxt, masks):
        gens = [] if nxt is None else [scores(g + 1, nxt)]
        gens += [update(hh, g, cur, masks) for hh in range(2)]
        _run_interleaved(gens)

    rows = _iota((tq, tk), 0)
    cols = _iota((tq, tk), 1)
    causal = [rows >= cols + j * tk for j in range(nsub)]

    _run_interleaved([scores(0, sa_sc)])

    def body(i, carry):
        step(2 * i, sa_sc, sb_sc, None)
        step(2 * i + 1, sb_sc, sa_sc, None)
        return carry

    lax.fori_loop(0, qi // 2, body, 0)

    @pl.when(qi % 2 == 1)
    def _():
        step(qi - 1, sa_sc, sb_sc, None)
        step(qi, sb_sc, None, causal)

    @pl.when(qi % 2 == 0)
    def _():
        step(qi, sa_sc, None, causal)

    outs = []
    for hh in range(2):
        acc = acc_sc[hh]
        outs.append(acc / acc[:, HEAD_DIM:HEAD_DIM + 1])
    o = jnp.where(_iota((1, LANES), 1) < HEAD_DIM, outs[0], pltpu.roll(outs[1], HEAD_DIM, 1))
    o_ref[...] = (o * _silu(g_ref[...])).astype(o_ref.dtype)


def _flash(qa, ka, va, z, g_col):
    lp = qa.shape[0]
    npair = qa.shape[1] // (2 * LANES)
    tk = 256
    tq = _pick_tile(lp, 768, tk)
    gblk = g_col // LANES
    return pl.pallas_call(
        functools.partial(_flash_kernel, tq=tq, tk=tk),
        out_shape=jax.ShapeDtypeStruct((lp, npair * LANES), BF16),
        grid=(npair, lp // tq),
        in_specs=[pl.BlockSpec((tq, 2 * LANES), lambda p, i: (i, p)),
                  pl.BlockSpec((lp, 2 * LANES), lambda p, i: (0, p)),
                  pl.BlockSpec((lp, 2 * LANES), lambda p, i: (0, p)),
                  pl.BlockSpec((tq, LANES), lambda p, i: (i, gblk + p))],
        out_specs=pl.BlockSpec((tq, LANES), lambda p, i: (i, p)),
        scratch_shapes=[pltpu.VMEM((2, tq, LANES), F32), pltpu.VMEM((2, tq, LANES), F32),
                        pltpu.VMEM((2, tq // tk, tq, tk), F32), pltpu.VMEM((2, tq // tk, tq, tk), F32)],
        compiler_params=_cp(("parallel", "arbitrary")),
        name="fox_flash",
    )(qa, ka, va, z)


def _lfpool_kernel(x_ref, o_ref):
    n = x_ref.shape[1]
    jr = _iota((n, 2 * n), 0)
    jc = _iota((n, 2 * n), 1)
    tri2 = ((jr > jc) | (jc == n)).astype(BF16)
    hi, mid, lo = _split3(x_ref[...])
    o_ref[...] = _dot(hi, tri2) + _dot(mid, tri2) + _dot(lo, tri2)


def _lfpool(lft):
    rws, n = lft.shape
    tm = _pick_tile(rws, 2048)
    return pl.pallas_call(
        _lfpool_kernel,
        out_shape=jax.ShapeDtypeStruct((rws, 2 * n), F32),
        grid=(rws // tm,),
        in_specs=[pl.BlockSpec((tm, n), lambda i: (i, 0))],
        out_specs=pl.BlockSpec((tm, 2 * n), lambda i: (i, 0)),
        compiler_params=_cp(("parallel",)), name="fox_lfpool",
    )(lft)


def _decode_kernel(*refs, ts, nheads, pps):
    q_ref = refs[1]
    k_refs = refs[2:2 + pps]
    v_refs = refs[2 + pps:2 + 2 * pps]
    f_refs = refs[2 + 2 * pps:2 + 3 * pps]
    kn_ref, vn_ref, lfn_ref, g_ref, o_ref, m_sc, l_sc, acc_sc, r_sc = refs[2 + 3 * pps:]
    j = pl.program_id(1)
    nj = pl.num_programs(1)
    nrow = nheads * ts
    page = k_refs[0].shape[-1]
    fw = nheads * HEAD_DIM

    @pl.when(j == 0)
    def _():
        m_sc[...] = jnp.full_like(m_sc, NEG_BIG)
        l_sc[...] = jnp.zeros_like(l_sc)
        acc_sc[...] = jnp.zeros_like(acc_sc)
        r_sc[...] = jnp.zeros_like(r_sc)

    q = q_ref[...]

    def per_row(x):
        return jnp.concatenate([x] * ts, axis=0)

    def update(blocks):
        ss = [_dot(q, kt.astype(BF16)) + bias for kt, _, bias in blocks]
        m_run, l_run, acc = m_sc[...], l_sc[...], acc_sc[...]
        half = (len(blocks) + 1) // 2
        for lo in range(0, len(blocks), half):
            m_new = m_run
            for s in ss[lo:lo + half]:
                m_new = jnp.maximum(m_new, jnp.max(s, axis=-1, keepdims=True))
            alpha = jnp.exp(m_run - m_new)
            l_run = alpha * l_run
            acc = acc * alpha
            for s, (_, vt, _) in zip(ss[lo:lo + half], blocks[lo:lo + half]):
                p = jnp.exp(s - m_new)
                l_run = l_run + jnp.sum(p, axis=-1, keepdims=True)
                acc = acc + _dot_nt(p.astype(BF16), vt.astype(BF16))
            m_run = m_new
        l_sc[...] = l_run
        acc_sc[...] = acc
        m_sc[...] = m_run

    run = r_sc[...]
    blocks = []
    for k_ref, v_ref, f_ref in zip(k_refs, v_refs, f_refs):
        f = f_ref[...]
        blocks.append((k_ref[...].reshape(fw, page), v_ref[...].reshape(fw, page),
                       per_row(f[:, :page] + run)))
        run = run + f[:, page:page + 1]
    r_sc[...] = run
    update(blocks)

    @pl.when(j == nj - 1)
    def _():
        nn = kn_ref.shape[-1]
        upper = (_iota((nn, nn), 0) <= _iota((nn, nn), 1)).astype(F32)
        cum = per_row(_dot(lfn_ref[...], upper, HI))
        tq = _iota((nrow, nn), 0) // nheads
        tk = _iota((nrow, nn), 1)
        update([(kn_ref[...], vn_ref[...], jnp.where((tk <= tq) & (tk < ts), -cum, NEG_BIG))])
        out = acc_sc[...] / l_sc[...]
        own = (_iota((nrow, fw), 0) % nheads) == (_iota((nrow, fw), 1) // HEAD_DIM)
        sel = (_iota((8, nrow), 1) // nheads == _iota((8, nrow), 0)).astype(F32)
        o = _dot(sel, jnp.where(own, out, 0.0), HI)
        o_ref[...] = (o * _silu(g_ref[...])).astype(o_ref.dtype)


def _decode(page_table, qbd, kt_pool, vt_pool, lfo, layer, knew_t, vnew_t, lfn_t, gate, *, ts, nheads, pps):
    bsz, nrow, fw = qbd.shape
    npg = page_table.shape[1]
    page = kt_pool.shape[-1]
    nn = knew_t.shape[-1]

    def pool(shape, which):
        nd = len(shape)
        return pl.BlockSpec((None, None) + shape,
                            lambda b, j, pt: (layer, pt[b, npg - 1 - pps * j - which]) + (0,) * nd)

    per_b = lambda w0, w1: pl.BlockSpec((None, w0, w1), lambda b, j, pt: (b, 0, 0))
    kv = (nheads, HEAD_DIM, page)
    each = range(pps)
    grid_spec = pltpu.PrefetchScalarGridSpec(
        num_scalar_prefetch=1, grid=(bsz, npg // pps),
        in_specs=([per_b(nrow, fw)] + [pool(kv, w) for w in each] + [pool(kv, w) for w in each]
                  + [pool((nheads, 2 * page), w) for w in each]
                  + [per_b(fw, nn), per_b(fw, nn), per_b(nheads, nn), per_b(8, fw)]),
        out_specs=per_b(8, fw),
        scratch_shapes=[pltpu.VMEM((nrow, 1), F32), pltpu.VMEM((nrow, 1), F32), pltpu.VMEM((nrow, fw), F32),
                        pltpu.VMEM((nheads, 1), F32)])
    return pl.pallas_call(
        functools.partial(_decode_kernel, ts=ts, nheads=nheads, pps=pps),
        out_shape=jax.ShapeDtypeStruct((bsz, 8, fw), BF16),
        grid_spec=grid_spec,
        compiler_params=_cp(("parallel", "arbitrary")),
        name="fox_decode",
    )(page_table, qbd, *([kt_pool] * pps), *([vt_pool] * pps), *([lfo] * pps), knew_t, vnew_t, lfn_t, gate)


def _merge_kernel(x_ref, or_ref, of_ref, mr_ref, mf_ref, wpr_ref, wpf_ref, wo_ref, o_ref):
    a = _dot(or_ref[...], wpr_ref[...])
    b = _dot(of_ref[...], wpf_ref[...])
    mixed = jax.nn.sigmoid(mr_ref[...]) * a + jax.nn.sigmoid(mf_ref[...]) * b
    o_ref[...] = x_ref[...] + _dot(mixed.astype(BF16), wo_ref[...])


def _merge(x, o_r, o_f, z, mr_col, mf_col, wpr, wpf, wo):
    m, d = x.shape
    rw = o_r.shape[1]
    fw = o_f.shape[1]
    tm = _pick_tile(m, 256)
    whole = lambda a: pl.BlockSpec(a.shape, lambda i: (0, 0), pipeline_mode=pl.Buffered(1))
    return pl.pallas_call(
        _merge_kernel,
        out_shape=jax.ShapeDtypeStruct((m, d), F32),
        grid=(m // tm,),
        in_specs=[pl.BlockSpec((tm, d), lambda i: (i, 0)),
                  pl.BlockSpec((tm, rw), lambda i: (i, 0)),
                  pl.BlockSpec((tm, fw), lambda i: (i, 0)),
                  pl.BlockSpec((tm, d), lambda i: (i, mr_col // d)),
                  pl.BlockSpec((tm, d), lambda i: (i, mf_col // d)),
                  whole(wpr), whole(wpf), whole(wo)],
        out_specs=pl.BlockSpec((tm, d), lambda i: (i, 0)),
        compiler_params=_cp(("parallel",)),
        name="merge",
    )(x, o_r, o_f, z, z, wpr, wpf, wo)


def _pad_cols(a, width):
    return jnp.pad(a, [(0, 0)] * (a.ndim - 1) + [(0, width - a.shape[-1])])


def kernel(x_prompt, x_sample, cache_k, cache_v, cache_logf, state_wkv, state_shift, page_table, meta_tokens, norm_gain, w_in, r_mu, r_w0, r_w2, r_a0, r_a2, r_v0, r_v1, r_v2, r_kk, r_ka, r_rk, r_lnx_w, r_lnx_b, f_bias, f_qgain, f_kgain, w_proj_r, w_proj_f, w_out):
    bsz, seq, d = x_prompt.shape
    db, ts, _ = x_sample.shape
    depth = w_in.shape[0]
    rw = r_w0.shape[1]
    rh = r_rk.shape[1]
    fh = f_bias.shape[1]
    fw = fh * HEAD_DIM
    lora_w = r_w2.shape[1]
    lora_a = r_a2.shape[1]
    shift_w = 3 * rw + lora_w + lora_a
    page = cache_k.shape[2]
    npool = cache_k.shape[1]
    npg = page_table.shape[1]
    assert bsz == 1 and rw == rh * HEAD_DIM and fw == rw and d == 2 * rw
    assert lora_w <= LANES and lora_a <= LANES and fh <= LANES and ts * fh <= LANES
    pps = next(n for n in (8, 4, 2, 1) if npg % n == 0)

    s_gr = shift_w
    s_q = s_gr + rw
    s_k = s_q + fw
    s_v = s_k + fw
    s_f = s_v + fw
    s_gf = s_f + fh
    s_mr = s_gf + fw
    s_mf = s_mr + d
    c_gr = 3 * rw
    c_q = c_gr + rw
    c_k = c_q + fw
    c_v = c_k + fw
    c_gf = c_v + fw
    c_mr = c_gf + fw
    c_mf = c_mr + d
    c_wa = c_mf + d
    c_f = c_wa + 2 * LANES
    n_pack = _round_up(c_f + LANES, INPROJ_TN)

    segs = ((0, 0, 3 * rw), (s_gr, c_gr, rw), (s_q, c_q, fw), (s_k, c_k, fw), (s_v, c_v, fw), (s_gf, c_gf, fw),
            (s_mr, c_mr, d), (s_mf, c_mf, d), (3 * rw, c_wa, lora_w), (3 * rw + lora_w, c_wa + LANES, lora_a),
            (s_f, c_f, fh))
    w_pack = _pack_weights(jnp.swapaxes(w_in, 1, 2), segs, c_wa, n_pack)

    def pack_shift(a):
        return a[..., :3 * rw], jnp.concatenate([_pad_cols(a[..., 3 * rw:3 * rw + lora_w], LANES),
                                                 _pad_cols(a[..., 3 * rw + lora_w:], LANES)], axis=-1)

    def unpack_shift(zrow):
        return jnp.concatenate([zrow[..., :3 * rw], zrow[..., c_wa:c_wa + lora_w],
                                zrow[..., c_wa + LANES:c_wa + LANES + lora_a]], axis=-1)

    length = seq + N_META
    lp = _round_up(length, 768) if length > 768 else _round_up(length, 256)
    chunk = 64
    xp = jnp.concatenate([meta_tokens.astype(F32), x_prompt[0], jnp.zeros((lp - length, d), F32)], axis=0)
    xs = x_sample.reshape(db * ts, d)
    aug = _fox_aug_consts(fw)
    s0_p = jnp.zeros((1, rh // 2, LANES, LANES), F32)

    kt_pool = jnp.transpose(cache_k, (0, 1, 3, 4, 2))
    vt_pool = jnp.transpose(cache_v, (0, 1, 3, 4, 2))
    lft = jnp.swapaxes(cache_logf, 2, 3).reshape(depth * npool * fh, page)
    lfo = _lfpool(lft).reshape(depth, npool, fh, 2 * page)

    vf_p = None
    vf_s = None
    outs = {n: [] for n in ("kp", "vp", "lp", "sp", "hp", "ks", "vs", "ls", "ss", "hs")}
    for l in range(depth):
        gain = norm_gain[l][None, :]
        mu3, muw = pack_shift(r_mu[l][None, :])
        if l == 0:
            v0 = jnp.zeros((1, rw), F32)
            v1 = jnp.zeros((rw, LANES), BF16)
            v2 = jnp.zeros((LANES, rw), BF16)
        else:
            v0 = r_v0[l - 1][None, :]
            v1 = _pad_cols(r_v1[l - 1], LANES).astype(BF16)
            v2 = jnp.pad(r_v2[l - 1], ((0, LANES - r_v2.shape[1]), (0, 0))).astype(BF16)
        prm = (mu3, muw, r_w0[l][None, :],
               jnp.pad(r_w2[l], ((0, LANES - lora_w), (0, 0))).astype(BF16),
               r_a0[l][None, :],
               jnp.pad(r_a2[l], ((0, LANES - lora_a), (0, 0))).astype(BF16),
               r_kk[l][None, :], r_ka[l][None, :], v0, v1, v2)
        rk = r_rk[l].reshape(1, rw)
        lnw = r_lnx_w[l][None, :]
        lnb = r_lnx_b[l][None, :]
        fb = _pad_cols(f_bias[l][None, :], LANES)
        qg = jnp.tile(f_qgain[l], fh)[None, :]
        kg = jnp.tile(f_kgain[l], fh)[None, :]
        wpr = w_proj_r[l].astype(BF16)
        wpf = w_proj_f[l].astype(BF16)
        wo = w_out[l].astype(BF16)

        z = _inproj(xp, gain, w_pack, l)
        res = _wkv_fused(z, vf_p, prm, s0_p, rk, lnw, lnb, rw=rw, wa_col=c_wa, gate_col=c_gr, chunk=chunk,
                         chunks_per_step=2, valid_len=length, emit_v=(l == 0 and depth > 1))
        o_r, s_fin = res[0], res[-1]
        if l == 0 and depth > 1:
            vf_p = res[1]
        kn, lf, qa, ka, va = _fox_prep(z, fb, qg, kg, fw=fw, q_col=c_q, k_col=c_k, v_col=c_v, f_col=c_f, aug=aug,
                                       valid_rows=length)
        o_f = _flash(qa, ka, va, z, c_gf)
        xp = _merge(xp, o_r, o_f, z, c_mr, c_mf, wpr, wpf, wo)
        outs["kp"].append(kn.reshape(1, length, fh, HEAD_DIM))
        outs["vp"].append(z[:length, c_v:c_v + fw].reshape(1, length, fh, HEAD_DIM))
        outs["lp"].append(lf[:length, :fh][None])
        outs["sp"].append(_unpack_state(s_fin))
        outs["hp"].append(unpack_shift(z[length - 1:length]))

        zs = _inproj(xs, gain, w_pack, l)
        zs3 = zs.reshape(db, ts, n_pack)
        sh3, shw = pack_shift(state_shift[l])
        prev3 = jnp.concatenate([sh3[:, None], zs3[:, :-1, :3 * rw]], axis=1).reshape(db * ts, 3 * rw)
        prevw = jnp.concatenate([shw[:, None], zs3[:, :-1, c_wa:c_wa + 2 * LANES]], axis=1).reshape(db * ts, 2 * LANES)
        res = _rwkv_prep(zs, prev3, prevw, vf_s, prm, rw=rw, wa_col=c_wa)
        if l == 0:
            vf_s = res[3]
        pad_t = lambda a: jnp.pad(a.reshape(db, ts, -1), ((0, 0), (0, 8 - ts), (0, 0)))
        r, lw, k2, v, kk, b = (pad_t(a) for a in res)
        gate_r = pad_t(zs[:, c_gr:c_gr + rw])
        o_r, s_fin = _wkv(r, lw, k2, v, kk, b, gate_r, 0, _pack_state(state_wkv[l]), rk, lnw, lnb,
                          chunk=8, chunks_per_step=1, valid_len=None, pairs_per_step=rw // LANES)
        o_r = o_r[:, :ts].reshape(db * ts, rw)
        qn, kn, lf = _fox_prep(zs, fb, qg, kg, fw=fw, q_col=c_q, k_col=c_k, v_col=c_v, f_col=c_f, aug=None)
        vn = zs[:, c_v:c_v + fw]
        q4 = qn.reshape(db, ts, fh, HEAD_DIM) * HEAD_DIM ** -0.5
        qbd = jnp.einsum("bthd,hg->bthgd", q4, jnp.eye(fh, dtype=F32)).reshape(db, ts * fh, fw).astype(BF16)
        pad_k = lambda a: _pad_cols(jnp.swapaxes(a.reshape(db, ts, -1), 1, 2), page)
        gate_f = pad_t(zs[:, c_gf:c_gf + fw])
        o_f = _decode(page_table, qbd, kt_pool, vt_pool, lfo, l, pad_k(kn), pad_k(vn), pad_k(lf[:, :fh]), gate_f,
                      ts=ts, nheads=fh, pps=pps)
        o_f = o_f[:, :ts].reshape(db * ts, fw)
        xs = _merge(xs, o_r, o_f, zs, c_mr, c_mf, wpr, wpf, wo)
        outs["ks"].append(kn.reshape(db, ts, fh, HEAD_DIM))
        outs["vs"].append(vn.reshape(db, ts, fh, HEAD_DIM))
        outs["ls"].append(lf[:, :fh].reshape(db, ts, fh))
        outs["ss"].append(_unpack_state(s_fin))
        outs["hs"].append(unpack_shift(zs3[:, -1]))

    y_prompt = xp[N_META:length][None]
    y_sample = xs.reshape(db, ts, d)
    st = lambda n: jnp.stack(outs[n])
    return (y_prompt, y_sample, st("kp"), st("vp"), st("lp"), st("sp"), st("hp"),
            st("ks"), st("vs"), st("ls"), st("ss"), st("hs"))
```
